```python
import math
import jax, jax.numpy as jnp
from jax import lax
import numpy as np


D_MODEL = 1024
BATCH = 8
SEQ = 2048
DEPTH = 1
DEC_BATCH = 128
DEC_SEQ = 8
PAST_LEN = 8192
PAGE_SIZE = 128

D_MIX = D_MODEL
D_A = D_MIX // 2
H_A = 8
HD_A = D_A // H_A
CHUNK = 128
D_B = D_MIX - D_A
H_B = 8
HD = D_B // H_B
KV_B = 2
G_B = H_B // KV_B
KV_W = KV_B * HD
CMP_STRIDE = 16
CMP_BLK = 2 * CMP_STRIDE
CMP_HID = 128
SEL_BLK = 64
N_SEL = 16
N_LOCAL = 2
WINDOW = 512
Q_BLOCK = 128
EPS = 1e-6
N_IN = 3 * D_A + 2 * D_B + 3 * H_B + 6 * KV_W

kernel_name = "hymba_gmlp_nsa_decode_step"


def rmsnorm(x, g):
    xf = x.astype(jnp.float32)
    xf = xf * lax.rsqrt(jnp.mean(xf * xf, axis=-1, keepdims=True) + EPS)
    return xf.astype(x.dtype) * g


def layernorm(x, g, b):
    xf = x.astype(jnp.float32)
    xc = xf - jnp.mean(xf, axis=-1, keepdims=True)
    xf = xc * lax.rsqrt(jnp.mean(xc * xc, axis=-1, keepdims=True) + EPS)
    return xf.astype(x.dtype) * g + b


def masked_softmax(s, mask):
    p = jax.nn.softmax(jnp.where(mask, s, -1e30), axis=-1)
    return jnp.where(mask, p, 0.0)


def alibi_slopes():
    return 2.0 ** (-8.0 * jnp.arange(1, H_B + 1, dtype=jnp.float32) / H_B)


def project(x, norm_in, w_in, b_gate, q_norm, k_norm_sel, k_norm_win):
    B, T, _ = x.shape
    h = rmsnorm(x, norm_in) @ w_in
    sizes = [D_A, D_A, D_A, D_B, D_B, 3 * H_B] + [KV_W] * 6
    cuts = [int(c) for c in np.cumsum(sizes)[:-1]]
    u, v, z_a, q, z_b, gl, k_c, v_c, k_s, v_s, k_w, v_w = jnp.split(h, cuts, axis=-1)
    heads = lambda t, n: t.reshape(B, T, n, HD)
    q = rmsnorm(heads(q, H_B), q_norm)
    gates = jax.nn.sigmoid(gl + b_gate).reshape(B, T, 3, H_B)
    return (u, v, z_a, q, z_b, gates,
            heads(k_c, KV_B), heads(v_c, KV_B),
            rmsnorm(heads(k_s, KV_B), k_norm_sel), heads(v_s, KV_B),
            rmsnorm(heads(k_w, KV_B), k_norm_win), heads(v_w, KV_B))


def chunk_gmlp(u, v, z, ln_g, ln_b, w_s, b_s, out_g):
    B, T, _ = u.shape
    C = min(T, CHUNK)
    vn = layernorm(v, ln_g, ln_b)
    w = jnp.where(jnp.tril(jnp.ones((C, C), dtype=bool)), w_s[:, :C, :C], 0.0)
    vc = vn.reshape(B, T // C, C, H_A, HD_A)
    s = jnp.einsum('hij,bcjhd->bcihd', w, vc) + b_s[:, :C].T[None, None, :, :, None]
    o = rmsnorm(u * s.reshape(B, T, D_A), out_g) * jax.nn.silu(z)
    return o, vn


def compress(rows, pe, w1, w2):
    B, L = rows.shape[:2]
    nc = L // CMP_STRIDE
    ch = rows[:, :nc * CMP_STRIDE].reshape(B, nc, CMP_STRIDE, KV_B, HD)
    w1 = w1.reshape(CMP_BLK, HD, CMP_HID)
    pe_bias = jnp.einsum('jd,jdh->h', pe, w1)
    h_first = jnp.einsum('bcjkd,jdh->bckh', ch[:, :-1], w1[:CMP_STRIDE])
    h_second = jnp.einsum('bcjkd,jdh->bckh', ch[:, 1:], w1[CMP_STRIDE:])
    return jax.nn.silu(h_first + h_second + pe_bias) @ w2


def nsa_attend(q, qpos, kc, vc, kc_end, kw, vw, kw_pos, fetch_sel, gates, n_sb, slopes):
    B, Tq = q.shape[:2]
    f32 = jnp.float32
    scale = HD ** -0.5
    qg = q.reshape(B, Tq, KV_B, G_B, HD)
    m = slopes.reshape(KV_B, G_B)[:, :, None]
    dist_c = (qpos[:, None] - kc_end[None, :]).astype(f32)
    s_c = jnp.einsum('bqkgd,bnkd->bqkgn', qg, kc).astype(f32) * scale - m * dist_c[:, None, None, :]
    p_c = masked_softmax(s_c, (dist_c >= 0)[:, None, None, :])
    o_c = jnp.einsum('bqkgn,bnkd->bqkgd', p_c.astype(vc.dtype), vc)
    ratio = SEL_BLK // CMP_STRIDE
    imp = p_c.sum(axis=3)
    imp = jnp.pad(imp, ((0, 0), (0, 0), (0, 0), (0, n_sb * ratio - imp.shape[-1])))
    imp = imp.reshape(B, Tq, KV_B, n_sb, ratio).sum(-1)
    cur = qpos // SEL_BLK
    j = jnp.arange(n_sb)
    future = j[None, :] > cur[:, None]
    forced = (j[None, :] == 0) | (j[None, :] > cur[:, None] - N_LOCAL)
    score = jnp.where(future[None, :, None, :], -1e9, jnp.where(forced[None, :, None, :], 1e9, imp))
    _, idx = lax.top_k(score, min(N_SEL, n_sb))
    pos = (idx[..., None] * SEL_BLK + jnp.arange(SEL_BLK)).reshape(B, Tq, KV_B, -1)
    ks, vs = fetch_sel(pos)
    dist_s = (qpos[None, :, None, None] - pos).astype(f32)
    s_s = jnp.einsum('bqkgd,bqksd->bqkgs', qg, ks).astype(f32) * scale - m * dist_s[:, :, :, None, :]
    p_s = masked_softmax(s_s, (dist_s >= 0)[:, :, :, None, :])
    o_s = jnp.einsum('bqkgs,bqksd->bqkgd', p_s.astype(vs.dtype), vs)
    dist_w = (qpos[:, None] - kw_pos[None, :]).astype(f32)
    mask_w = (dist_w >= 0) & (dist_w < WINDOW) & (kw_pos[None, :] >= 0)
    s_w = jnp.einsum('bqkgd,bskd->bqkgs', qg, kw).astype(f32) * scale - m * dist_w[:, None, None, :]
    p_w = masked_softmax(s_w, mask_w[:, None, None, :])
    o_w = jnp.einsum('bqkgs,bskd->bqkgd', p_w.astype(vw.dtype), vw)
    g = gates.reshape(B, Tq, 3, KV_B, G_B)[..., None]
    o = g[:, :, 0] * o_c + g[:, :, 1] * o_s + g[:, :, 2] * o_w
    return o.reshape(B, Tq, D_B)


def merge(x, o_a, o_b, z_b, out_norm_b, w_out):
    o_b = rmsnorm(o_b, out_norm_b) * jax.nn.silu(z_b)
    return x + jnp.concatenate([o_a, o_b], axis=-1) @ w_out


def layer_prompt(x, lw, slopes):
    (norm_in, w_in, b_gate, ln_v_g, ln_v_b, w_spatial, b_spatial, q_norm, k_norm_cmp, k_norm_sel,
     k_norm_win, cmp_pe_k, cmp_pe_v, w_cmp_k1, w_cmp_k2, w_cmp_v1, w_cmp_v2, out_norm_a, out_norm_b, w_out) = lw
    B, T, _ = x.shape
    u, v, z_a, q, z_b, gates, kc_r, vc_r, ks, vs, kw, vw = project(
        x, norm_in, w_in, b_gate, q_norm, k_norm_sel, k_norm_win)
    o_a, _ = chunk_gmlp(u, v, z_a, ln_v_g, ln_v_b, w_spatial, b_spatial, out_norm_a)
    kc = rmsnorm(compress(kc_r, cmp_pe_k, w_cmp_k1, w_cmp_k2), k_norm_cmp)
    vc = compress(vc_r, cmp_pe_v, w_cmp_v1, w_cmp_v2)
    kc_end = jnp.arange(kc.shape[1]) * CMP_STRIDE + CMP_BLK - 1
    n_sb = -(-T // SEL_BLK)
    pad = ((0, 0), (WINDOW, 0), (0, 0), (0, 0))
    kw_pad, vw_pad = jnp.pad(kw, pad), jnp.pad(vw, pad)
    b_ix = jnp.arange(B)[:, None, None, None]
    g_ix = jnp.arange(KV_B)[None, None, :, None]

    def fetch(pos):
        pc = jnp.clip(pos, 0, T - 1)
        return ks[b_ix, pc, g_ix], vs[b_ix, pc, g_ix]

    def q_block(c):
        start = c * Q_BLOCK
        qpos = start + jnp.arange(Q_BLOCK)
        kw_pos = start - WINDOW + jnp.arange(WINDOW + Q_BLOCK)
        sl = lambda t, n: lax.dynamic_slice_in_dim(t, start, n, axis=1)
        return nsa_attend(sl(q, Q_BLOCK), qpos, kc, vc, kc_end,
                          sl(kw_pad, WINDOW + Q_BLOCK), sl(vw_pad, WINDOW + Q_BLOCK), kw_pos,
                          fetch, sl(gates, Q_BLOCK), n_sb, slopes)

    o_b = lax.map(q_block, jnp.arange(T // Q_BLOCK))
    o_b = jnp.moveaxis(o_b, 0, 1).reshape(B, T, D_B)
    y = merge(x, o_a, o_b, z_b, out_norm_b, w_out)
    n_keep = min(WINDOW, T)
    return y, (kc_r, vc_r, ks, vs, kw[:, T - n_keep:], vw[:, T - n_keep:])


def layer_sample(x, l, cache_k_cmp, cache_v_cmp, cache_k_sel, cache_v_sel, k_win_buf, v_win_buf,
                 page_table, lw, slopes):
    (norm_in, w_in, b_gate, ln_v_g, ln_v_b, w_spatial, b_spatial, q_norm, k_norm_cmp, k_norm_sel,
     k_norm_win, cmp_pe_k, cmp_pe_v, w_cmp_k1, w_cmp_k2, w_cmp_v1, w_cmp_v2, out_norm_a, out_norm_b, w_out) = lw
    B, T, _ = x.shape
    n_pages = page_table.shape[1]
    page = cache_k_cmp.shape[2]
    past_len = n_pages * page
    u, v, z_a, q, z_b, gates, kc_r, vc_r, ks, vs, kw, vw = project(
        x, norm_in, w_in, b_gate, q_norm, k_norm_sel, k_norm_win)
    o_a, v_state = chunk_gmlp(u, v, z_a, ln_v_g, ln_v_b, w_spatial, b_spatial, out_norm_a)
    gather_past = lambda c: c[l, page_table].reshape(B, past_len, KV_B, HD)
    kc_all = jnp.concatenate([gather_past(cache_k_cmp), kc_r], axis=1)
    vc_all = jnp.concatenate([gather_past(cache_v_cmp), vc_r], axis=1)
    kc = rmsnorm(compress(kc_all, cmp_pe_k, w_cmp_k1, w_cmp_k2), k_norm_cmp)
    vc = compress(vc_all, cmp_pe_v, w_cmp_v1, w_cmp_v2)
    kc_end = jnp.arange(kc.shape[1]) * CMP_STRIDE + CMP_BLK - 1
    n_sb = -(-(past_len + T) // SEL_BLK)
    b_ix = jnp.arange(B)[:, None, None, None]
    g_ix = jnp.arange(KV_B)[None, None, :, None]

    def fetch(pos):
        in_past = (pos < past_len)[..., None]
        pp = jnp.clip(pos, 0, past_len - 1)
        phys = page_table[b_ix, pp // page]
        off = pp % page
        pn = jnp.clip(pos - past_len, 0, T - 1)
        k = jnp.where(in_past, cache_k_sel[l, phys, off, g_ix], ks[b_ix, pn, g_ix])
        v_ = jnp.where(in_past, cache_v_sel[l, phys, off, g_ix], vs[b_ix, pn, g_ix])
        return k, v_

    n_buf = k_win_buf.shape[1]
    kw_all = jnp.concatenate([k_win_buf, kw], axis=1)
    vw_all = jnp.concatenate([v_win_buf, vw], axis=1)
    kw_pos = past_len - n_buf + jnp.arange(n_buf + T)
    qpos = past_len + jnp.arange(T)
    o_b = nsa_attend(q, qpos, kc, vc, kc_end, kw_all, vw_all, kw_pos, fetch, gates, n_sb, slopes)
    y = merge(x, o_a, o_b, z_b, out_norm_b, w_out)
    return y, (kc_r, vc_r, ks, vs, kw_all[:, T:], vw_all[:, T:], v_state)


def setup_inputs(seed: int = 0) -> dict:
    key = jax.random.key(seed)
    k = jax.random.split(key, 40)
    n_pages = PAST_LEN // PAGE_SIZE
    n_used = DEC_BATCH * n_pages
    n_pool = n_used + n_used // 4
    n_buf = min(WINDOW, PAST_LEN)
    nrm = lambda kk, shape, s=1.0: s * jax.random.normal(kk, shape, jnp.float32)
    gain = lambda kk, shape: 1.0 + 0.02 * jax.random.normal(kk, shape, jnp.float32)
    pool = (DEPTH, n_pool, PAGE_SIZE, KV_B, HD)
    page_table = jax.random.permutation(k[8], n_pool)[:n_used].reshape(DEC_BATCH, n_pages).astype(jnp.int32)
    return {
        "x_prompt": nrm(k[0], (BATCH, SEQ, D_MODEL)),
        "x_sample": nrm(k[1], (DEC_BATCH, DEC_SEQ, D_MODEL)),
        "cache_k_cmp": nrm(k[2], pool),
        "cache_v_cmp": nrm(k[3], pool),
        "cache_k_sel": nrm(k[4], pool),
        "cache_v_sel": nrm(k[5], pool),
        "state_k_win": nrm(k[6], (DEPTH, DEC_BATCH, n_buf, KV_B, HD)),
        "state_v_win": nrm(k[7], (DEPTH, DEC_BATCH, n_buf, KV_B, HD)),
        "page_table": page_table,
        "norm_in": gain(k[9], (DEPTH, D_MODEL)),
        "w_in": nrm(k[10], (DEPTH, D_MODEL, N_IN), D_MODEL ** -0.5),
        "b_gate": nrm(k[11], (DEPTH, 3 * H_B), 0.01),
        "ln_v_g": gain(k[12], (DEPTH, D_A)),
        "ln_v_b": nrm(k[13], (DEPTH, D_A), 0.02),
        "w_spatial": nrm(k[14], (DEPTH, H_A, CHUNK, CHUNK), 0.5 * CHUNK ** -0.5),
        "b_spatial": gain(k[15], (DEPTH, H_A, CHUNK)),
        "q_norm": gain(k[16], (DEPTH, HD)),
        "k_norm_cmp": gain(k[17], (DEPTH, HD)),
        "k_norm_sel": gain(k[18], (DEPTH, HD)),
        "k_norm_win": gain(k[19], (DEPTH, HD)),
        "cmp_pe_k": nrm(k[20], (DEPTH, CMP_BLK, HD), 0.02),
        "cmp_pe_v": nrm(k[21], (DEPTH, CMP_BLK, HD), 0.02),
        "w_cmp_k1": nrm(k[22], (DEPTH, CMP_BLK * HD, CMP_HID), (CMP_BLK * HD) ** -0.5),
        "w_cmp_k2": nrm(k[23], (DEPTH, CMP_HID, HD), CMP_HID ** -0.5),
        "w_cmp_v1": nrm(k[24], (DEPTH, CMP_BLK * HD, CMP_HID), (CMP_BLK * HD) ** -0.5),
        "w_cmp_v2": nrm(k[25], (DEPTH, CMP_HID, HD), CMP_HID ** -0.5),
        "out_norm_a": gain(k[26], (DEPTH, D_A)),
        "out_norm_b": gain(k[27], (DEPTH, D_B)),
        "w_out": nrm(k[28], (DEPTH, D_MIX, D_MODEL), D_MIX ** -0.5),
    }


def reference(x_prompt, x_sample, cache_k_cmp, cache_v_cmp, cache_k_sel, cache_v_sel, state_k_win,
              state_v_win, page_table, norm_in, w_in, b_gate, ln_v_g, ln_v_b, w_spatial, b_spatial,
              q_norm, k_norm_cmp, k_norm_sel, k_norm_win, cmp_pe_k, cmp_pe_v, w_cmp_k1, w_cmp_k2,
              w_cmp_v1, w_cmp_v2, out_norm_a, out_norm_b, w_out):
    slopes = alibi_slopes()
    h_p, h_s = x_prompt, x_sample
    st_p, st_s = [], []
    for l in range(DEPTH):
        lw = (norm_in[l], w_in[l], b_gate[l], ln_v_g[l], ln_v_b[l], w_spatial[l], b_spatial[l],
              q_norm[l], k_norm_cmp[l], k_norm_sel[l], k_norm_win[l], cmp_pe_k[l], cmp_pe_v[l],
              w_cmp_k1[l], w_cmp_k2[l], w_cmp_v1[l], w_cmp_v2[l], out_norm_a[l], out_norm_b[l], w_out[l])
        h_p, sp = layer_prompt(h_p, lw, slopes)
        h_s, ss = layer_sample(h_s, l, cache_k_cmp, cache_v_cmp, cache_k_sel, cache_v_sel,
                               state_k_win[l], state_v_win[l], page_table, lw, slopes)
        st_p.append(sp)
        st_s.append(ss)
    new_p = [jnp.stack(t) for t in zip(*st_p)]
    new_s = [jnp.stack(t) for t in zip(*st_s)]
    return (h_p, h_s, new_p[0], new_p[1], new_p[2], new_p[3], new_p[4], new_p[5],
            new_s[0], new_s[1], new_s[2], new_s[3], new_s[4], new_s[5], new_s[6])
```

```python
import functools

import numpy as np
import jax
import jax.numpy as jnp
from jax import lax
from jax.experimental import pallas as pl
from jax.experimental.pallas import tpu as pltpu

F32 = jnp.float32
BF16 = jnp.bfloat16

D_MODEL = 1024
D_A = 512
H_A = 8
CHUNK = 128
D_B = 512
H_B = 8
HD = 64
KV_B = 2
G_B = 4
KV_W = KV_B * HD
CMP_STRIDE = 16
CMP_BLK = 32
CMP_HID = 128
SEL_BLK = 64
N_SEL = 16
N_LOCAL = 2
WINDOW = 512
Q_BLOCK = 128
EPS = 1e-6
NEG = -1e30
SCALE = HD ** -0.5

LANES = 128
N_PACK = 3456
VMEM_LIMIT = 56 * 1024 * 1024


def _dot(a, b):
    return jnp.dot(a, b, preferred_element_type=F32)


def _dot_nt(a, b):
    return lax.dot_general(a, b, (((1,), (1,)), ((), ())), preferred_element_type=F32)


def _group_mean_sq(t, g_ref):
    t2 = t * t
    hi = t2.astype(BF16)
    lo = (t2 - hi.astype(F32)).astype(BF16)
    g = g_ref[...]
    return (_dot(hi, g) + _dot(lo, g)) * (1.0 / HD)


def _silu(z):
    return z * jax.nn.sigmoid(z)


def _proj_kernel(x_ref, nin_ref, w_ref, bg_ref, lng_ref, lnb_ref, wc_ref, bc_ref, qn_ref, ksn_ref,
                 kwn_ref, ona_ref, g512_ref, g128_ref,
                 oa_ref, vn_ref, q_ref, szb_ref, gates_ref, kc_ref, vc_ref, ks_ref, vs_ref, kw_ref,
                 vw_ref):
    tn = x_ref.shape[0]
    x = x_ref[...]
    ms = jnp.mean(x * x, axis=-1, keepdims=True)
    xb = (x * lax.rsqrt(ms + EPS) * nin_ref[...]).astype(BF16)

    def proj(lo, n):
        return _dot(xb, w_ref[:, lo:lo + n])

    u = proj(0, D_A)
    v = proj(D_A, D_A)
    mu = jnp.mean(v, axis=-1, keepdims=True)
    vc = v - mu
    var = jnp.mean(vc * vc, axis=-1, keepdims=True)
    vn = vc * lax.rsqrt(var + EPS) * lng_ref[...] + lnb_ref[...]
    vn_ref[...] = vn
    vnb = vn.astype(BF16)
    lane = lax.broadcasted_iota(jnp.int32, (CHUNK, LANES), 1)
    rows = []
    for c in range(tn // CHUNK):
        vcb = vnb[c * CHUNK:(c + 1) * CHUNK]
        cols = []
        for m in range(D_A // LANES):
            va = vcb[:, m * LANES:(m + 1) * LANES]
            sa = _dot(wc_ref[2 * m], va)
            sb = _dot(wc_ref[2 * m + 1], va)
            cols.append(jnp.where(lane < HD, sa, sb))
        rows.append(jnp.concatenate(cols, axis=1) + bc_ref[...])
    s = jnp.concatenate(rows, axis=0)
    o = u * s
    o = o * lax.rsqrt(jnp.mean(o * o, axis=-1, keepdims=True) + EPS) * ona_ref[...]
    za = proj(2 * D_A, D_A)
    oa_ref[...] = (o * _silu(za)).astype(BF16)

    q = proj(3 * D_A, D_B)
    q_ref[...] = q * lax.rsqrt(_group_mean_sq(q, g512_ref) + EPS) * qn_ref[...]
    zb = proj(3 * D_A + D_B, D_B)
    szb_ref[...] = _silu(zb).astype(BF16)
    base = 3 * D_A + 2 * D_B
    kc_ref[...] = proj(base, KV_W)
    vc_ref[...] = proj(base + KV_W, KV_W)
    ks = proj(base + 2 * KV_W, KV_W)
    ks_ref[...] = ks * lax.rsqrt(_group_mean_sq(ks, g128_ref) + EPS) * ksn_ref[...]
    vs_ref[...] = proj(base + 3 * KV_W, KV_W)
    kw = proj(base + 4 * KV_W, KV_W)
    kw_ref[...] = kw * lax.rsqrt(_group_mean_sq(kw, g128_ref) + EPS) * kwn_ref[...]
    vw_ref[...] = proj(base + 5 * KV_W, KV_W)
    gl = proj(base + 6 * KV_W, LANES)
    gates_ref[...] = jax.nn.sigmoid(gl + bg_ref[...])


def _row_tile(n):
    for tn in (256, 128):
        if n % tn == 0:
            return tn
    raise ValueError(f"row count {n} must be a multiple of {CHUNK}")


def _projection(x2d, wts):
    n = x2d.shape[0]
    tn = _row_tile(n)
    full = lambda a: pl.BlockSpec(a.shape, lambda i: (0,) * a.ndim)
    consts = [wts["nin"], wts["w_in"], wts["bg"], wts["lng"], wts["lnb"], wts["wc"], wts["bc"],
              wts["qn"], wts["ksn"], wts["kwn"], wts["ona"], wts["g512"], wts["g128"]]
    row = lambda w: pl.BlockSpec((tn, w), lambda i: (i, 0))
    out_w = [(D_A, BF16), (D_A, F32), (D_B, F32), (D_B, BF16), (LANES, F32)] + [(KV_W, F32)] * 6
    return pl.pallas_call(
        _proj_kernel,
        grid=(n // tn,),
        in_specs=[row(D_MODEL)] + [full(a) for a in consts],
        out_specs=[row(w) for w, _ in out_w],
        out_shape=[jax.ShapeDtypeStruct((n, w), dt) for w, dt in out_w],
        compiler_params=pltpu.CompilerParams(dimension_semantics=("arbitrary",),
                                             vmem_limit_bytes=VMEM_LIMIT),
        name="projection",
    )(x2d, *consts)


def _merge_kernel(x_ref, oa_ref, ob_ref, szb_ref, onb_ref, w_ref, y_ref):
    ob = ob_ref[...]
    ob = ob * lax.rsqrt(jnp.mean(ob * ob, axis=-1, keepdims=True) + EPS) * onb_ref[...]
    ob = (ob * szb_ref[...].astype(F32)).astype(BF16)
    y_ref[...] = x_ref[...] + _dot(oa_ref[...], w_ref[:D_A, :]) + _dot(ob, w_ref[D_A:, :])


def _merge(x2d, oa, ob, szb, wts):
    n = x2d.shape[0]
    tn = _row_tile(n)
    row = lambda w: pl.BlockSpec((tn, w), lambda i: (i, 0))
    full = lambda a: pl.BlockSpec(a.shape, lambda i: (0,) * a.ndim)
    return pl.pallas_call(
        _merge_kernel,
        grid=(n // tn,),
        in_specs=[row(D_MODEL), row(D_A), row(D_B), row(D_B), full(wts["onb"]), full(wts["w_out"])],
        out_specs=row(D_MODEL),
        out_shape=jax.ShapeDtypeStruct((n, D_MODEL), F32),
        compiler_params=pltpu.CompilerParams(dimension_semantics=("arbitrary",),
                                             vmem_limit_bytes=VMEM_LIMIT),
        name="merge",
    )(x2d, oa, ob, szb, wts["onb"], wts["w_out"])


def _compress_math(xb, w1_ref, pea_ref, peb_ref, w2_ref):
    h1 = _dot(xb, w1_ref[0])
    h2 = _dot(xb, w1_ref[1])
    r = h2.shape[0]
    h2n = pltpu.roll(h2, r - 1, 0)
    pe = _dot(pea_ref[...], w1_ref[0]) + _dot(peb_ref[...], w1_ref[1])
    pre = h1 + h2n + pe[0:1]
    return _dot(_silu(pre).astype(BF16), w2_ref[...])


def _compress_kernel(rk_ref, rv_ref, w1k_ref, w1v_ref, peak_ref, pebk_ref, peav_ref, pebv_ref,
                     w2k_ref, w2v_ref, kn_ref, g128_ref, kc_ref, vc_ref):
    kc = _compress_math(rk_ref[0].astype(BF16), w1k_ref, peak_ref, pebk_ref, w2k_ref)
    kc_ref[0] = kc * lax.rsqrt(_group_mean_sq(kc, g128_ref) + EPS) * kn_ref[...]
    vc_ref[0] = _compress_math(rv_ref[0].astype(BF16), w1v_ref, peav_ref, pebv_ref, w2v_ref)


def _cmp_consts(wts):
    return [wts["w1k"], wts["w1v"], wts["peak"], wts["pebk"], wts["peav"], wts["pebv"],
            wts["w2k"], wts["w2v"], wts["kcn"], wts["g128"]]


def _compress_prompt(rk, rv, wts):
    b, r, w = rk.shape
    consts = _cmp_consts(wts)
    full = lambda a: pl.BlockSpec(a.shape, lambda i: (0,) * a.ndim)
    blk = pl.BlockSpec((1, r, w), lambda i: (i, 0, 0))
    oblk = pl.BlockSpec((1, r, KV_W), lambda i: (i, 0, 0))
    return pl.pallas_call(
        _compress_kernel,
        grid=(b,),
        in_specs=[blk, blk] + [full(a) for a in consts],
        out_specs=[oblk, oblk],
        out_shape=[jax.ShapeDtypeStruct((b, r, KV_W), F32)] * 2,
        compiler_params=pltpu.CompilerParams(dimension_semantics=("arbitrary",),
                                             vmem_limit_bytes=VMEM_LIMIT),
        name="compress_prompt",
    )(rk, rv, *consts)


def _topk_rows(score, k):
    n = score.shape[0]
    ridx = lax.broadcasted_iota(jnp.int32, score.shape, 0).astype(F32)
    sel = jnp.zeros(score.shape, jnp.bool_)
    for _ in range(k):
        m = jnp.max(score, axis=0, keepdims=True)
        idx = jnp.min(jnp.where(score == m, ridx, float(n)), axis=0, keepdims=True)
        hit = ridx == idx
        sel = jnp.logical_or(sel, hit)
        score = jnp.where(hit, -jnp.inf, score)
    return sel


def _online_update(carry, s, vt):
    m, l, acc = carry
    m_new = jnp.maximum(m, jnp.max(s, axis=0, keepdims=True))
    alpha = jnp.exp(m - m_new)
    p = jnp.exp(s - m_new)
    l = alpha * l + jnp.sum(p, axis=0, keepdims=True)
    acc = alpha * acc + _dot(vt, p.astype(BF16))
    return m_new, l, acc


def _pattn_kernel(qt_ref, gt_ref, ks_ref, vst_ref, kw_ref, vwt_ref, kc_ref, vct_ref, o_ref,
                  imp_sc, bias_sc):
    t = pl.program_id(1)
    nb = kc_ref.shape[1]
    n_sb = bias_sc.shape[0]
    qw = G_B * Q_BLOCK
    q0 = t * Q_BLOCK
    qpos = q0 + lax.broadcasted_iota(jnp.int32, (1, Q_BLOCK), 1)
    qpos4 = jnp.concatenate([qpos] * G_B, axis=1)
    kiota = lax.broadcasted_iota(jnp.int32, (Q_BLOCK, qw), 0)
    init = (jnp.full((1, qw), NEG, F32), jnp.zeros((1, qw), F32), jnp.zeros((HD, qw), F32))

    for g in range(KV_B):
        gl = slice(g * LANES, (g + 1) * LANES)
        gd = slice(g * HD, (g + 1) * HD)
        qt = jnp.concatenate([qt_ref[0, G_B * g + hh] for hh in range(G_B)], axis=1)

        sc = _dot(kc_ref[0, :, gl], qt)
        ci = lax.broadcasted_iota(jnp.int32, (nb, qw), 0)
        valid = jnp.logical_and(qpos4 >= CMP_STRIDE * ci + (CMP_BLK - 1), ci < nb - 1)
        sc = jnp.where(valid, sc, NEG)
        pc = jnp.where(valid, jnp.exp(sc - jnp.max(sc, axis=0, keepdims=True)), 0.0)
        lc = jnp.sum(pc, axis=0, keepdims=True)
        pc = pc * (1.0 / jnp.where(lc > 0.0, lc, 1.0))
        o_c = _dot(vct_ref[0, gd, :], pc.astype(BF16))

        imp_sc[...] = sum(pc[:, hh * Q_BLOCK:(hh + 1) * Q_BLOCK] for hh in range(G_B))
        ratio = SEL_BLK // CMP_STRIDE
        imp = sum(imp_sc[pl.ds(r, n_sb, stride=ratio), :] for r in range(ratio))
        bi = lax.broadcasted_iota(jnp.int32, (n_sb, Q_BLOCK), 0)
        cur = jnp.right_shift(qpos, SEL_BLK.bit_length() - 1)
        forced = jnp.logical_or(bi == 0, bi > cur - N_LOCAL)
        score = jnp.where(bi > cur, -1e9, jnp.where(forced, 1e9, imp))
        sel = _topk_rows(score, min(N_SEL, n_sb))
        bias_sc[...] = jnp.where(sel, 0.0, NEG)

        def sel_scores(j):
            k0 = pl.multiple_of(j * Q_BLOCK, Q_BLOCK)
            s = _dot(ks_ref[0, pl.ds(k0, Q_BLOCK), gl], qt)
            b0 = jnp.broadcast_to(bias_sc[pl.ds(2 * j, 1), :], (SEL_BLK, Q_BLOCK))
            b1 = jnp.broadcast_to(bias_sc[pl.ds(2 * j + 1, 1), :], (SEL_BLK, Q_BLOCK))
            bias = jnp.concatenate([b0, b1], axis=0)
            return s + jnp.concatenate([bias] * G_B, axis=1), k0

        def sel_step(j, carry):
            s, k0 = sel_scores(j)
            return _online_update(carry, s, vst_ref[0, gd, pl.ds(k0, Q_BLOCK)])

        carry = lax.fori_loop(0, t, sel_step, init)
        s, k0 = sel_scores(t)
        s = jnp.where(k0 + kiota <= qpos4, s, NEG)
        m_s, l_s, a_s = _online_update(carry, s, vst_ref[0, gd, pl.ds(k0, Q_BLOCK)])
        o_s = a_s * (1.0 / l_s)

        n_wt = WINDOW // Q_BLOCK + 1

        def win_step(w, carry):
            k0 = pl.multiple_of(q0 + (w - (n_wt - 1)) * Q_BLOCK, Q_BLOCK)
            s = _dot(kw_ref[0, pl.ds(k0, Q_BLOCK), gl], qt)
            dist = qpos4 - (k0 + kiota)
            s = jnp.where(jnp.logical_and(dist >= 0, dist < WINDOW), s, NEG)
            return _online_update(carry, s, vwt_ref[0, gd, pl.ds(k0, Q_BLOCK)])

        m_w, l_w, a_w = lax.fori_loop(jnp.maximum(0, n_wt - 1 - t), n_wt, win_step, init)
        o_w = a_w * (1.0 / l_w)

        for hh in range(G_B):
            h = G_B * g + hh
            hs = slice(hh * Q_BLOCK, (hh + 1) * Q_BLOCK)
            o_ref[0, h * HD:(h + 1) * HD, :] = (gt_ref[0, h:h + 1, :] * o_c[:, hs]
                                                + gt_ref[0, H_B + h:H_B + h + 1, :] * o_s[:, hs]
                                                + gt_ref[0, 2 * H_B + h:2 * H_B + h + 1, :] * o_w[:, hs])


def _prompt_attention(qt, gt, ksa, vst, kwa, vwt, kca, vct):
    b, _, _, t = qt.shape
    nb = kca.shape[1]
    n_sb = t // SEL_BLK
    per_b = lambda a: pl.BlockSpec((1,) + a.shape[1:], lambda i, j: (i,) + (0,) * (a.ndim - 1))
    return pl.pallas_call(
        _pattn_kernel,
        grid=(b, t // Q_BLOCK),
        in_specs=[pl.BlockSpec((1, H_B, LANES, Q_BLOCK), lambda i, j: (i, 0, 0, j)),
                  pl.BlockSpec((1, 3 * H_B, Q_BLOCK), lambda i, j: (i, 0, j)),
                  per_b(ksa), per_b(vst), per_b(kwa), per_b(vwt), per_b(kca), per_b(vct)],
        out_specs=pl.BlockSpec((1, D_B, Q_BLOCK), lambda i, j: (i, 0, j)),
        out_shape=jax.ShapeDtypeStruct((b, D_B, t), F32),
        scratch_shapes=[pltpu.VMEM((nb, Q_BLOCK), F32), pltpu.VMEM((n_sb, Q_BLOCK), F32)],
        compiler_params=pltpu.CompilerParams(dimension_semantics=("arbitrary", "arbitrary"),
                                             vmem_limit_bytes=VMEM_LIMIT),
        name="prompt_attention",
    )(qt, gt, ksa, vst, kwa, vwt, kca, vct)


def _page_copy(hbm, buf, sem, page, slot, which, p, rows):
    return pltpu.make_async_copy(hbm.at[page], buf.at[slot, which, pl.ds(p * rows, rows)],
                                 sem.at[slot, which])


def _gather_pages(pt_ref, hbm_a, hbm_b, buf, sem, rows):
    b = pl.program_id(0)
    nb = pl.num_programs(0)
    n_pages = pt_ref.shape[1]
    slot = lax.rem(b, 2)

    def issue(bb, sl):
        for p in range(n_pages):
            page = pt_ref[bb, p]
            _page_copy(hbm_a, buf, sem, page, sl, 0, p, rows).start()
            _page_copy(hbm_b, buf, sem, page, sl, 1, p, rows).start()

    @pl.when(b == 0)
    def _():
        issue(0, 0)

    @pl.when(b + 1 < nb)
    def _():
        issue(b + 1, 1 - slot)

    for p in range(n_pages):
        _page_copy(hbm_a, buf, sem, 0, slot, 0, p, rows).wait()
        _page_copy(hbm_b, buf, sem, 0, slot, 1, p, rows).wait()
    return slot


def _scmp_kernel(pt_ref, kc_hbm, vc_hbm, qk_ref, slope_ref, irow_ref, w1k_ref, w1v_ref, peak_ref,
                 pebk_ref, peav_ref, pebv_ref, w2k_ref, w2v_ref, kn_ref, g128_ref, gsum_ref,
                 oc_ref, imp_ref, buf, sem, *, past_len):
    rows = buf.shape[2] // pt_ref.shape[1]
    slot = _gather_pages(pt_ref, kc_hbm, vc_hbm, buf, sem, rows)
    kc = _compress_math(buf[slot, 0].astype(BF16), w1k_ref, peak_ref, pebk_ref, w2k_ref)
    kc = kc * lax.rsqrt(_group_mean_sq(kc, g128_ref) + EPS) * kn_ref[...]
    vc = _compress_math(buf[slot, 1].astype(BF16), w1v_ref, peav_ref, pebv_ref, w2v_ref)
    nb = kc.shape[0]
    nr = qk_ref.shape[1]

    s = _dot_nt(qk_ref[0], kc.astype(BF16))
    ci = lax.broadcasted_iota(jnp.int32, (nr, nb), 1)
    end = (CMP_STRIDE * ci + (CMP_BLK - 1)).astype(F32)
    dist = (past_len + irow_ref[:, 0:1]) - end
    valid = jnp.logical_and(dist >= 0.0, ci < nb - 1)
    s = jnp.where(valid, s - slope_ref[:, 0:1] * dist, NEG)
    p = jnp.where(valid, jnp.exp(s - jnp.max(s, axis=-1, keepdims=True)), 0.0)
    l = jnp.sum(p, axis=-1, keepdims=True)
    p = p * (1.0 / jnp.where(l > 0.0, l, 1.0))
    oc_ref[0] = _dot(p.astype(BF16), vc.astype(BF16))

    tq = nr // H_B
    parts = []
    for g in range(KV_B):
        base = g * G_B * tq
        parts.append(sum(p[base + hh * tq:base + (hh + 1) * tq] for hh in range(G_B)))
    imp = jnp.concatenate(parts, axis=0)
    hi = imp.astype(BF16)
    r1 = imp - hi.astype(F32)
    mid = r1.astype(BF16)
    lo = (r1 - mid.astype(F32)).astype(BF16)
    gs = gsum_ref[...]
    imp_ref[0] = _dot(hi, gs) + _dot(mid, gs) + _dot(lo, gs)


def _sample_compressed(page_table, kc_pages, vc_pages, qk, slope, irow, wts, past_len):
    b, n_pages = page_table.shape
    rows = kc_pages.shape[1]
    nb = n_pages * rows
    nr = qk.shape[1]
    ratio = SEL_BLK // CMP_STRIDE
    n_blk = -(-(nb // ratio + 1) // LANES) * LANES
    gsum = jnp.asarray(np.arange(nb)[:, None] // ratio == np.arange(n_blk)[None, :], BF16)
    consts = [slope, irow] + _cmp_consts(wts) + [gsum]
    full = lambda a: pl.BlockSpec(a.shape, lambda i, pt: (0,) * a.ndim)
    grid_spec = pltpu.PrefetchScalarGridSpec(
        num_scalar_prefetch=1,
        grid=(b,),
        in_specs=[pl.BlockSpec(memory_space=pl.ANY), pl.BlockSpec(memory_space=pl.ANY),
                  pl.BlockSpec((1, nr, LANES), lambda i, pt: (i, 0, 0))] + [full(a) for a in consts],
        out_specs=[pl.BlockSpec((1, nr, KV_W), lambda i, pt: (i, 0, 0)),
                   pl.BlockSpec((1, nr // G_B, n_blk), lambda i, pt: (i, 0, 0))],
        scratch_shapes=[pltpu.VMEM((2, 2, nb, kc_pages.shape[2]), F32),
                        pltpu.SemaphoreType.DMA((2, 2))],
    )
    return pl.pallas_call(
        functools.partial(_scmp_kernel, past_len=float(past_len)),
        grid_spec=grid_spec,
        out_shape=[jax.ShapeDtypeStruct((b, nr, KV_W), F32),
                   jax.ShapeDtypeStruct((b, nr // G_B, n_blk), F32)],
        compiler_params=pltpu.CompilerParams(dimension_semantics=("arbitrary",),
                                             vmem_limit_bytes=VMEM_LIMIT),
        name="sample_compressed",
    )(page_table, kc_pages, vc_pages, qk, *consts)


def _topk_lanes_kernel(imp_ref, out_ref, *, n_blocks, cur, k):
    imp = imp_ref[...]
    bi = lax.broadcasted_iota(jnp.int32, imp.shape, 1).astype(F32)
    forced = jnp.logical_or(bi == 0, bi > cur - N_LOCAL)
    score = jnp.where(bi > cur, -1e9, jnp.where(forced, 1e9, imp))
    score = jnp.where(bi < n_blocks, score, -jnp.inf)
    width = imp.shape[1]
    sel = jnp.zeros(imp.shape, jnp.bool_)
    for _ in range(k):
        m = jnp.max(score, axis=-1, keepdims=True)
        idx = jnp.min(jnp.where(score == m, bi, float(width)), axis=-1, keepdims=True)
        hit = bi == idx
        sel = jnp.logical_or(sel, hit)
        score = jnp.where(hit, -jnp.inf, score)
    out_ref[...] = jnp.where(sel, 0.0, NEG)


def _sample_topk(imp2d, n_blocks, cur):
    n, w = imp2d.shape
    tr = 256 if n % 256 == 0 else n
    blk = pl.BlockSpec((tr, w), lambda i: (i, 0))
    return pl.pallas_call(
        functools.partial(_topk_lanes_kernel, n_blocks=n_blocks, cur=cur, k=min(N_SEL, n_blocks)),
        grid=(n // tr,),
        in_specs=[blk],
        out_specs=blk,
        out_shape=jax.ShapeDtypeStruct((n, w), F32),
        compiler_params=pltpu.CompilerParams(dimension_semantics=("arbitrary",)),
        name="sample_topk",
    )(imp2d)


def _ssel_kernel(pt_ref, ks_hbm, vs_hbm, qsel_ref, knew_ref, vnew_ref, kwst_ref, vwst_ref, kwnew_ref,
                 vwnew_ref, oc_ref, gsel_ref, slope_ref, irow_ref, o_ref,
                 buf, sem, s_sc, oh_sc, *, past_len, tile):
    b = pl.program_id(0)
    page = buf.shape[2] // pt_ref.shape[1]
    n_keys = buf.shape[2]
    nr = qsel_ref.shape[1]
    n_tiles = n_keys // tile
    n_pb = oh_sc.shape[1]

    @pl.when(b == 0)
    def _():
        ki = lax.broadcasted_iota(jnp.int32, oh_sc.shape, 0)
        bi = lax.broadcasted_iota(jnp.int32, oh_sc.shape, 1)
        oh_sc[...] = jnp.where(ki // SEL_BLK == bi, 1.0, 0.0).astype(BF16)

    slot = _gather_pages(pt_ref, ks_hbm, vs_hbm, buf, sem, page)
    slope = slope_ref[:, 0:1]
    irow = irow_ref[:, 0:1]
    qpos = past_len + irow
    qall = qsel_ref[0]
    qk = qall[:, :KV_W]

    lane = lax.broadcasted_iota(jnp.int32, (nr, tile), 1).astype(F32)

    def score_step(j, mrun):
        k0 = pl.multiple_of(j * tile, tile)
        kcat = jnp.concatenate([buf[slot, 0, pl.ds(k0, tile), :].astype(BF16),
                                oh_sc[pl.ds(k0, tile), :]], axis=1)
        s = _dot_nt(qall, kcat) - slope * (qpos - (k0.astype(F32) + lane))
        s_sc[:, pl.ds(k0, tile)] = s
        return jnp.maximum(mrun, s)

    mrun = lax.fori_loop(0, n_tiles, score_step, jnp.full((nr, tile), NEG, F32))
    nl = lax.broadcasted_iota(jnp.int32, (nr, LANES), 1).astype(F32)
    dn = irow - nl
    s_new = jnp.where(dn >= 0.0, _dot_nt(qk, knew_ref[0]) - slope * dn, NEG)
    m = jnp.maximum(jnp.max(mrun, axis=-1, keepdims=True), jnp.max(s_new, axis=-1, keepdims=True))

    def pv_step(j, carry):
        lrun, acc = carry
        k0 = pl.multiple_of(j * tile, tile)
        p = jnp.exp(s_sc[:, pl.ds(k0, tile)] - m)
        acc = acc + _dot(p.astype(BF16), buf[slot, 1, pl.ds(k0, tile), :].astype(BF16))
        return lrun + p, acc

    lrun, acc = lax.fori_loop(0, n_tiles, pv_step,
                              (jnp.zeros((nr, tile), F32), jnp.zeros((nr, KV_W), F32)))
    p_new = jnp.exp(s_new - m)
    acc = acc + _dot(p_new.astype(BF16), vnew_ref[0])
    l = jnp.sum(lrun, axis=-1, keepdims=True) + jnp.sum(p_new, axis=-1, keepdims=True)
    o_s = acc * (1.0 / l)

    n_buf = kwst_ref.shape[1]
    wl = lax.broadcasted_iota(jnp.int32, (nr, n_buf), 1).astype(F32)
    dw = (n_buf + irow) - wl
    s_w = jnp.where(jnp.logical_and(dw >= 0.0, dw < WINDOW),
                    _dot_nt(qk, kwst_ref[0].astype(BF16)) - slope * dw, NEG)
    s_wn = jnp.where(dn >= 0.0, _dot_nt(qk, kwnew_ref[0]) - slope * dn, NEG)
    m_w = jnp.maximum(jnp.max(s_w, axis=-1, keepdims=True), jnp.max(s_wn, axis=-1, keepdims=True))
    p_w = jnp.exp(s_w - m_w)
    p_wn = jnp.exp(s_wn - m_w)
    l_w = jnp.sum(p_w, axis=-1, keepdims=True) + jnp.sum(p_wn, axis=-1, keepdims=True)
    o_w = (_dot(p_w.astype(BF16), vwst_ref[0].astype(BF16)) + _dot(p_wn.astype(BF16), vwnew_ref[0])) * (1.0 / l_w)

    gsel = gsel_ref[0]
    o_ref[0] = gsel[:, 0:1] * oc_ref[0] + gsel[:, 1:2] * o_s + gsel[:, 2:3] * o_w


def _sample_selected(page_table, ks_pages, vs_pages, qsel, knew, vnew, kwst, vwst, kwnew, vwnew, oc,
                     gsel, slope, irow, past_len):
    b, n_pages = page_table.shape
    page = ks_pages.shape[1]
    n_keys = n_pages * page
    nr = qsel.shape[1]
    tile = 256 if n_keys % 256 == 0 else page
    per_b = lambda a: pl.BlockSpec((1,) + a.shape[1:], lambda i, pt: (i,) + (0,) * (a.ndim - 1))
    full = lambda a: pl.BlockSpec(a.shape, lambda i, pt: (0,) * a.ndim)
    grid_spec = pltpu.PrefetchScalarGridSpec(
        num_scalar_prefetch=1,
        grid=(b,),
        in_specs=[pl.BlockSpec(memory_space=pl.ANY), pl.BlockSpec(memory_space=pl.ANY)]
                 + [per_b(a) for a in (qsel, knew, vnew, kwst, vwst, kwnew, vwnew, oc, gsel)]
                 + [full(slope), full(irow)],
        out_specs=pl.BlockSpec((1, nr, KV_W), lambda i, pt: (i, 0, 0)),
        scratch_shapes=[pltpu.VMEM((2, 2, n_keys, KV_W), F32),
                        pltpu.SemaphoreType.DMA((2, 2)),
                        pltpu.VMEM((nr, n_keys), F32),
                        pltpu.VMEM((n_keys, n_keys // SEL_BLK), BF16)],
    )
    return pl.pallas_call(
        functools.partial(_ssel_kernel, past_len=float(past_len), tile=tile),
        grid_spec=grid_spec,
        out_shape=jax.ShapeDtypeStruct((b, nr, KV_W), F32),
        compiler_params=pltpu.CompilerParams(dimension_semantics=("arbitrary",),
                                             vmem_limit_bytes=VMEM_LIMIT),
        name="sample_selected",
    )(page_table, ks_pages, vs_pages, qsel, knew, vnew, kwst, vwst, kwnew, vwnew, oc, gsel, slope, irow)


def _slopes():
    return np.asarray([2.0 ** -(h + 1) for h in range(H_B)], np.float32)


def _block_ones(n):
    idx = np.arange(n) // HD
    return jnp.asarray(idx[:, None] == idx[None, :], BF16)


def _layer_weights(l, t_chunk, norm_in, w_in, b_gate, ln_v_g, ln_v_b, w_spatial, b_spatial, q_norm,
                   k_norm_cmp, k_norm_sel, k_norm_win, cmp_pe_k, cmp_pe_v, w_cmp_k1, w_cmp_k2,
                   w_cmp_v1, w_cmp_v2, out_norm_a, out_norm_b, w_out):
    w = w_in[l]
    n_main = 3 * D_A + 2 * D_B
    n_gate = 3 * H_B
    w_pack = jnp.concatenate([w[:, :n_main], w[:, n_main + n_gate:], w[:, n_main:n_main + n_gate],
                              jnp.zeros((D_MODEL, N_PACK - w.shape[1]), F32)], axis=1).astype(BF16)
    c = t_chunk
    rep = CHUNK // c
    tri = jnp.where(jnp.tril(jnp.ones((c, c), bool)), w_spatial[l][:, :c, :c], 0.0)
    wc = jnp.einsum("ab,hij->haibj", jnp.eye(rep, dtype=F32), tri).reshape(H_A, CHUNK, CHUNK).astype(BF16)
    bc = jnp.tile(jnp.repeat(b_spatial[l][:, :c].T, D_A // H_A, axis=1), (rep, 1))

    def big1(w1):
        w1r = w1.reshape(2, CMP_STRIDE, HD, CMP_HID)
        return jnp.einsum("ajdh,ck->ajcdkh", w1r, jnp.eye(KV_B, dtype=F32)).reshape(
            2, CMP_STRIDE * KV_W, KV_B * CMP_HID).astype(BF16)

    def big2(w2):
        return jnp.einsum("hd,kc->khcd", w2, jnp.eye(KV_B, dtype=F32)).reshape(
            KV_B * CMP_HID, KV_W).astype(BF16)

    def pe_rows(pe):
        halves = pe.reshape(2, CMP_STRIDE, 1, HD)
        flat = jnp.broadcast_to(halves, (2, CMP_STRIDE, KV_B, HD)).reshape(2, 1, CMP_STRIDE * KV_W)
        flat = jnp.broadcast_to(flat, (2, 8, CMP_STRIDE * KV_W)).astype(BF16)
        return flat[0], flat[1]

    peak, pebk = pe_rows(cmp_pe_k[l])
    peav, pebv = pe_rows(cmp_pe_v[l])
    row = lambda a, reps: jnp.tile(a, reps)[None, :]
    bg = jnp.concatenate([b_gate[l], jnp.zeros((LANES - n_gate,), F32)])[None, :]
    return dict(
        nin=norm_in[l][None, :], w_in=w_pack, bg=bg, lng=ln_v_g[l][None, :], lnb=ln_v_b[l][None, :],
        wc=wc, bc=bc, qn=row(q_norm[l], H_B), ksn=row(k_norm_sel[l], KV_B), kwn=row(k_norm_win[l], KV_B),
        kcn=row(k_norm_cmp[l], KV_B), ona=out_norm_a[l][None, :], onb=out_norm_b[l][None, :],
        g512=_block_ones(D_B), g128=_block_ones(KV_W),
        w1k=big1(w_cmp_k1[l]), w1v=big1(w_cmp_v1[l]), w2k=big2(w_cmp_k2[l]), w2v=big2(w_cmp_v2[l]),
        peak=peak, pebk=pebk, peav=peav, pebv=pebv, w_out=w_out[l].astype(BF16))


def _pos_feats(pos):
    pos = np.asarray(pos)
    f = np.zeros((pos.shape[0], HD), np.float32)
    f[:, 0] = SEL_BLK * (pos // SEL_BLK)
    f[:, 1] = pos % SEL_BLK
    f[:, 2] = 1.0
    f[:, 3] = 1.0
    return jnp.asarray(f, BF16)


def _key_aug(k, pos):
    b, n, _ = k.shape
    feats = jnp.broadcast_to(_pos_feats(pos)[None, :, None, :], (b, n, KV_B, HD))
    return jnp.concatenate([k.reshape(b, n, KV_B, HD).astype(BF16), feats], axis=-1).reshape(b, n, 2 * KV_W)


def _query_aug_t(q, pos):
    b, t, _ = q.shape
    pos = np.asarray(pos)
    sl = _slopes()
    f = np.zeros((H_B, HD, t), np.float32)
    f[:, 0, :] = sl[:, None]
    f[:, 1, :] = sl[:, None]
    f[:, 2, :] = -sl[:, None] * (SEL_BLK * (pos // SEL_BLK))[None, :]
    f[:, 3, :] = -sl[:, None] * (pos % SEL_BLK)[None, :]
    qt = jnp.transpose((q * SCALE).astype(BF16).reshape(b, t, H_B, HD), (0, 2, 3, 1))
    feats = jnp.broadcast_to(jnp.asarray(f, BF16)[None], (b, H_B, HD, t))
    return jnp.concatenate([qt, feats], axis=2)


def _prompt_layer(x, wts):
    b, t, _ = x.shape
    n = b * t
    x2d = x.reshape(n, D_MODEL)
    oa, _, q, szb, gates, kc_r, vc_r, ks, vs, kw, vw = _projection(x2d, wts)
    nc = t // CMP_STRIDE
    kc, vc = _compress_prompt(kc_r.reshape(b, nc, CMP_STRIDE * KV_W),
                              vc_r.reshape(b, nc, CMP_STRIDE * KV_W), wts)
    pos = np.arange(t)
    three = lambda a: a.reshape(b, t, KV_W)
    tr = lambda a: jnp.transpose(a.astype(BF16), (0, 2, 1))
    kc_end = np.arange(nc) * CMP_STRIDE + CMP_BLK - 1
    ob_t = _prompt_attention(
        _query_aug_t(q.reshape(b, t, D_B), pos),
        jnp.transpose(gates[:, :3 * H_B].reshape(b, t, 3 * H_B), (0, 2, 1)),
        _key_aug(three(ks), pos), tr(three(vs)), _key_aug(three(kw), pos), tr(three(vw)),
        _key_aug(kc, kc_end), tr(vc))
    ob = jnp.transpose(ob_t, (0, 2, 1)).reshape(n, D_B)
    y = _merge(x2d, oa, ob, szb, wts).reshape(b, t, D_MODEL)
    heads = lambda a: a.reshape(b, t, KV_B, HD)
    n_keep = min(WINDOW, t)
    return y, (heads(kc_r), heads(vc_r), heads(ks), heads(vs),
               heads(kw)[:, t - n_keep:], heads(vw)[:, t - n_keep:])


def _sample_layer(x, l, cache_k_cmp, cache_v_cmp, cache_k_sel, cache_v_sel, k_win_buf, v_win_buf,
                  page_table, wts):
    b, t, _ = x.shape
    n = b * t
    n_pool, page = cache_k_cmp.shape[1], cache_k_cmp.shape[2]
    n_pages = page_table.shape[1]
    past_len = n_pages * page
    x2d = x.reshape(n, D_MODEL)
    oa, vn, q, szb, gates, kc_r, vc_r, ks, vs, kw, vw = _projection(x2d, wts)

    nr = H_B * t
    sl = _slopes()
    slope = jnp.asarray(np.broadcast_to(np.repeat(sl, t)[:, None], (nr, LANES)).copy())
    irow = jnp.asarray(np.broadcast_to(np.tile(np.arange(t, dtype=np.float32), H_B)[:, None], (nr, LANES)).copy())
    qh = jnp.transpose((q * SCALE).reshape(b, t, KV_B, G_B, HD), (0, 2, 3, 1, 4))
    qk = jnp.einsum("bghtd,gk->bghtkd", qh, jnp.eye(KV_B, dtype=F32)).reshape(b, nr, KV_W).astype(BF16)

    rows = page // CMP_STRIDE
    oc, imp = _sample_compressed(
        page_table, cache_k_cmp[l].reshape(n_pool, rows, CMP_STRIDE * KV_W),
        cache_v_cmp[l].reshape(n_pool, rows, CMP_STRIDE * KV_W), qk, slope, irow, wts, past_len)

    n_sb = -(-(past_len + t) // SEL_BLK)
    n_pb = past_len // SEL_BLK
    cur = past_len // SEL_BLK
    wpad = imp.shape[2]
    bias = _sample_topk(imp.reshape(b * KV_B * t, wpad), n_sb, cur).reshape(b, KV_B, 1, t, wpad)
    bias = jnp.broadcast_to(bias[..., :n_pb], (b, KV_B, G_B, t, n_pb)).reshape(b, nr, n_pb)
    qsel = jnp.concatenate([qk, bias.astype(BF16)], axis=-1)

    pad_rows = lambda a: jnp.pad(a.reshape(b, t, KV_W).astype(BF16), ((0, 0), (0, LANES - t), (0, 0)))
    gsel = jnp.transpose(gates[:, :3 * H_B].reshape(b, t, 3, H_B), (0, 3, 1, 2)).reshape(b, nr, 3)
    gsel = jnp.pad(gsel, ((0, 0), (0, 0), (0, LANES - 3)))
    o = _sample_selected(page_table, cache_k_sel[l].reshape(n_pool, page, KV_W),
                         cache_v_sel[l].reshape(n_pool, page, KV_W), qsel, pad_rows(ks), pad_rows(vs),
                         k_win_buf.reshape(b, -1, KV_W), v_win_buf.reshape(b, -1, KV_W),
                         pad_rows(kw), pad_rows(vw), oc, gsel, slope, irow, past_len)
    o5 = o.reshape(b, KV_B, G_B, t, KV_B, HD)
    ob = jnp.stack([o5[:, g, :, :, g, :] for g in range(KV_B)], axis=1)
    ob = jnp.transpose(ob, (0, 3, 1, 2, 4)).reshape(n, D_B)
    y = _merge(x2d, oa, ob, szb, wts).reshape(b, t, D_MODEL)
    heads = lambda a: a.reshape(b, t, KV_B, HD)
    kw_all = jnp.concatenate([k_win_buf, heads(kw)], axis=1)
    vw_all = jnp.concatenate([v_win_buf, heads(vw)], axis=1)
    return y, (heads(kc_r), heads(vc_r), heads(ks), heads(vs), kw_all[:, t:], vw_all[:, t:],
               vn.reshape(b, t, D_A))


def kernel(x_prompt, x_sample, cache_k_cmp, cache_v_cmp, cache_k_sel, cache_v_sel, state_k_win, state_v_win, page_table, norm_in, w_in, b_gate, ln_v_g, ln_v_b, w_spatial, b_spatial, q_norm, k_norm_cmp, k_norm_sel, k_norm_win, cmp_pe_k, cmp_pe_v, w_cmp_k1, w_cmp_k2, w_cmp_v1, w_cmp_v2, out_norm_a, out_norm_b, w_out):
    params = (norm_in, w_in, b_gate, ln_v_g, ln_v_b, w_spatial, b_spatial, q_norm, k_norm_cmp,
              k_norm_sel, k_norm_win, cmp_pe_k, cmp_pe_v, w_cmp_k1, w_cmp_k2, w_cmp_v1, w_cmp_v2,
              out_norm_a, out_norm_b, w_out)
    depth = w_in.shape[0]
    h_p, h_s = x_prompt, x_sample
    st_p, st_s = [], []
    for l in range(depth):
        wts_p = _layer_weights(l, min(h_p.shape[1], CHUNK), *params)
        wts_s = _layer_weights(l, min(h_s.shape[1], CHUNK), *params)
        h_p, sp = _prompt_layer(h_p, wts_p)
        h_s, ss = _sample_layer(h_s, l, cache_k_cmp, cache_v_cmp, cache_k_sel, cache_v_sel,
                                state_k_win[l], state_v_win[l], page_table, wts_s)
        st_p.append(sp)
        st_s.append(ss)
    new_p = [jnp.stack(ts) for ts in zip(*st_p)]
    new_s = [jnp.stack(ts) for ts in zip(*st_s)]
    return (h_p, h_s, new_p[0], new_p[1], new_p[2], new_p[3], new_p[4], new_p[5],
            new_s[0], new_s[1], new_s[2], new_s[3], new_s[4], new_s[5], new_s[6])
```

```python
import functools

import numpy as np
import jax
import jax.numpy as jnp
from jax import lax
from jax.experimental import pallas as pl
from jax.experimental.pallas import tpu as pltpu

F32 = jnp.float32
BF16 = jnp.bfloat16

D_MODEL = 1024
D_A = 512
H_A = 8
CHUNK = 128
D_B = 512
H_B = 8
HD = 64
KV_B = 2
G_B = 4
KV_W = KV_B * HD
CMP_STRIDE = 16
CMP_BLK = 32
CMP_HID = 128
SEL_BLK = 64
N_SEL = 16
N_LOCAL = 2
WINDOW = 512
Q_BLOCK = 128
EPS = 1e-6
NEG = -1e30
SCALE = HD ** -0.5

LANES = 128
N_PACK = 3456
VMEM_LIMIT = 56 * 1024 * 1024


def _dot(a, b):
    return jnp.dot(a, b, preferred_element_type=F32)


def _dot_nt(a, b):
    return lax.dot_general(a, b, (((1,), (1,)), ((), ())), preferred_element_type=F32)


def _group_mean_sq(t, g_ref):
    t2 = t * t
    hi = t2.astype(BF16)
    lo = (t2 - hi.astype(F32)).astype(BF16)
    g = g_ref[...]
    return (_dot(hi, g) + _dot(lo, g)) * (1.0 / HD)


def _silu(z):
    return z * jax.nn.sigmoid(z)


def _proj_kernel(x_ref, nin_ref, w_ref, bg_ref, lng_ref, lnb_ref, wc_ref, bc_ref, qn_ref, ksn_ref,
                 kwn_ref, ona_ref, g512_ref, g128_ref,
                 oa_ref, vn_ref, q_ref, szb_ref, gates_ref, kc_ref, vc_ref, ks_ref, vs_ref, kw_ref,
                 vw_ref):
    tn = x_ref.shape[0]
    x = x_ref[...]
    ms = jnp.mean(x * x, axis=-1, keepdims=True)
    xb = (x * lax.rsqrt(ms + EPS) * nin_ref[...]).astype(BF16)

    def proj(lo, n):
        return _dot(xb, w_ref[:, lo:lo + n])

    u = proj(0, D_A)
    v = proj(D_A, D_A)
    mu = jnp.mean(v, axis=-1, keepdims=True)
    vc = v - mu
    var = jnp.mean(vc * vc, axis=-1, keepdims=True)
    vn = vc * lax.rsqrt(var + EPS) * lng_ref[...] + lnb_ref[...]
    vn_ref[...] = vn
    vnb = vn.astype(BF16)
    lane = lax.broadcasted_iota(jnp.int32, (CHUNK, LANES), 1)
    rows = []
    for c in range(tn // CHUNK):
        vcb = vnb[c * CHUNK:(c + 1) * CHUNK]
        cols = []
        for m in range(D_A // LANES):
            va = vcb[:, m * LANES:(m + 1) * LANES]
            sa = _dot(wc_ref[2 * m], va)
            sb = _dot(wc_ref[2 * m + 1], va)
            cols.append(jnp.where(lane < HD, sa, sb))
        rows.append(jnp.concatenate(cols, axis=1) + bc_ref[...])
    s = jnp.concatenate(rows, axis=0)
    o = u * s
    o = o * lax.rsqrt(jnp.mean(o * o, axis=-1, keepdims=True) + EPS) * ona_ref[...]
    za = proj(2 * D_A, D_A)
    oa_ref[...] = (o * _silu(za)).astype(BF16)

    q = proj(3 * D_A, D_B)
    q_ref[...] = q * lax.rsqrt(_group_mean_sq(q, g512_ref) + EPS) * qn_ref[...]
    zb = proj(3 * D_A + D_B, D_B)
    szb_ref[...] = _silu(zb).astype(BF16)
    base = 3 * D_A + 2 * D_B
    kc_ref[...] = proj(base, KV_W)
    vc_ref[...] = proj(base + KV_W, KV_W)
    ks = proj(base + 2 * KV_W, KV_W)
    ks_ref[...] = ks * lax.rsqrt(_group_mean_sq(ks, g128_ref) + EPS) * ksn_ref[...]
    vs_ref[...] = proj(base + 3 * KV_W, KV_W)
    kw = proj(base + 4 * KV_W, KV_W)
    kw_ref[...] = kw * lax.rsqrt(_group_mean_sq(kw, g128_ref) + EPS) * kwn_ref[...]
    vw_ref[...] = proj(base + 5 * KV_W, KV_W)
    gl = proj(base + 6 * KV_W, LANES)
    gates_ref[...] = jax.nn.sigmoid(gl + bg_ref[...])


def _row_tile(n):
    for tn in (256, 128):
        if n % tn == 0:
            return tn
    raise ValueError(f"row count {n} must be a multiple of {CHUNK}")


def _projection(x2d, wts):
    n = x2d.shape[0]
    tn = _row_tile(n)
    full = lambda a: pl.BlockSpec(a.shape, lambda i: (0,) * a.ndim)
    consts = [wts["nin"], wts["w_in"], wts["bg"], wts["lng"], wts["lnb"], wts["wc"], wts["bc"],
              wts["qn"], wts["ksn"], wts["kwn"], wts["ona"], wts["g512"], wts["g128"]]
    row = lambda w: pl.BlockSpec((tn, w), lambda i: (i, 0))
    out_w = [(D_A, BF16), (D_A, F32), (D_B, F32), (D_B, BF16), (LANES, F32)] + [(KV_W, F32)] * 6
    return pl.pallas_call(
        _proj_kernel,
        grid=(n // tn,),
        in_specs=[row(D_MODEL)] + [full(a) for a in consts],
        out_specs=[row(w) for w, _ in out_w],
        out_shape=[jax.ShapeDtypeStruct((n, w), dt) for w, dt in out_w],
        compiler_params=pltpu.CompilerParams(dimension_semantics=("arbitrary",),
                                             vmem_limit_bytes=VMEM_LIMIT),
        name="projection",
    )(x2d, *consts)


def _merge_kernel(x_ref, oa_ref, ob_ref, szb_ref, onb_ref, w_ref, y_ref):
    ob = ob_ref[...]
    ob = ob * lax.rsqrt(jnp.mean(ob * ob, axis=-1, keepdims=True) + EPS) * onb_ref[...]
    ob = (ob * szb_ref[...].astype(F32)).astype(BF16)
    y_ref[...] = x_ref[...] + _dot(oa_ref[...], w_ref[:D_A, :]) + _dot(ob, w_ref[D_A:, :])


def _merge(x2d, oa, ob, szb, wts):
    n = x2d.shape[0]
    tn = _row_tile(n)
    row = lambda w: pl.BlockSpec((tn, w), lambda i: (i, 0))
    full = lambda a: pl.BlockSpec(a.shape, lambda i: (0,) * a.ndim)
    return pl.pallas_call(
        _merge_kernel,
        grid=(n // tn,),
        in_specs=[row(D_MODEL), row(D_A), row(D_B), row(D_B), full(wts["onb"]), full(wts["w_out"])],
        out_specs=row(D_MODEL),
        out_shape=jax.ShapeDtypeStruct((n, D_MODEL), F32),
        compiler_params=pltpu.CompilerParams(dimension_semantics=("arbitrary",),
                                             vmem_limit_bytes=VMEM_LIMIT),
        name="merge",
    )(x2d, oa, ob, szb, wts["onb"], wts["w_out"])


def _compress_math(xb, w1_ref, pea_ref, peb_ref, w2_ref):
    h1 = _dot(xb, w1_ref[0])
    h2 = _dot(xb, w1_ref[1])
    r = h2.shape[0]
    h2n = pltpu.roll(h2, r - 1, 0)
    pe = _dot(pea_ref[...], w1_ref[0]) + _dot(peb_ref[...], w1_ref[1])
    pre = h1 + h2n + pe[0:1]
    return _dot(_silu(pre).astype(BF16), w2_ref[...])


def _compress_kernel(rk_ref, rv_ref, w1k_ref, w1v_ref, peak_ref, pebk_ref, peav_ref, pebv_ref,
                     w2k_ref, w2v_ref, kn_ref, g128_ref, kc_ref, vc_ref):
    kc = _compress_math(rk_ref[0].astype(BF16), w1k_ref, peak_ref, pebk_ref, w2k_ref)
    kc_ref[0] = kc * lax.rsqrt(_group_mean_sq(kc, g128_ref) + EPS) * kn_ref[...]
    vc_ref[0] = _compress_math(rv_ref[0].astype(BF16), w1v_ref, peav_ref, pebv_ref, w2v_ref)


def _cmp_consts(wts):
    return [wts["w1k"], wts["w1v"], wts["peak"], wts["pebk"], wts["peav"], wts["pebv"],
            wts["w2k"], wts["w2v"], wts["kcn"], wts["g128"]]


def _compress_prompt(rk, rv, wts):
    b, r, w = rk.shape
    consts = _cmp_consts(wts)
    full = lambda a: pl.BlockSpec(a.shape, lambda i: (0,) * a.ndim)
    blk = pl.BlockSpec((1, r, w), lambda i: (i, 0, 0))
    oblk = pl.BlockSpec((1, r, KV_W), lambda i: (i, 0, 0))
    return pl.pallas_call(
        _compress_kernel,
        grid=(b,),
        in_specs=[blk, blk] + [full(a) for a in consts],
        out_specs=[oblk, oblk],
        out_shape=[jax.ShapeDtypeStruct((b, r, KV_W), F32)] * 2,
        compiler_params=pltpu.CompilerParams(dimension_semantics=("arbitrary",),
                                             vmem_limit_bytes=VMEM_LIMIT),
        name="compress_prompt",
    )(rk, rv, *consts)


def _topk_rows(score, k):
    n = score.shape[0]
    ridx = lax.broadcasted_iota(jnp.int32, score.shape, 0).astype(F32)
    sel = jnp.zeros(score.shape, jnp.bool_)
    for _ in range(k):
        m = jnp.max(score, axis=0, keepdims=True)
        idx = jnp.min(jnp.where(score == m, ridx, float(n)), axis=0, keepdims=True)
        hit = ridx == idx
        sel = jnp.logical_or(sel, hit)
        score = jnp.where(hit, -jnp.inf, score)
    return sel


def _online_update(carry, s, vt):
    m, l, acc = carry
    m_new = jnp.maximum(m, jnp.max(s, axis=0, keepdims=True))
    alpha = jnp.exp(m - m_new)
    p = jnp.exp(s - m_new)
    l = alpha * l + jnp.sum(p, axis=0, keepdims=True)
    acc = alpha * acc + _dot(vt, p.astype(BF16))
    return m_new, l, acc


def _pattn_kernel(qt_ref, gt_ref, ks_ref, vst_ref, kw_ref, vwt_ref, kc_ref, vct_ref, o_ref,
                  imp_sc, bias_sc):
    t = pl.program_id(1)
    nb = kc_ref.shape[1]
    n_sb = bias_sc.shape[1]
    qw = G_B * Q_BLOCK
    pair = 2 * Q_BLOCK
    q0 = t * Q_BLOCK
    qpos = q0 + lax.broadcasted_iota(jnp.int32, (1, Q_BLOCK), 1)
    qpos4 = jnp.concatenate([qpos] * G_B, axis=1)
    gls = [slice(g * LANES, (g + 1) * LANES) for g in range(KV_B)]
    gds = [slice(g * HD, (g + 1) * HD) for g in range(KV_B)]
    qts = [jnp.concatenate([qt_ref[0, G_B * g + hh] for hh in range(G_B)], axis=1)
           for g in range(KV_B)]

    o_c = []
    for g in range(KV_B):
        sc = _dot(kc_ref[0, :, gls[g]], qts[g])
        ci = lax.broadcasted_iota(jnp.int32, (nb, qw), 0)
        valid = jnp.logical_and(qpos4 >= CMP_STRIDE * ci + (CMP_BLK - 1), ci < nb - 1)
        sc = jnp.where(valid, sc, NEG)
        pc = jnp.where(valid, jnp.exp(sc - jnp.max(sc, axis=0, keepdims=True)), 0.0)
        lc = jnp.sum(pc, axis=0, keepdims=True)
        pc = pc * (1.0 / jnp.where(lc > 0.0, lc, 1.0))
        o_c.append(_dot(vct_ref[0, gds[g], :], pc.astype(BF16)))
        imp_sc[g] = sum(pc[:, hh * Q_BLOCK:(hh + 1) * Q_BLOCK] for hh in range(G_B))
        ratio = SEL_BLK // CMP_STRIDE
        imp = sum(imp_sc[g, pl.ds(r, n_sb, stride=ratio), :] for r in range(ratio))
        bi = lax.broadcasted_iota(jnp.int32, (n_sb, Q_BLOCK), 0)
        cur = jnp.right_shift(qpos, SEL_BLK.bit_length() - 1)
        forced = jnp.logical_or(bi == 0, bi > cur - N_LOCAL)
        score = jnp.where(bi > cur, -1e9, jnp.where(forced, 1e9, imp))
        bias_sc[g] = jnp.where(_topk_rows(score, min(N_SEL, n_sb)), 0.0, NEG)

    piota = lax.broadcasted_iota(jnp.int32, (pair, qw), 0)

    def sel_pair(p, carry, causal):
        k0 = pl.multiple_of(p * pair, pair)
        out = []
        for g in range(KV_B):
            s = _dot(ks_ref[0, pl.ds(k0, pair), gls[g]], qts[g])
            rows = [jnp.broadcast_to(bias_sc[g, pl.ds((pair // SEL_BLK) * p + i, 1), :],
                                     (SEL_BLK, Q_BLOCK)) for i in range(pair // SEL_BLK)]
            bias = jnp.concatenate(rows, axis=0)
            s = s + jnp.concatenate([bias] * G_B, axis=1)
            if causal:
                s = jnp.where(k0 + piota <= qpos4, s, NEG)
            out.append(_online_update(carry[g], s, vst_ref[0, gds[g], pl.ds(k0, pair)]))
        return tuple(out)

    init = (jnp.full((1, qw), NEG, F32), jnp.zeros((1, qw), F32), jnp.zeros((HD, qw), F32))
    last = t // 2
    carry = lax.fori_loop(0, last, lambda p, c: sel_pair(p, c, False), (init,) * KV_B)
    carry = sel_pair(last, carry, True)
    o_s = [a * (1.0 / l) for _, l, a in carry]

    span = WINDOW + Q_BLOCK
    w0 = pl.multiple_of(jnp.maximum(q0 - WINDOW, 0), Q_BLOCK)
    dist = qpos4 - (w0 + lax.broadcasted_iota(jnp.int32, (span, qw), 0))
    wvalid = jnp.logical_and(dist >= 0, dist < WINDOW)
    o_w = []
    for g in range(KV_B):
        s = jnp.where(wvalid, _dot(kw_ref[0, pl.ds(w0, span), gls[g]], qts[g]), NEG)
        p = jnp.exp(s - jnp.max(s, axis=0, keepdims=True))
        l = jnp.sum(p, axis=0, keepdims=True)
        o_w.append(_dot(vwt_ref[0, gds[g], pl.ds(w0, span)], p.astype(BF16)) * (1.0 / l))

    for g in range(KV_B):
        for hh in range(G_B):
            h = G_B * g + hh
            hs = slice(hh * Q_BLOCK, (hh + 1) * Q_BLOCK)
            o_ref[0, h * HD:(h + 1) * HD, :] = (gt_ref[0, h:h + 1, :] * o_c[g][:, hs]
                                                + gt_ref[0, H_B + h:H_B + h + 1, :] * o_s[g][:, hs]
                                                + gt_ref[0, 2 * H_B + h:2 * H_B + h + 1, :] * o_w[g][:, hs])


def _prompt_attention(qt, gt, ksa, vst, kwa, vwt, kca, vct):
    b, _, _, t = qt.shape
    nb = kca.shape[1]
    n_sb = t // SEL_BLK
    per_b = lambda a: pl.BlockSpec((1,) + a.shape[1:], lambda i, j: (i,) + (0,) * (a.ndim - 1))
    return pl.pallas_call(
        _pattn_kernel,
        grid=(b, t // Q_BLOCK),
        in_specs=[pl.BlockSpec((1, H_B, LANES, Q_BLOCK), lambda i, j: (i, 0, 0, j)),
                  pl.BlockSpec((1, 3 * H_B, Q_BLOCK), lambda i, j: (i, 0, j)),
                  per_b(ksa), per_b(vst), per_b(kwa), per_b(vwt), per_b(kca), per_b(vct)],
        out_specs=pl.BlockSpec((1, D_B, Q_BLOCK), lambda i, j: (i, 0, j)),
        out_shape=jax.ShapeDtypeStruct((b, D_B, t), F32),
        scratch_shapes=[pltpu.VMEM((KV_B, nb, Q_BLOCK), F32), pltpu.VMEM((KV_B, n_sb, Q_BLOCK), F32)],
        compiler_params=pltpu.CompilerParams(dimension_semantics=("arbitrary", "arbitrary"),
                                             vmem_limit_bytes=VMEM_LIMIT),
        name="prompt_attention",
    )(qt, gt, ksa, vst, kwa, vwt, kca, vct)


def _page_copy(hbm, buf, sem, page, slot, which, p):
    return pltpu.make_async_copy(hbm.at[page], buf.at[slot, which, p], sem.at[slot, which])


def _gather_pages(pt_ref, hbm_a, hbm_b, buf, sem):
    b = pl.program_id(0)
    nb = pl.num_programs(0)
    n_pages = pt_ref.shape[1]
    slot = lax.rem(b, 2)

    def issue(bb, sl):
        for p in range(n_pages):
            page = pt_ref[bb, p]
            _page_copy(hbm_a, buf, sem, page, sl, 0, p).start()
            _page_copy(hbm_b, buf, sem, page, sl, 1, p).start()

    @pl.when(b == 0)
    def _():
        issue(0, 0)

    @pl.when(b + 1 < nb)
    def _():
        issue(b + 1, 1 - slot)

    for p in range(n_pages):
        _page_copy(hbm_a, buf, sem, 0, slot, 0, p).wait()
        _page_copy(hbm_b, buf, sem, 0, slot, 1, p).wait()
    return slot


T_PITCH = 24
T_BUFS = 4


def _chunk_rows(buf, slot, which, t_sc, lhs_sc):
    n_pages, _, page = buf.shape[2:]
    cpp = page // CMP_STRIDE
    for p in range(n_pages):
        x = buf[slot, which, p].T
        ts = t_sc.at[p % T_BUFS]
        for c in range(cpp):
            ts[T_PITCH * c:T_PITCH * c + CMP_STRIDE, :] = x[CMP_STRIDE * c:CMP_STRIDE * (c + 1), :]
        for j in range(CMP_STRIDE):
            lhs_sc[which, cpp * p:cpp * (p + 1), KV_W * j:KV_W * (j + 1)] = ts[pl.ds(j, cpp, stride=T_PITCH), :]


def _scmp_kernel(pt_ref, kc_hbm, vc_hbm, qk_ref, slope_ref, irow_ref, w1k_ref, w1v_ref, peak_ref,
                 pebk_ref, peav_ref, pebv_ref, w2k_ref, w2v_ref, kn_ref, g128_ref, gsum_ref,
                 oc_ref, imp_ref, buf, sem, t_sc, lhs_sc, *, past_len):
    slot = _gather_pages(pt_ref, kc_hbm, vc_hbm, buf, sem)
    _chunk_rows(buf, slot, 0, t_sc, lhs_sc)
    _chunk_rows(buf, slot, 1, t_sc, lhs_sc)
    kc = _compress_math(lhs_sc[0].astype(BF16), w1k_ref, peak_ref, pebk_ref, w2k_ref)
    kc = kc * lax.rsqrt(_group_mean_sq(kc, g128_ref) + EPS) * kn_ref[...]
    vc = _compress_math(lhs_sc[1].astype(BF16), w1v_ref, peav_ref, pebv_ref, w2v_ref)
    nb = kc.shape[0]
    nr = qk_ref.shape[1]

    s = _dot_nt(qk_ref[0], kc.astype(BF16))
    ci = lax.broadcasted_iota(jnp.int32, (nr, nb), 1)
    end = (CMP_STRIDE * ci + (CMP_BLK - 1)).astype(F32)
    dist = (past_len + irow_ref[:, 0:1]) - end
    valid = jnp.logical_and(dist >= 0.0, ci < nb - 1)
    s = jnp.where(valid, s - slope_ref[:, 0:1] * dist, NEG)
    p = jnp.where(valid, jnp.exp(s - jnp.max(s, axis=-1, keepdims=True)), 0.0)
    l = jnp.sum(p, axis=-1, keepdims=True)
    p = p * (1.0 / jnp.where(l > 0.0, l, 1.0))
    oc_ref[0] = _dot(p.astype(BF16), vc.astype(BF16))

    tq = nr // H_B
    parts = []
    for g in range(KV_B):
        base = g * G_B * tq
        parts.append(sum(p[base + hh * tq:base + (hh + 1) * tq] for hh in range(G_B)))
    imp = jnp.concatenate(parts, axis=0)
    hi = imp.astype(BF16)
    r1 = imp - hi.astype(F32)
    mid = r1.astype(BF16)
    lo = (r1 - mid.astype(F32)).astype(BF16)
    gs = gsum_ref[...]
    imp_ref[0] = _dot(hi, gs) + _dot(mid, gs) + _dot(lo, gs)


def _sample_compressed(page_table, kc_pages, vc_pages, qk, slope, irow, wts, past_len):
    b, n_pages = page_table.shape
    page = kc_pages.shape[2]
    cpp = page // CMP_STRIDE
    nb = n_pages * cpp
    nr = qk.shape[1]
    ratio = SEL_BLK // CMP_STRIDE
    n_blk = -(-(nb // ratio + 1) // LANES) * LANES
    gsum = jnp.asarray(np.arange(nb)[:, None] // ratio == np.arange(n_blk)[None, :], BF16)
    consts = [slope, irow] + _cmp_consts(wts) + [gsum]
    full = lambda a: pl.BlockSpec(a.shape, lambda i, pt: (0,) * a.ndim)
    grid_spec = pltpu.PrefetchScalarGridSpec(
        num_scalar_prefetch=1,
        grid=(b,),
        in_specs=[pl.BlockSpec(memory_space=pl.ANY), pl.BlockSpec(memory_space=pl.ANY),
                  pl.BlockSpec((1, nr, LANES), lambda i, pt: (i, 0, 0))] + [full(a) for a in consts],
        out_specs=[pl.BlockSpec((1, nr, KV_W), lambda i, pt: (i, 0, 0)),
                   pl.BlockSpec((1, nr // G_B, n_blk), lambda i, pt: (i, 0, 0))],
        scratch_shapes=[pltpu.VMEM((2, 2, n_pages, KV_W, page), F32),
                        pltpu.SemaphoreType.DMA((2, 2)),
                        pltpu.VMEM((T_BUFS, cpp * T_PITCH, KV_W), F32),
                        pltpu.VMEM((2, nb, CMP_STRIDE * KV_W), F32)],
    )
    return pl.pallas_call(
        functools.partial(_scmp_kernel, past_len=float(past_len)),
        grid_spec=grid_spec,
        out_shape=[jax.ShapeDtypeStruct((b, nr, KV_W), F32),
                   jax.ShapeDtypeStruct((b, nr // G_B, n_blk), F32)],
        compiler_params=pltpu.CompilerParams(dimension_semantics=("arbitrary",),
                                             vmem_limit_bytes=VMEM_LIMIT),
        name="sample_compressed",
    )(page_table, kc_pages, vc_pages, qk, *consts)


def _topk_lanes_kernel(imp_ref, out_ref, *, n_blocks, cur, k):
    imp = imp_ref[...]
    bi = lax.broadcasted_iota(jnp.int32, imp.shape, 1).astype(F32)
    forced = jnp.logical_or(bi == 0, bi > cur - N_LOCAL)
    score = jnp.where(bi > cur, -1e9, jnp.where(forced, 1e9, imp))
    score = jnp.where(bi < n_blocks, score, -jnp.inf)
    width = imp.shape[1]
    sel = jnp.zeros(imp.shape, jnp.bool_)
    for _ in range(k):
        m = jnp.max(score, axis=-1, keepdims=True)
        idx = jnp.min(jnp.where(score == m, bi, float(width)), axis=-1, keepdims=True)
        hit = bi == idx
        sel = jnp.logical_or(sel, hit)
        score = jnp.where(hit, -jnp.inf, score)
    out_ref[...] = jnp.where(sel, 0.0, NEG)


def _sample_topk(imp2d, n_blocks, cur):
    n, w = imp2d.shape
    tr = 256 if n % 256 == 0 else n
    blk = pl.BlockSpec((tr, w), lambda i: (i, 0))
    return pl.pallas_call(
        functools.partial(_topk_lanes_kernel, n_blocks=n_blocks, cur=cur, k=min(N_SEL, n_blocks)),
        grid=(n // tr,),
        in_specs=[blk],
        out_specs=blk,
        out_shape=jax.ShapeDtypeStruct((n, w), F32),
        compiler_params=pltpu.CompilerParams(dimension_semantics=("arbitrary",)),
        name="sample_topk",
    )(imp2d)


def _ssel_kernel(pt_ref, ks_hbm, vs_hbm, qsel_ref, knewt_ref, vnew_ref, kwst_ref, vwst_ref, kwnewt_ref,
                 vwnew_ref, oc_ref, gsel_ref, slope_ref, irow_ref, o_ref,
                 buf, sem, s_sc, oh_sc, *, past_len):
    b = pl.program_id(0)
    n_pages, _, page = buf.shape[2:]
    nr = qsel_ref.shape[1]
    ppt = 2 if n_pages % 2 == 0 else 1
    tile = ppt * page

    @pl.when(b == 0)
    def _():
        bi = lax.broadcasted_iota(jnp.int32, oh_sc.shape, 0)
        ki = lax.broadcasted_iota(jnp.int32, oh_sc.shape, 1)
        oh_sc[...] = jnp.where(jnp.right_shift(ki, SEL_BLK.bit_length() - 1) == bi, 1.0, 0.0).astype(BF16)

    slot = _gather_pages(pt_ref, ks_hbm, vs_hbm, buf, sem)
    slope = slope_ref[:, 0:1]
    irow = irow_ref[:, 0:1]
    qpos = past_len + irow
    qall = qsel_ref[0]
    qk = qall[:, :KV_W]

    def key_tile(which, j):
        return jnp.concatenate([buf[slot, which, ppt * j + i] for i in range(ppt)], axis=1).astype(BF16)

    lane = lax.broadcasted_iota(jnp.int32, (nr, tile), 1).astype(F32)
    mrun = jnp.full((nr, tile), NEG, F32)
    for j in range(n_pages // ppt):
        kcat = jnp.concatenate([key_tile(0, j), oh_sc[:, j * tile:(j + 1) * tile]], axis=0)
        s = _dot(qall, kcat) - slope * (qpos - (float(j * tile) + lane))
        s_sc[:, j * tile:(j + 1) * tile] = s
        mrun = jnp.maximum(mrun, s)
    nl = lax.broadcasted_iota(jnp.int32, (nr, LANES), 1).astype(F32)
    dn = irow - nl
    s_new = jnp.where(dn >= 0.0, _dot(qk, knewt_ref[0]) - slope * dn, NEG)
    m = jnp.maximum(jnp.max(mrun, axis=-1, keepdims=True), jnp.max(s_new, axis=-1, keepdims=True))

    p_new = jnp.exp(s_new - m)
    acc = _dot(p_new.astype(BF16), vnew_ref[0])
    lrun = jnp.zeros((nr, tile), F32)
    for j in range(n_pages // ppt):
        p = jnp.exp(s_sc[:, j * tile:(j + 1) * tile] - m)
        acc = acc + _dot_nt(p.astype(BF16), key_tile(1, j))
        lrun = lrun + p
    l = jnp.sum(lrun, axis=-1, keepdims=True) + jnp.sum(p_new, axis=-1, keepdims=True)
    o_s = acc * (1.0 / l)

    n_buf = kwst_ref.shape[2]
    wl = lax.broadcasted_iota(jnp.int32, (nr, n_buf), 1).astype(F32)
    dw = (n_buf + irow) - wl
    s_w = jnp.where(jnp.logical_and(dw >= 0.0, dw < WINDOW),
                    _dot(qk, kwst_ref[0].astype(BF16)) - slope * dw, NEG)
    s_wn = jnp.where(dn >= 0.0, _dot(qk, kwnewt_ref[0]) - slope * dn, NEG)
    m_w = jnp.maximum(jnp.max(s_w, axis=-1, keepdims=True), jnp.max(s_wn, axis=-1, keepdims=True))
    p_w = jnp.exp(s_w - m_w)
    p_wn = jnp.exp(s_wn - m_w)
    l_w = jnp.sum(p_w, axis=-1, keepdims=True) + jnp.sum(p_wn, axis=-1, keepdims=True)
    o_w = (_dot_nt(p_w.astype(BF16), vwst_ref[0].astype(BF16))
           + _dot(p_wn.astype(BF16), vwnew_ref[0])) * (1.0 / l_w)

    gsel = gsel_ref[0]
    o_ref[0] = gsel[:, 0:1] * oc_ref[0] + gsel[:, 1:2] * o_s + gsel[:, 2:3] * o_w


def _sample_selected(page_table, ks_pages, vs_pages, qsel, knewt, vnew, kwst, vwst, kwnewt, vwnew, oc,
                     gsel, slope, irow, past_len):
    b, n_pages = page_table.shape
    page = ks_pages.shape[2]
    n_keys = n_pages * page
    nr = qsel.shape[1]
    per_b = lambda a: pl.BlockSpec((1,) + a.shape[1:], lambda i, pt: (i,) + (0,) * (a.ndim - 1))
    full = lambda a: pl.BlockSpec(a.shape, lambda i, pt: (0,) * a.ndim)
    grid_spec = pltpu.PrefetchScalarGridSpec(
        num_scalar_prefetch=1,
        grid=(b,),
        in_specs=[pl.BlockSpec(memory_space=pl.ANY), pl.BlockSpec(memory_space=pl.ANY)]
                 + [per_b(a) for a in (qsel, knewt, vnew, kwst, vwst, kwnewt, vwnew, oc, gsel)]
                 + [full(slope), full(irow)],
        out_specs=pl.BlockSpec((1, nr, KV_W), lambda i, pt: (i, 0, 0)),
        scratch_shapes=[pltpu.VMEM((2, 2, n_pages, KV_W, page), F32),
                        pltpu.SemaphoreType.DMA((2, 2)),
                        pltpu.VMEM((nr, n_keys), F32),
                        pltpu.VMEM((n_keys // SEL_BLK, n_keys), BF16)],
    )
    return pl.pallas_call(
        functools.partial(_ssel_kernel, past_len=float(past_len)),
        grid_spec=grid_spec,
        out_shape=jax.ShapeDtypeStruct((b, nr, KV_W), F32),
        compiler_params=pltpu.CompilerParams(dimension_semantics=("arbitrary",),
                                             vmem_limit_bytes=VMEM_LIMIT),
        name="sample_selected",
    )(page_table, ks_pages, vs_pages, qsel, knewt, vnew, kwst, vwst, kwnewt, vwnew, oc, gsel, slope, irow)


def _slopes():
    return np.asarray([2.0 ** -(h + 1) for h in range(H_B)], np.float32)


def _block_ones(n):
    idx = np.arange(n) // HD
    return jnp.asarray(idx[:, None] == idx[None, :], BF16)


def _layer_weights(l, t_chunk, norm_in, w_in, b_gate, ln_v_g, ln_v_b, w_spatial, b_spatial, q_norm,
                   k_norm_cmp, k_norm_sel, k_norm_win, cmp_pe_k, cmp_pe_v, w_cmp_k1, w_cmp_k2,
                   w_cmp_v1, w_cmp_v2, out_norm_a, out_norm_b, w_out):
    w = w_in[l]
    n_main = 3 * D_A + 2 * D_B
    n_gate = 3 * H_B
    w_pack = jnp.concatenate([w[:, :n_main], w[:, n_main + n_gate:], w[:, n_main:n_main + n_gate],
                              jnp.zeros((D_MODEL, N_PACK - w.shape[1]), F32)], axis=1).astype(BF16)
    c = t_chunk
    rep = CHUNK // c
    tri = jnp.where(jnp.tril(jnp.ones((c, c), bool)), w_spatial[l][:, :c, :c], 0.0)
    wc = jnp.einsum("ab,hij->haibj", jnp.eye(rep, dtype=F32), tri).reshape(H_A, CHUNK, CHUNK).astype(BF16)
    bc = jnp.tile(jnp.repeat(b_spatial[l][:, :c].T, D_A // H_A, axis=1), (rep, 1))

    def big1(w1):
        w1r = w1.reshape(2, CMP_STRIDE, HD, CMP_HID)
        return jnp.einsum("ajdh,ck->ajcdkh", w1r, jnp.eye(KV_B, dtype=F32)).reshape(
            2, CMP_STRIDE * KV_W, KV_B * CMP_HID).astype(BF16)

    def big2(w2):
        return jnp.einsum("hd,kc->khcd", w2, jnp.eye(KV_B, dtype=F32)).reshape(
            KV_B * CMP_HID, KV_W).astype(BF16)

    def pe_rows(pe):
        halves = pe.reshape(2, CMP_STRIDE, 1, HD)
        flat = jnp.broadcast_to(halves, (2, CMP_STRIDE, KV_B, HD)).reshape(2, 1, CMP_STRIDE * KV_W)
        flat = jnp.broadcast_to(flat, (2, 8, CMP_STRIDE * KV_W)).astype(BF16)
        return flat[0], flat[1]

    peak, pebk = pe_rows(cmp_pe_k[l])
    peav, pebv = pe_rows(cmp_pe_v[l])
    row = lambda a, reps: jnp.tile(a, reps)[None, :]
    bg = jnp.concatenate([b_gate[l], jnp.zeros((LANES - n_gate,), F32)])[None, :]
    return dict(
        nin=norm_in[l][None, :], w_in=w_pack, bg=bg, lng=ln_v_g[l][None, :], lnb=ln_v_b[l][None, :],
        wc=wc, bc=bc, qn=row(q_norm[l], H_B), ksn=row(k_norm_sel[l], KV_B), kwn=row(k_norm_win[l], KV_B),
        kcn=row(k_norm_cmp[l], KV_B), ona=out_norm_a[l][None, :], onb=out_norm_b[l][None, :],
        g512=_block_ones(D_B), g128=_block_ones(KV_W),
        w1k=big1(w_cmp_k1[l]), w1v=big1(w_cmp_v1[l]), w2k=big2(w_cmp_k2[l]), w2v=big2(w_cmp_v2[l]),
        peak=peak, pebk=pebk, peav=peav, pebv=pebv, w_out=w_out[l].astype(BF16))


def _pos_feats(pos):
    pos = np.asarray(pos)
    f = np.zeros((pos.shape[0], HD), np.float32)
    f[:, 0] = SEL_BLK * (pos // SEL_BLK)
    f[:, 1] = pos % SEL_BLK
    f[:, 2] = 1.0
    f[:, 3] = 1.0
    return jnp.asarray(f, BF16)


def _key_aug(k, pos):
    b, n, _ = k.shape
    feats = jnp.broadcast_to(_pos_feats(pos)[None, :, None, :], (b, n, KV_B, HD))
    return jnp.concatenate([k.reshape(b, n, KV_B, HD).astype(BF16), feats], axis=-1).reshape(b, n, 2 * KV_W)


def _query_aug_t(q, pos):
    b, t, _ = q.shape
    pos = np.asarray(pos)
    sl = _slopes()
    f = np.zeros((H_B, HD, t), np.float32)
    f[:, 0, :] = sl[:, None]
    f[:, 1, :] = sl[:, None]
    f[:, 2, :] = -sl[:, None] * (SEL_BLK * (pos // SEL_BLK))[None, :]
    f[:, 3, :] = -sl[:, None] * (pos % SEL_BLK)[None, :]
    qt = jnp.transpose((q * SCALE).astype(BF16).reshape(b, t, H_B, HD), (0, 2, 3, 1))
    feats = jnp.broadcast_to(jnp.asarray(f, BF16)[None], (b, H_B, HD, t))
    return jnp.concatenate([qt, feats], axis=2)


def _prompt_layer(x, wts):
    b, t, _ = x.shape
    n = b * t
    x2d = x.reshape(n, D_MODEL)
    oa, _, q, szb, gates, kc_r, vc_r, ks, vs, kw, vw = _projection(x2d, wts)
    nc = t // CMP_STRIDE
    kc, vc = _compress_prompt(kc_r.reshape(b, nc, CMP_STRIDE * KV_W),
                              vc_r.reshape(b, nc, CMP_STRIDE * KV_W), wts)
    pos = np.arange(t)
    three = lambda a: a.reshape(b, t, KV_W)
    tr = lambda a: jnp.transpose(a.astype(BF16), (0, 2, 1))
    kc_end = np.arange(nc) * CMP_STRIDE + CMP_BLK - 1
    ob_t = _prompt_attention(
        _query_aug_t(q.reshape(b, t, D_B), pos),
        jnp.transpose(gates[:, :3 * H_B].reshape(b, t, 3 * H_B), (0, 2, 1)),
        _key_aug(three(ks), pos), tr(three(vs)), _key_aug(three(kw), pos), tr(three(vw)),
        _key_aug(kc, kc_end), tr(vc))
    ob = jnp.transpose(ob_t, (0, 2, 1)).reshape(n, D_B)
    y = _merge(x2d, oa, ob, szb, wts).reshape(b, t, D_MODEL)
    heads = lambda a: a.reshape(b, t, KV_B, HD)
    n_keep = min(WINDOW, t)
    return y, (heads(kc_r), heads(vc_r), heads(ks), heads(vs),
               heads(kw)[:, t - n_keep:], heads(vw)[:, t - n_keep:])


def _sample_layer(x, l, cache_k_cmp, cache_v_cmp, cache_k_sel, cache_v_sel, k_win_buf, v_win_buf,
                  page_table, wts):
    b, t, _ = x.shape
    n = b * t
    n_pool, page = cache_k_cmp.shape[1], cache_k_cmp.shape[2]
    n_pages = page_table.shape[1]
    past_len = n_pages * page
    x2d = x.reshape(n, D_MODEL)
    oa, vn, q, szb, gates, kc_r, vc_r, ks, vs, kw, vw = _projection(x2d, wts)

    nr = H_B * t
    sl = _slopes()
    slope = jnp.asarray(np.broadcast_to(np.repeat(sl, t)[:, None], (nr, LANES)).copy())
    irow = jnp.asarray(np.broadcast_to(np.tile(np.arange(t, dtype=np.float32), H_B)[:, None], (nr, LANES)).copy())
    qh = jnp.transpose((q * SCALE).reshape(b, t, KV_B, G_B, HD), (0, 2, 3, 1, 4))
    qk = jnp.einsum("bghtd,gk->bghtkd", qh, jnp.eye(KV_B, dtype=F32)).reshape(b, nr, KV_W).astype(BF16)

    minor_pos = lambda a: jnp.transpose(a, (0, 2, 3, 1)).reshape(a.shape[0], KV_W, a.shape[1])
    oc, imp = _sample_compressed(page_table, minor_pos(cache_k_cmp[l]), minor_pos(cache_v_cmp[l]),
                                 qk, slope, irow, wts, past_len)

    n_sb = -(-(past_len + t) // SEL_BLK)
    n_pb = past_len // SEL_BLK
    cur = past_len // SEL_BLK
    wpad = imp.shape[2]
    bias = _sample_topk(imp.reshape(b * KV_B * t, wpad), n_sb, cur).reshape(b, KV_B, 1, t, wpad)
    bias = jnp.broadcast_to(bias[..., :n_pb], (b, KV_B, G_B, t, n_pb)).reshape(b, nr, n_pb)
    qsel = jnp.concatenate([qk, bias.astype(BF16)], axis=-1)

    pad_rows = lambda a: jnp.pad(a.reshape(b, t, KV_W).astype(BF16), ((0, 0), (0, LANES - t), (0, 0)))
    pad_cols = lambda a: jnp.transpose(pad_rows(a), (0, 2, 1))
    gsel = jnp.transpose(gates[:, :3 * H_B].reshape(b, t, 3, H_B), (0, 3, 1, 2)).reshape(b, nr, 3)
    gsel = jnp.pad(gsel, ((0, 0), (0, 0), (0, LANES - 3)))
    o = _sample_selected(page_table, minor_pos(cache_k_sel[l]), minor_pos(cache_v_sel[l]), qsel,
                         pad_cols(ks), pad_rows(vs), minor_pos(k_win_buf), minor_pos(v_win_buf),
                         pad_cols(kw), pad_rows(vw), oc, gsel, slope, irow, past_len)
    o5 = o.reshape(b, KV_B, G_B, t, KV_B, HD)
    ob = jnp.stack([o5[:, g, :, :, g, :] for g in range(KV_B)], axis=1)
    ob = jnp.transpose(ob, (0, 3, 1, 2, 4)).reshape(n, D_B)
    y = _merge(x2d, oa, ob, szb, wts).reshape(b, t, D_MODEL)
    heads = lambda a: a.reshape(b, t, KV_B, HD)
    kw_all = jnp.concatenate([k_win_buf, heads(kw)], axis=1)
    vw_all = jnp.concatenate([v_win_buf, heads(vw)], axis=1)
    return y, (heads(kc_r), heads(vc_r), heads(ks), heads(vs), kw_all[:, t:], vw_all[:, t:],
               vn.reshape(b, t, D_A))


def kernel(x_prompt, x_sample, cache_k_cmp, cache_v_cmp, cache_k_sel, cache_v_sel, state_k_win, state_v_win, page_table, norm_in, w_in, b_gate, ln_v_g, ln_v_b, w_spatial, b_spatial, q_norm, k_norm_cmp, k_norm_sel, k_norm_win, cmp_pe_k, cmp_pe_v, w_cmp_k1, w_cmp_k2, w_cmp_v1, w_cmp_v2, out_norm_a, out_norm_b, w_out):
    params = (norm_in, w_in, b_gate, ln_v_g, ln_v_b, w_spatial, b_spatial, q_norm, k_norm_cmp,
              k_norm_sel, k_norm_win, cmp_pe_k, cmp_pe_v, w_cmp_k1, w_cmp_k2, w_cmp_v1, w_cmp_v2,
              out_norm_a, out_norm_b, w_out)
    depth = w_in.shape[0]
    h_p, h_s = x_prompt, x_sample
    st_p, st_s = [], []
    for l in range(depth):
        wts_p = _layer_weights(l, min(h_p.shape[1], CHUNK), *params)
        wts_s = _layer_weights(l, min(h_s.shape[1], CHUNK), *params)
        h_p, sp = _prompt_layer(h_p, wts_p)
        h_s, ss = _sample_layer(h_s, l, cache_k_cmp, cache_v_cmp, cache_k_sel, cache_v_sel,
                                state_k_win[l], state_v_win[l], page_table, wts_s)
        st_p.append(sp)
        st_s.append(ss)
    new_p = [jnp.stack(ts) for ts in zip(*st_p)]
    new_s = [jnp.stack(ts) for ts in zip(*st_s)]
    return (h_p, h_s, new_p[0], new_p[1], new_p[2], new_p[3], new_p[4], new_p[5],
            new_s[0], new_s[1], new_s[2], new_s[3], new_s[4], new_s[5], new_s[6])
```

```python
import functools

import numpy as np
import jax
import jax.numpy as jnp
from jax import lax
from jax.experimental import pallas as pl
from jax.experimental.pallas import tpu as pltpu

F32 = jnp.float32
BF16 = jnp.bfloat16

D_MODEL = 1024
D_A = 512
H_A = 8
CHUNK = 128
D_B = 512
H_B = 8
HD = 64
KV_B = 2
G_B = 4
KV_W = KV_B * HD
CMP_STRIDE = 16
CMP_BLK = 32
CMP_HID = 128
SEL_BLK = 64
N_SEL = 16
N_LOCAL = 2
WINDOW = 512
Q_BLOCK = 128
EPS = 1e-6
NEG = -1e30
SCALE = HD ** -0.5

LANES = 128
N_PACK = 3456
VMEM_LIMIT = 56 * 1024 * 1024


def _dot(a, b):
    return jnp.dot(a, b, preferred_element_type=F32)


def _dot_nt(a, b):
    return lax.dot_general(a, b, (((1,), (1,)), ((), ())), preferred_element_type=F32)


def _group_mean_sq(t, g_ref):
    t2 = t * t
    hi = t2.astype(BF16)
    lo = (t2 - hi.astype(F32)).astype(BF16)
    g = g_ref[...]
    return (_dot(hi, g) + _dot(lo, g)) * (1.0 / HD)


def _silu(z):
    return z * jax.nn.sigmoid(z)


def _key_feats(pos):
    lane = lax.broadcasted_iota(jnp.int32, (pos.shape[0], LANES), 1)
    sh = SEL_BLK.bit_length() - 1
    blk = (jnp.right_shift(pos, sh) * SEL_BLK).astype(F32)
    off = jnp.bitwise_and(pos, SEL_BLK - 1).astype(F32)
    return jnp.where(lane == HD, blk, jnp.where(lane == HD + 1, off,
                     jnp.where(jnp.logical_or(lane == HD + 2, lane == HD + 3), 1.0, 0.0)))


def _key_aug(k, pos):
    lane = lax.broadcasted_iota(jnp.int32, k.shape, 1)
    feats = _key_feats(pos)
    g0 = jnp.where(lane < HD, k, feats)
    g1 = jnp.where(lane < HD, pltpu.roll(k, HD, 1), feats)
    return jnp.concatenate([g0, g1], axis=1).astype(BF16)


def _proj_values(x_ref, nin_ref, w_ref, bg_ref, lng_ref, lnb_ref, wc_ref, bc_ref, qn_ref, ksn_ref,
                 kwn_ref, ona_ref, g512_ref, g128_ref):
    tn = x_ref.shape[0]
    x = x_ref[...]
    ms = jnp.mean(x * x, axis=-1, keepdims=True)
    xb = (x * lax.rsqrt(ms + EPS) * nin_ref[...]).astype(BF16)

    def proj(lo, n):
        return _dot(xb, w_ref[:, lo:lo + n])

    u = proj(0, D_A)
    v = proj(D_A, D_A)
    mu = jnp.mean(v, axis=-1, keepdims=True)
    vc = v - mu
    var = jnp.mean(vc * vc, axis=-1, keepdims=True)
    vn = vc * lax.rsqrt(var + EPS) * lng_ref[...] + lnb_ref[...]
    vnb = vn.astype(BF16)
    lane = lax.broadcasted_iota(jnp.int32, (CHUNK, LANES), 1)
    rows = []
    for c in range(tn // CHUNK):
        vcb = vnb[c * CHUNK:(c + 1) * CHUNK]
        cols = []
        for m in range(D_A // LANES):
            va = vcb[:, m * LANES:(m + 1) * LANES]
            sa = _dot(wc_ref[2 * m], va)
            sb = _dot(wc_ref[2 * m + 1], va)
            cols.append(jnp.where(lane < HD, sa, sb))
        rows.append(jnp.concatenate(cols, axis=1) + bc_ref[...])
    s = jnp.concatenate(rows, axis=0)
    o = u * s
    o = o * lax.rsqrt(jnp.mean(o * o, axis=-1, keepdims=True) + EPS) * ona_ref[...]
    za = proj(2 * D_A, D_A)
    oa = (o * _silu(za)).astype(BF16)

    q = proj(3 * D_A, D_B)
    q = q * lax.rsqrt(_group_mean_sq(q, g512_ref) + EPS) * qn_ref[...]
    szb = _silu(proj(3 * D_A + D_B, D_B)).astype(BF16)
    base = 3 * D_A + 2 * D_B
    kc = proj(base, KV_W)
    vc = proj(base + KV_W, KV_W)
    ks = proj(base + 2 * KV_W, KV_W)
    ks = ks * lax.rsqrt(_group_mean_sq(ks, g128_ref) + EPS) * ksn_ref[...]
    vs = proj(base + 3 * KV_W, KV_W)
    kw = proj(base + 4 * KV_W, KV_W)
    kw = kw * lax.rsqrt(_group_mean_sq(kw, g128_ref) + EPS) * kwn_ref[...]
    vw = proj(base + 5 * KV_W, KV_W)
    gates = jax.nn.sigmoid(proj(base + 6 * KV_W, LANES) + bg_ref[...])
    return oa, vn, q, szb, gates, kc, vc, ks, vs, kw, vw


N_PROJ_CONSTS = 13


def _proj_sample_kernel(*refs):
    ins, outs = refs[:1 + N_PROJ_CONSTS], refs[1 + N_PROJ_CONSTS:]
    for ref, val in zip(outs, _proj_values(*ins)):
        ref[...] = val


def _proj_prompt_kernel(*refs, seq):
    ins = refs[:1 + N_PROJ_CONSTS]
    (oa_ref, szb_ref, qt_ref, gt_ref, ksa_ref, kwa_ref, kcr_ref, vcr_ref,
     kct_ref, vct_ref, kst_ref, vst_ref, kwt_ref, vwt_ref) = refs[1 + N_PROJ_CONSTS:]
    oa, _, q, szb, gates, kc, vc, ks, vs, kw, vw = _proj_values(*ins)
    tn = oa.shape[0]
    t0 = lax.rem(pl.program_id(0) * tn, seq)
    oa_ref[...] = oa
    szb_ref[...] = szb
    kcr_ref[...] = kc
    vcr_ref[...] = vc
    for ref, val in ((kct_ref, kc), (vct_ref, vc), (kst_ref, ks), (vst_ref, vs), (kwt_ref, kw),
                     (vwt_ref, vw)):
        ref[0] = val.T
    gt_ref[0] = gates.T[:3 * H_B]
    pos_rows = t0 + lax.broadcasted_iota(jnp.int32, (tn, 1), 0)
    ksa_ref[0] = _key_aug(ks, pos_rows)
    kwa_ref[0] = _key_aug(kw, pos_rows)
    pos = t0 + lax.broadcasted_iota(jnp.int32, (HD, tn), 1)
    row = lax.broadcasted_iota(jnp.int32, (HD, tn), 0)
    sh = SEL_BLK.bit_length() - 1
    blk = (jnp.right_shift(pos, sh) * SEL_BLK).astype(F32)
    off = jnp.bitwise_and(pos, SEL_BLK - 1).astype(F32)
    qs = q * SCALE
    for m in range(D_B // LANES):
        qt = qs[:, m * LANES:(m + 1) * LANES].T
        for i in range(LANES // HD):
            h = (LANES // HD) * m + i
            slope = 2.0 ** -(h + 1)
            feats = jnp.where(row < 2, slope, jnp.where(row == 2, -slope * blk,
                              jnp.where(row == 3, -slope * off, 0.0)))
            qt_ref[0, h, :HD, :] = qt[i * HD:(i + 1) * HD].astype(BF16)
            qt_ref[0, h, HD:, :] = feats.astype(BF16)


def _row_tile(n):
    for tn in (256, 128):
        if n % tn == 0:
            return tn
    raise ValueError(f"row count {n} must be a multiple of {CHUNK}")


def _proj_consts(wts):
    return [wts["nin"], wts["w_in"], wts["bg"], wts["lng"], wts["lnb"], wts["wc"], wts["bc"],
            wts["qn"], wts["ksn"], wts["kwn"], wts["ona"], wts["g512"], wts["g128"]]


def _projection_sample(x2d, wts):
    n = x2d.shape[0]
    tn = _row_tile(n)
    full = lambda a: pl.BlockSpec(a.shape, lambda i: (0,) * a.ndim)
    consts = _proj_consts(wts)
    row = lambda w: pl.BlockSpec((tn, w), lambda i: (i, 0))
    out_w = [(D_A, BF16), (D_A, F32), (D_B, F32), (D_B, BF16), (LANES, F32)] + [(KV_W, F32)] * 6
    return pl.pallas_call(
        _proj_sample_kernel,
        grid=(n // tn,),
        in_specs=[row(D_MODEL)] + [full(a) for a in consts],
        out_specs=[row(w) for w, _ in out_w],
        out_shape=[jax.ShapeDtypeStruct((n, w), dt) for w, dt in out_w],
        compiler_params=pltpu.CompilerParams(dimension_semantics=("arbitrary",),
                                             vmem_limit_bytes=VMEM_LIMIT),
        name="projection_sample",
    )(x2d, *consts)


def _projection_prompt(x2d, b, seq, wts):
    n = x2d.shape[0]
    tn = _row_tile(seq)
    tps = seq // tn
    full = lambda a: pl.BlockSpec(a.shape, lambda i: (0,) * a.ndim)
    consts = _proj_consts(wts)
    row = lambda w: pl.BlockSpec((tn, w), lambda i: (i, 0))
    minor = lambda r: pl.BlockSpec((1, r, tn), lambda i: (i // tps, 0, i % tps))
    specs = [(row(D_A), (n, D_A), BF16), (row(D_B), (n, D_B), BF16),
             (pl.BlockSpec((1, H_B, LANES, tn), lambda i: (i // tps, 0, 0, i % tps)),
              (b, H_B, LANES, seq), BF16),
             (minor(3 * H_B), (b, 3 * H_B, seq), F32)]
    specs += [(pl.BlockSpec((1, tn, 2 * KV_W), lambda i: (i // tps, i % tps, 0)),
               (b, seq, 2 * KV_W), BF16)] * 2
    specs += [(row(KV_W), (n, KV_W), F32)] * 2
    specs += [(minor(KV_W), (b, KV_W, seq), F32)] * 6
    return pl.pallas_call(
        functools.partial(_proj_prompt_kernel, seq=seq),
        grid=(n // tn,),
        in_specs=[row(D_MODEL)] + [full(a) for a in consts],
        out_specs=[s for s, _, _ in specs],
        out_shape=[jax.ShapeDtypeStruct(shape, dt) for _, shape, dt in specs],
        compiler_params=pltpu.CompilerParams(dimension_semantics=("arbitrary",),
                                             vmem_limit_bytes=VMEM_LIMIT),
        name="projection_prompt",
    )(x2d, *consts)


def _merge_kernel(x_ref, oa_ref, ob_ref, szb_ref, onb_ref, w_ref, y_ref):
    ob = ob_ref[...]
    ob = ob * lax.rsqrt(jnp.mean(ob * ob, axis=-1, keepdims=True) + EPS) * onb_ref[...]
    ob = (ob * szb_ref[...].astype(F32)).astype(BF16)
    y_ref[...] = x_ref[...] + _dot(oa_ref[...], w_ref[:D_A, :]) + _dot(ob, w_ref[D_A:, :])


def _merge(x2d, oa, ob, szb, wts):
    n = x2d.shape[0]
    tn = _row_tile(n)
    row = lambda w: pl.BlockSpec((tn, w), lambda i: (i, 0))
    full = lambda a: pl.BlockSpec(a.shape, lambda i: (0,) * a.ndim)
    return pl.pallas_call(
        _merge_kernel,
        grid=(n // tn,),
        in_specs=[row(D_MODEL), row(D_A), row(D_B), row(D_B), full(wts["onb"]), full(wts["w_out"])],
        out_specs=row(D_MODEL),
        out_shape=jax.ShapeDtypeStruct((n, D_MODEL), F32),
        compiler_params=pltpu.CompilerParams(dimension_semantics=("arbitrary",),
                                             vmem_limit_bytes=VMEM_LIMIT),
        name="merge",
    )(x2d, oa, ob, szb, wts["onb"], wts["w_out"])


def _compress_math(xb, w1_ref, pea_ref, peb_ref, w2_ref):
    h1 = _dot(xb, w1_ref[0])
    h2 = _dot(xb, w1_ref[1])
    r = h2.shape[0]
    h2n = pltpu.roll(h2, r - 1, 0)
    pe = _dot(pea_ref[...], w1_ref[0]) + _dot(peb_ref[...], w1_ref[1])
    pre = h1 + h2n + pe[0:1]
    return _dot(_silu(pre).astype(BF16), w2_ref[...])


def _compress_kernel(rk_ref, rv_ref, w1k_ref, w1v_ref, peak_ref, pebk_ref, peav_ref, pebv_ref,
                     w2k_ref, w2v_ref, kn_ref, g128_ref, kc_ref, vc_ref):
    kc = _compress_math(rk_ref[0].astype(BF16), w1k_ref, peak_ref, pebk_ref, w2k_ref)
    kc = kc * lax.rsqrt(_group_mean_sq(kc, g128_ref) + EPS) * kn_ref[...]
    end = CMP_STRIDE * lax.broadcasted_iota(jnp.int32, (kc.shape[0], 1), 0) + (CMP_BLK - 1)
    kc_ref[0] = _key_aug(kc, end)
    vc = _compress_math(rv_ref[0].astype(BF16), w1v_ref, peav_ref, pebv_ref, w2v_ref)
    vc_ref[0] = vc.T.astype(BF16)


def _cmp_consts(wts):
    return [wts["w1k"], wts["w1v"], wts["peak"], wts["pebk"], wts["peav"], wts["pebv"],
            wts["w2k"], wts["w2v"], wts["kcn"], wts["g128"]]


def _compress_prompt(rk, rv, wts):
    b, r, w = rk.shape
    consts = _cmp_consts(wts)
    full = lambda a: pl.BlockSpec(a.shape, lambda i: (0,) * a.ndim)
    blk = pl.BlockSpec((1, r, w), lambda i: (i, 0, 0))
    return pl.pallas_call(
        _compress_kernel,
        grid=(b,),
        in_specs=[blk, blk] + [full(a) for a in consts],
        out_specs=[pl.BlockSpec((1, r, 2 * KV_W), lambda i: (i, 0, 0)),
                   pl.BlockSpec((1, KV_W, r), lambda i: (i, 0, 0))],
        out_shape=[jax.ShapeDtypeStruct((b, r, 2 * KV_W), BF16),
                   jax.ShapeDtypeStruct((b, KV_W, r), BF16)],
        compiler_params=pltpu.CompilerParams(dimension_semantics=("arbitrary",),
                                             vmem_limit_bytes=VMEM_LIMIT),
        name="compress_prompt",
    )(rk, rv, *consts)


def _topk_rows(score, k):
    n = score.shape[0]
    ridx = lax.broadcasted_iota(jnp.int32, score.shape, 0).astype(F32)
    sel = jnp.zeros(score.shape, jnp.bool_)
    for _ in range(k):
        m = jnp.max(score, axis=0, keepdims=True)
        idx = jnp.min(jnp.where(score == m, ridx, float(n)), axis=0, keepdims=True)
        hit = ridx == idx
        sel = jnp.logical_or(sel, hit)
        score = jnp.where(hit, -jnp.inf, score)
    return sel


def _online_update(carry, s, vt):
    m, l, acc = carry
    m_new = jnp.maximum(m, jnp.max(s, axis=0, keepdims=True))
    alpha = jnp.exp(m - m_new)
    p = jnp.exp(s - m_new)
    l = alpha * l + jnp.sum(p, axis=0, keepdims=True)
    acc = alpha * acc + _dot(vt, p.astype(BF16))
    return m_new, l, acc


def _pattn_kernel(qt_ref, gt_ref, ks_ref, vst_ref, kw_ref, vwt_ref, kc_ref, vct_ref, o_ref,
                  imp_sc, bias_sc):
    t = pl.program_id(1)
    nb = kc_ref.shape[1]
    n_sb = bias_sc.shape[1]
    qw = G_B * Q_BLOCK
    pair = 2 * Q_BLOCK
    q0 = t * Q_BLOCK
    qpos = q0 + lax.broadcasted_iota(jnp.int32, (1, Q_BLOCK), 1)
    qpos4 = jnp.concatenate([qpos] * G_B, axis=1)
    gls = [slice(g * LANES, (g + 1) * LANES) for g in range(KV_B)]
    gds = [slice(g * HD, (g + 1) * HD) for g in range(KV_B)]
    qts = [jnp.concatenate([qt_ref[0, G_B * g + hh] for hh in range(G_B)], axis=1)
           for g in range(KV_B)]

    o_c = []
    for g in range(KV_B):
        sc = _dot(kc_ref[0, :, gls[g]], qts[g])
        ci = lax.broadcasted_iota(jnp.int32, (nb, qw), 0)
        valid = jnp.logical_and(qpos4 >= CMP_STRIDE * ci + (CMP_BLK - 1), ci < nb - 1)
        sc = jnp.where(valid, sc, NEG)
        pc = jnp.where(valid, jnp.exp(sc - jnp.max(sc, axis=0, keepdims=True)), 0.0)
        lc = jnp.sum(pc, axis=0, keepdims=True)
        pc = pc * (1.0 / jnp.where(lc > 0.0, lc, 1.0))
        o_c.append(_dot(vct_ref[0, gds[g], :], pc.astype(BF16)))
        imp_sc[g] = sum(pc[:, hh * Q_BLOCK:(hh + 1) * Q_BLOCK] for hh in range(G_B))
        ratio = SEL_BLK // CMP_STRIDE
        imp = sum(imp_sc[g, pl.ds(r, n_sb, stride=ratio), :] for r in range(ratio))
        bi = lax.broadcasted_iota(jnp.int32, (n_sb, Q_BLOCK), 0)
        cur = jnp.right_shift(qpos, SEL_BLK.bit_length() - 1)
        forced = jnp.logical_or(bi == 0, bi > cur - N_LOCAL)
        score = jnp.where(bi > cur, -1e9, jnp.where(forced, 1e9, imp))
        bias_sc[g] = jnp.where(_topk_rows(score, min(N_SEL, n_sb)), 0.0, NEG)

    piota = lax.broadcasted_iota(jnp.int32, (pair, qw), 0)

    def sel_pair(p, carry, causal):
        k0 = pl.multiple_of(p * pair, pair)
        out = []
        for g in range(KV_B):
            s = _dot(ks_ref[0, pl.ds(k0, pair), gls[g]], qts[g])
            rows = [jnp.broadcast_to(bias_sc[g, pl.ds((pair // SEL_BLK) * p + i, 1), :],
                                     (SEL_BLK, Q_BLOCK)) for i in range(pair // SEL_BLK)]
            bias = jnp.concatenate(rows, axis=0)
            s = s + jnp.concatenate([bias] * G_B, axis=1)
            if causal:
                s = jnp.where(k0 + piota <= qpos4, s, NEG)
            out.append(_online_update(carry[g], s, vst_ref[0, gds[g], pl.ds(k0, pair)].astype(BF16)))
        return tuple(out)

    init = (jnp.full((1, qw), NEG, F32), jnp.zeros((1, qw), F32), jnp.zeros((HD, qw), F32))
    last = t // 2
    carry = lax.fori_loop(0, last, lambda p, c: sel_pair(p, c, False), (init,) * KV_B)
    carry = sel_pair(last, carry, True)
    o_s = [a * (1.0 / l) for _, l, a in carry]

    span = WINDOW + Q_BLOCK
    w0 = pl.multiple_of(jnp.maximum(q0 - WINDOW, 0), Q_BLOCK)
    dist = qpos4 - (w0 + lax.broadcasted_iota(jnp.int32, (span, qw), 0))
    wvalid = jnp.logical_and(dist >= 0, dist < WINDOW)
    o_w = []
    for g in range(KV_B):
        s = jnp.where(wvalid, _dot(kw_ref[0, pl.ds(w0, span), gls[g]], qts[g]), NEG)
        p = jnp.exp(s - jnp.max(s, axis=0, keepdims=True))
        l = jnp.sum(p, axis=0, keepdims=True)
        o_w.append(_dot(vwt_ref[0, gds[g], pl.ds(w0, span)].astype(BF16), p.astype(BF16)) * (1.0 / l))

    for m in range(H_B // 2):
        halves = []
        for h in (2 * m, 2 * m + 1):
            g, hs = h // G_B, slice((h % G_B) * Q_BLOCK, (h % G_B + 1) * Q_BLOCK)
            halves.append(gt_ref[0, h:h + 1, :] * o_c[g][:, hs]
                          + gt_ref[0, H_B + h:H_B + h + 1, :] * o_s[g][:, hs]
                          + gt_ref[0, 2 * H_B + h:2 * H_B + h + 1, :] * o_w[g][:, hs])
        o_ref[:, m * LANES:(m + 1) * LANES] = jnp.concatenate(halves, axis=0).T


def _prompt_attention(qt, gt, ksa, vst, kwa, vwt, kca, vct):
    b, _, _, t = qt.shape
    nb = kca.shape[1]
    n_sb = t // SEL_BLK
    per_b = lambda a: pl.BlockSpec((1,) + a.shape[1:], lambda i, j: (i,) + (0,) * (a.ndim - 1))
    return pl.pallas_call(
        _pattn_kernel,
        grid=(b, t // Q_BLOCK),
        in_specs=[pl.BlockSpec((1, H_B, LANES, Q_BLOCK), lambda i, j: (i, 0, 0, j)),
                  pl.BlockSpec((1, 3 * H_B, Q_BLOCK), lambda i, j: (i, 0, j)),
                  per_b(ksa), per_b(vst), per_b(kwa), per_b(vwt), per_b(kca), per_b(vct)],
        out_specs=pl.BlockSpec((Q_BLOCK, D_B), lambda i, j: (i * (t // Q_BLOCK) + j, 0)),
        out_shape=jax.ShapeDtypeStruct((b * t, D_B), F32),
        scratch_shapes=[pltpu.VMEM((KV_B, nb, Q_BLOCK), F32), pltpu.VMEM((KV_B, n_sb, Q_BLOCK), F32)],
        compiler_params=pltpu.CompilerParams(dimension_semantics=("arbitrary", "arbitrary"),
                                             vmem_limit_bytes=VMEM_LIMIT),
        name="prompt_attention",
    )(qt, gt, ksa, vst, kwa, vwt, kca, vct)


def _page_copy(hbm, buf, sem, page, slot, which, p):
    return pltpu.make_async_copy(hbm.at[page], buf.at[slot, which, p], sem.at[slot, which])


def _gather_pages(pt_ref, hbm_a, hbm_b, buf, sem):
    b = pl.program_id(0)
    nb = pl.num_programs(0)
    n_pages = pt_ref.shape[1]
    slot = lax.rem(b, 2)

    def issue(bb, sl):
        for p in range(n_pages):
            page = pt_ref[bb, p]
            _page_copy(hbm_a, buf, sem, page, sl, 0, p).start()
            _page_copy(hbm_b, buf, sem, page, sl, 1, p).start()

    @pl.when(b == 0)
    def _():
        issue(0, 0)

    @pl.when(b + 1 < nb)
    def _():
        issue(b + 1, 1 - slot)

    for p in range(n_pages):
        _page_copy(hbm_a, buf, sem, 0, slot, 0, p).wait()
        _page_copy(hbm_b, buf, sem, 0, slot, 1, p).wait()
    return slot


SUBLANES = 8


def _sublane_transpose(tiles):
    tiles = list(tiles)
    sub = lax.broadcasted_iota(jnp.int32, tiles[0].shape, 0)
    k = 1
    while k < SUBLANES:
        hi = jnp.bitwise_and(sub, k) != 0
        for c in range(SUBLANES):
            if c & k:
                continue
            a, b = tiles[c], tiles[c | k]
            tiles[c] = jnp.where(hi, pltpu.roll(b, k, 0), a)
            tiles[c | k] = jnp.where(hi, b, pltpu.roll(a, SUBLANES - k, 0))
        k *= 2
    return tiles


def _chunk_rows(buf, slot, which, lhs_sc):
    n_pages, _, page = buf.shape[2:]
    cpp = page // CMP_STRIDE
    assert cpp == SUBLANES and CMP_STRIDE % SUBLANES == 0
    for pp in range(n_pages // 2):
        per_j = []
        for i in range(2):
            x = buf[slot, which, 2 * pp + i].T
            cols = [None] * CMP_STRIDE
            for h in range(CMP_STRIDE // SUBLANES):
                tiles = [x[CMP_STRIDE * c + SUBLANES * h:CMP_STRIDE * c + SUBLANES * (h + 1)]
                         for c in range(cpp)]
                cols[SUBLANES * h:SUBLANES * (h + 1)] = _sublane_transpose(tiles)
            per_j.append(cols)
        for j in range(CMP_STRIDE):
            lhs_sc[which, 2 * cpp * pp:2 * cpp * (pp + 1), KV_W * j:KV_W * (j + 1)] = (
                jnp.concatenate([per_j[0][j], per_j[1][j]], axis=0).astype(BF16))


def _scmp_kernel(pt_ref, kc_hbm, vc_hbm, qk_ref, slope_ref, irow_ref, w1k_ref, w1v_ref, peak_ref,
                 pebk_ref, peav_ref, pebv_ref, w2k_ref, w2v_ref, kn_ref, g128_ref, gsum_ref,
                 oc_ref, imp_ref, buf, sem, lhs_sc, *, past_len):
    slot = _gather_pages(pt_ref, kc_hbm, vc_hbm, buf, sem)
    _chunk_rows(buf, slot, 0, lhs_sc)
    _chunk_rows(buf, slot, 1, lhs_sc)
    kc = _compress_math(lhs_sc[0], w1k_ref, peak_ref, pebk_ref, w2k_ref)
    kc = kc * lax.rsqrt(_group_mean_sq(kc, g128_ref) + EPS) * kn_ref[...]
    vc = _compress_math(lhs_sc[1], w1v_ref, peav_ref, pebv_ref, w2v_ref)
    nb = kc.shape[0]
    nr = qk_ref.shape[1]

    s = _dot_nt(qk_ref[0], kc.astype(BF16))
    ci = lax.broadcasted_iota(jnp.int32, (nr, nb), 1)
    end = (CMP_STRIDE * ci + (CMP_BLK - 1)).astype(F32)
    dist = (past_len + irow_ref[:, 0:1]) - end
    valid = jnp.logical_and(dist >= 0.0, ci < nb - 1)
    s = jnp.where(valid, s - slope_ref[:, 0:1] * dist, NEG)
    p = jnp.where(valid, jnp.exp(s - jnp.max(s, axis=-1, keepdims=True)), 0.0)
    l = jnp.sum(p, axis=-1, keepdims=True)
    p = p * (1.0 / jnp.where(l > 0.0, l, 1.0))
    oc_ref[0] = _dot(p.astype(BF16), vc.astype(BF16))

    tq = nr // H_B
    parts = []
    for g in range(KV_B):
        base = g * G_B * tq
        parts.append(sum(p[base + hh * tq:base + (hh + 1) * tq] for hh in range(G_B)))
    imp = jnp.concatenate(parts, axis=0)
    hi = imp.astype(BF16)
    r1 = imp - hi.astype(F32)
    mid = r1.astype(BF16)
    lo = (r1 - mid.astype(F32)).astype(BF16)
    gs = gsum_ref[...]
    imp_ref[0] = _dot(hi, gs) + _dot(mid, gs) + _dot(lo, gs)


def _sample_compressed(page_table, kc_pages, vc_pages, qk, slope, irow, wts, past_len):
    b, n_pages = page_table.shape
    page = kc_pages.shape[2]
    cpp = page // CMP_STRIDE
    nb = n_pages * cpp
    nr = qk.shape[1]
    ratio = SEL_BLK // CMP_STRIDE
    n_blk = -(-(nb // ratio + 1) // LANES) * LANES
    gsum = jnp.asarray(np.arange(nb)[:, None] // ratio == np.arange(n_blk)[None, :], BF16)
    consts = [slope, irow] + _cmp_consts(wts) + [gsum]
    full = lambda a: pl.BlockSpec(a.shape, lambda i, pt: (0,) * a.ndim)
    grid_spec = pltpu.PrefetchScalarGridSpec(
        num_scalar_prefetch=1,
        grid=(b,),
        in_specs=[pl.BlockSpec(memory_space=pl.ANY), pl.BlockSpec(memory_space=pl.ANY),
                  pl.BlockSpec((1, nr, LANES), lambda i, pt: (i, 0, 0))] + [full(a) for a in consts],
        out_specs=[pl.BlockSpec((1, nr, KV_W), lambda i, pt: (i, 0, 0)),
                   pl.BlockSpec((1, nr // G_B, n_blk), lambda i, pt: (i, 0, 0))],
        scratch_shapes=[pltpu.VMEM((2, 2, n_pages, KV_W, page), F32),
                        pltpu.SemaphoreType.DMA((2, 2)),
                        pltpu.VMEM((2, nb, CMP_STRIDE * KV_W), BF16)],
    )
    return pl.pallas_call(
        functools.partial(_scmp_kernel, past_len=float(past_len)),
        grid_spec=grid_spec,
        out_shape=[jax.ShapeDtypeStruct((b, nr, KV_W), F32),
                   jax.ShapeDtypeStruct((b, nr // G_B, n_blk), F32)],
        compiler_params=pltpu.CompilerParams(dimension_semantics=("arbitrary",),
                                             vmem_limit_bytes=VMEM_LIMIT),
        name="sample_compressed",
    )(page_table, kc_pages, vc_pages, qk, *consts)


def _topk_lanes_kernel(imp_ref, out_ref, *, n_blocks, cur, k):
    imp = imp_ref[...]
    bi = lax.broadcasted_iota(jnp.int32, imp.shape, 1).astype(F32)
    forced = jnp.logical_or(bi == 0, bi > cur - N_LOCAL)
    score = jnp.where(bi > cur, -1e9, jnp.where(forced, 1e9, imp))
    score = jnp.where(bi < n_blocks, score, -jnp.inf)
    width = imp.shape[1]
    sel = jnp.zeros(imp.shape, jnp.bool_)
    for _ in range(k):
        m = jnp.max(score, axis=-1, keepdims=True)
        idx = jnp.min(jnp.where(score == m, bi, float(width)), axis=-1, keepdims=True)
        hit = bi == idx
        sel = jnp.logical_or(sel, hit)
        score = jnp.where(hit, -jnp.inf, score)
    out_ref[...] = jnp.where(sel, 0.0, NEG)


def _sample_topk(imp2d, n_blocks, cur):
    n, w = imp2d.shape
    tr = 256 if n % 256 == 0 else n
    blk = pl.BlockSpec((tr, w), lambda i: (i, 0))
    return pl.pallas_call(
        functools.partial(_topk_lanes_kernel, n_blocks=n_blocks, cur=cur, k=min(N_SEL, n_blocks)),
        grid=(n // tr,),
        in_specs=[blk],
        out_specs=blk,
        out_shape=jax.ShapeDtypeStruct((n, w), F32),
        compiler_params=pltpu.CompilerParams(dimension_semantics=("arbitrary",)),
        name="sample_topk",
    )(imp2d)


def _ssel_kernel(pt_ref, ks_hbm, vs_hbm, qsel_ref, knewt_ref, vnew_ref, kwst_ref, vwst_ref, kwnewt_ref,
                 vwnew_ref, oc_ref, gsel_ref, slope_ref, irow_ref, o_ref,
                 buf, sem, s_sc, oh_sc, *, past_len):
    b = pl.program_id(0)
    n_pages, _, page = buf.shape[2:]
    nr = qsel_ref.shape[1]
    ppt = 2 if n_pages % 2 == 0 else 1
    tile = ppt * page

    @pl.when(b == 0)
    def _():
        bi = lax.broadcasted_iota(jnp.int32, oh_sc.shape, 0)
        ki = lax.broadcasted_iota(jnp.int32, oh_sc.shape, 1)
        oh_sc[...] = jnp.where(jnp.right_shift(ki, SEL_BLK.bit_length() - 1) == bi, 1.0, 0.0).astype(BF16)

    slot = _gather_pages(pt_ref, ks_hbm, vs_hbm, buf, sem)
    slope = slope_ref[:, 0:1]
    irow = irow_ref[:, 0:1]
    qpos = past_len + irow
    qall = qsel_ref[0]
    qk = qall[:, :KV_W]

    def key_tile(which, j):
        return jnp.concatenate([buf[slot, which, ppt * j + i] for i in range(ppt)], axis=1).astype(BF16)

    lane = lax.broadcasted_iota(jnp.int32, (nr, tile), 1).astype(F32)
    mrun = jnp.full((nr, tile), NEG, F32)
    for j in range(n_pages // ppt):
        kcat = jnp.concatenate([key_tile(0, j), oh_sc[:, j * tile:(j + 1) * tile]], axis=0)
        s = _dot(qall, kcat) - slope * (qpos - (float(j * tile) + lane))
        s_sc[:, j * tile:(j + 1) * tile] = s
        mrun = jnp.maximum(mrun, s)
    nl = lax.broadcasted_iota(jnp.int32, (nr, LANES), 1).astype(F32)
    dn = irow - nl
    s_new = jnp.where(dn >= 0.0, _dot(qk, knewt_ref[0]) - slope * dn, NEG)
    m = jnp.maximum(jnp.max(mrun, axis=-1, keepdims=True), jnp.max(s_new, axis=-1, keepdims=True))

    p_new = jnp.exp(s_new - m)
    acc = _dot(p_new.astype(BF16), vnew_ref[0])
    lrun = jnp.zeros((nr, tile), F32)
    for j in range(n_pages // ppt):
        p = jnp.exp(s_sc[:, j * tile:(j + 1) * tile] - m)
        acc = acc + _dot_nt(p.astype(BF16), key_tile(1, j))
        lrun = lrun + p
    l = jnp.sum(lrun, axis=-1, keepdims=True) + jnp.sum(p_new, axis=-1, keepdims=True)
    o_s = acc * (1.0 / l)

    n_buf = kwst_ref.shape[2]
    wl = lax.broadcasted_iota(jnp.int32, (nr, n_buf), 1).astype(F32)
    dw = (n_buf + irow) - wl
    s_w = jnp.where(jnp.logical_and(dw >= 0.0, dw < WINDOW),
                    _dot(qk, kwst_ref[0].astype(BF16)) - slope * dw, NEG)
    s_wn = jnp.where(dn >= 0.0, _dot(qk, kwnewt_ref[0]) - slope * dn, NEG)
    m_w = jnp.maximum(jnp.max(s_w, axis=-1, keepdims=True), jnp.max(s_wn, axis=-1, keepdims=True))
    p_w = jnp.exp(s_w - m_w)
    p_wn = jnp.exp(s_wn - m_w)
    l_w = jnp.sum(p_w, axis=-1, keepdims=True) + jnp.sum(p_wn, axis=-1, keepdims=True)
    o_w = (_dot_nt(p_w.astype(BF16), vwst_ref[0].astype(BF16))
           + _dot(p_wn.astype(BF16), vwnew_ref[0])) * (1.0 / l_w)

    gsel = gsel_ref[0]
    o_ref[0] = gsel[:, 0:1] * oc_ref[0] + gsel[:, 1:2] * o_s + gsel[:, 2:3] * o_w


def _sample_selected(page_table, ks_pages, vs_pages, qsel, knewt, vnew, kwst, vwst, kwnewt, vwnew, oc,
                     gsel, slope, irow, past_len):
    b, n_pages = page_table.shape
    page = ks_pages.shape[2]
    n_keys = n_pages * page
    nr = qsel.shape[1]
    per_b = lambda a: pl.BlockSpec((1,) + a.shape[1:], lambda i, pt: (i,) + (0,) * (a.ndim - 1))
    full = lambda a: pl.BlockSpec(a.shape, lambda i, pt: (0,) * a.ndim)
    grid_spec = pltpu.PrefetchScalarGridSpec(
        num_scalar_prefetch=1,
        grid=(b,),
        in_specs=[pl.BlockSpec(memory_space=pl.ANY), pl.BlockSpec(memory_space=pl.ANY)]
                 + [per_b(a) for a in (qsel, knewt, vnew, kwst, vwst, kwnewt, vwnew, oc, gsel)]
                 + [full(slope), full(irow)],
        out_specs=pl.BlockSpec((1, nr, KV_W), lambda i, pt: (i, 0, 0)),
        scratch_shapes=[pltpu.VMEM((2, 2, n_pages, KV_W, page), F32),
                        pltpu.SemaphoreType.DMA((2, 2)),
                        pltpu.VMEM((nr, n_keys), F32),
                        pltpu.VMEM((n_keys // SEL_BLK, n_keys), BF16)],
    )
    return pl.pallas_call(
        functools.partial(_ssel_kernel, past_len=float(past_len)),
        grid_spec=grid_spec,
        out_shape=jax.ShapeDtypeStruct((b, nr, KV_W), F32),
        compiler_params=pltpu.CompilerParams(dimension_semantics=("arbitrary",),
                                             vmem_limit_bytes=VMEM_LIMIT),
        name="sample_selected",
    )(page_table, ks_pages, vs_pages, qsel, knewt, vnew, kwst, vwst, kwnewt, vwnew, oc, gsel, slope, irow)


def _slopes():
    return np.asarray([2.0 ** -(h + 1) for h in range(H_B)], np.float32)


def _block_ones(n):
    idx = np.arange(n) // HD
    return jnp.asarray(idx[:, None] == idx[None, :], BF16)


def _layer_weights(l, t_chunk, norm_in, w_in, b_gate, ln_v_g, ln_v_b, w_spatial, b_spatial, q_norm,
                   k_norm_cmp, k_norm_sel, k_norm_win, cmp_pe_k, cmp_pe_v, w_cmp_k1, w_cmp_k2,
                   w_cmp_v1, w_cmp_v2, out_norm_a, out_norm_b, w_out):
    w = w_in[l]
    n_main = 3 * D_A + 2 * D_B
    n_gate = 3 * H_B
    w_pack = jnp.concatenate([w[:, :n_main], w[:, n_main + n_gate:], w[:, n_main:n_main + n_gate],
                              jnp.zeros((D_MODEL, N_PACK - w.shape[1]), F32)], axis=1).astype(BF16)
    c = t_chunk
    rep = CHUNK // c
    tri = jnp.where(jnp.tril(jnp.ones((c, c), bool)), w_spatial[l][:, :c, :c], 0.0)
    wc = jnp.einsum("ab,hij->haibj", jnp.eye(rep, dtype=F32), tri).reshape(H_A, CHUNK, CHUNK).astype(BF16)
    bc = jnp.tile(jnp.repeat(b_spatial[l][:, :c].T, D_A // H_A, axis=1), (rep, 1))

    def big1(w1):
        w1r = w1.reshape(2, CMP_STRIDE, HD, CMP_HID)
        return jnp.einsum("ajdh,ck->ajcdkh", w1r, jnp.eye(KV_B, dtype=F32)).reshape(
            2, CMP_STRIDE * KV_W, KV_B * CMP_HID).astype(BF16)

    def big2(w2):
        return jnp.einsum("hd,kc->khcd", w2, jnp.eye(KV_B, dtype=F32)).reshape(
            KV_B * CMP_HID, KV_W).astype(BF16)

    def pe_rows(pe):
        halves = pe.reshape(2, CMP_STRIDE, 1, HD)
        flat = jnp.broadcast_to(halves, (2, CMP_STRIDE, KV_B, HD)).reshape(2, 1, CMP_STRIDE * KV_W)
        flat = jnp.broadcast_to(flat, (2, 8, CMP_STRIDE * KV_W)).astype(BF16)
        return flat[0], flat[1]

    peak, pebk = pe_rows(cmp_pe_k[l])
    peav, pebv = pe_rows(cmp_pe_v[l])
    row = lambda a, reps: jnp.tile(a, reps)[None, :]
    bg = jnp.concatenate([b_gate[l], jnp.zeros((LANES - n_gate,), F32)])[None, :]
    return dict(
        nin=norm_in[l][None, :], w_in=w_pack, bg=bg, lng=ln_v_g[l][None, :], lnb=ln_v_b[l][None, :],
        wc=wc, bc=bc, qn=row(q_norm[l], H_B), ksn=row(k_norm_sel[l], KV_B), kwn=row(k_norm_win[l], KV_B),
        kcn=row(k_norm_cmp[l], KV_B), ona=out_norm_a[l][None, :], onb=out_norm_b[l][None, :],
        g512=_block_ones(D_B), g128=_block_ones(KV_W),
        w1k=big1(w_cmp_k1[l]), w1v=big1(w_cmp_v1[l]), w2k=big2(w_cmp_k2[l]), w2v=big2(w_cmp_v2[l]),
        peak=peak, pebk=pebk, peav=peav, pebv=pebv, w_out=w_out[l].astype(BF16))


def _prompt_layer(x, wts):
    b, t, _ = x.shape
    n = b * t
    x2d = x.reshape(n, D_MODEL)
    (oa, szb, qt, gt, ksa, kwa, kc_r, vc_r,
     kct, vct, kst, vst, kwt, vwt) = _projection_prompt(x2d, b, t, wts)
    nc = t // CMP_STRIDE
    kca, vcct = _compress_prompt(kc_r.reshape(b, nc, CMP_STRIDE * KV_W),
                                 vc_r.reshape(b, nc, CMP_STRIDE * KV_W), wts)
    ob = _prompt_attention(qt, gt, ksa, vst, kwa, vwt, kca, vcct)
    y = _merge(x2d, oa, ob, szb, wts).reshape(b, t, D_MODEL)
    heads = lambda a: jnp.transpose(a.reshape(b, KV_B, HD, a.shape[2]), (0, 3, 1, 2))
    n_keep = min(WINDOW, t)
    return y, (heads(kct), heads(vct), heads(kst), heads(vst),
               heads(kwt[:, :, t - n_keep:]), heads(vwt[:, :, t - n_keep:]))


def _sample_layer(x, l, cache_k_cmp, cache_v_cmp, cache_k_sel, cache_v_sel, k_win_buf, v_win_buf,
                  page_table, wts):
    b, t, _ = x.shape
    n = b * t
    n_pool, page = cache_k_cmp.shape[1], cache_k_cmp.shape[2]
    n_pages = page_table.shape[1]
    past_len = n_pages * page
    x2d = x.reshape(n, D_MODEL)
    oa, vn, q, szb, gates, kc_r, vc_r, ks, vs, kw, vw = _projection_sample(x2d, wts)

    nr = H_B * t
    sl = _slopes()
    slope = jnp.asarray(np.broadcast_to(np.repeat(sl, t)[:, None], (nr, LANES)).copy())
    irow = jnp.asarray(np.broadcast_to(np.tile(np.arange(t, dtype=np.float32), H_B)[:, None], (nr, LANES)).copy())
    qh = jnp.transpose((q * SCALE).reshape(b, t, KV_B, G_B, HD), (0, 2, 3, 1, 4))
    qk = jnp.einsum("bghtd,gk->bghtkd", qh, jnp.eye(KV_B, dtype=F32)).reshape(b, nr, KV_W).astype(BF16)

    minor_pos = lambda a: jnp.transpose(a, (0, 2, 3, 1)).reshape(a.shape[0], KV_W, a.shape[1])
    oc, imp = _sample_compressed(page_table, minor_pos(cache_k_cmp[l]), minor_pos(cache_v_cmp[l]),
                                 qk, slope, irow, wts, past_len)

    n_sb = -(-(past_len + t) // SEL_BLK)
    n_pb = past_len // SEL_BLK
    cur = past_len // SEL_BLK
    wpad = imp.shape[2]
    bias = _sample_topk(imp.reshape(b * KV_B * t, wpad), n_sb, cur).reshape(b, KV_B, 1, t, wpad)
    bias = jnp.broadcast_to(bias[..., :n_pb], (b, KV_B, G_B, t, n_pb)).reshape(b, nr, n_pb)
    qsel = jnp.concatenate([qk, bias.astype(BF16)], axis=-1)

    pad_rows = lambda a: jnp.pad(a.reshape(b, t, KV_W).astype(BF16), ((0, 0), (0, LANES - t), (0, 0)))
    pad_cols = lambda a: jnp.transpose(pad_rows(a), (0, 2, 1))
    gsel = jnp.transpose(gates[:, :3 * H_B].reshape(b, t, 3, H_B), (0, 3, 1, 2)).reshape(b, nr, 3)
    gsel = jnp.pad(gsel, ((0, 0), (0, 0), (0, LANES - 3)))
    o = _sample_selected(page_table, minor_pos(cache_k_sel[l]), minor_pos(cache_v_sel[l]), qsel,
                         pad_cols(ks), pad_rows(vs), minor_pos(k_win_buf), minor_pos(v_win_buf),
                         pad_cols(kw), pad_rows(vw), oc, gsel, slope, irow, past_len)
    o5 = o.reshape(b, KV_B, G_B, t, KV_B, HD)
    ob = jnp.stack([o5[:, g, :, :, g, :] for g in range(KV_B)], axis=1)
    ob = jnp.transpose(ob, (0, 3, 1, 2, 4)).reshape(n, D_B)
    y = _merge(x2d, oa, ob, szb, wts).reshape(b, t, D_MODEL)
    heads = lambda a: a.reshape(b, t, KV_B, HD)
    kw_all = jnp.concatenate([k_win_buf, heads(kw)], axis=1)
    vw_all = jnp.concatenate([v_win_buf, heads(vw)], axis=1)
    return y, (heads(kc_r), heads(vc_r), heads(ks), heads(vs), kw_all[:, t:], vw_all[:, t:],
               vn.reshape(b, t, D_A))


def kernel(x_prompt, x_sample, cache_k_cmp, cache_v_cmp, cache_k_sel, cache_v_sel, state_k_win, state_v_win, page_table, norm_in, w_in, b_gate, ln_v_g, ln_v_b, w_spatial, b_spatial, q_norm, k_norm_cmp, k_norm_sel, k_norm_win, cmp_pe_k, cmp_pe_v, w_cmp_k1, w_cmp_k2, w_cmp_v1, w_cmp_v2, out_norm_a, out_norm_b, w_out):
    params = (norm_in, w_in, b_gate, ln_v_g, ln_v_b, w_spatial, b_spatial, q_norm, k_norm_cmp,
              k_norm_sel, k_norm_win, cmp_pe_k, cmp_pe_v, w_cmp_k1, w_cmp_k2, w_cmp_v1, w_cmp_v2,
              out_norm_a, out_norm_b, w_out)
    depth = w_in.shape[0]
    h_p, h_s = x_prompt, x_sample
    st_p, st_s = [], []
    for l in range(depth):
        wts_p = _layer_weights(l, min(h_p.shape[1], CHUNK), *params)
        wts_s = _layer_weights(l, min(h_s.shape[1], CHUNK), *params)
        h_p, sp = _prompt_layer(h_p, wts_p)
        h_s, ss = _sample_layer(h_s, l, cache_k_cmp, cache_v_cmp, cache_k_sel, cache_v_sel,
                                state_k_win[l], state_v_win[l], page_table, wts_s)
        st_p.append(sp)
        st_s.append(ss)
    new_p = [jnp.stack(ts) for ts in zip(*st_p)]
    new_s = [jnp.stack(ts) for ts in zip(*st_s)]
    return (h_p, h_s, new_p[0], new_p[1], new_p[2], new_p[3], new_p[4], new_p[5],
            new_s[0], new_s[1], new_s[2], new_s[3], new_s[4], new_s[5], new_s[6])
```

```python
import functools

import numpy as np
import jax
import jax.numpy as jnp
from jax import lax
from jax.experimental import pallas as pl
from jax.experimental.pallas import tpu as pltpu

F32 = jnp.float32
BF16 = jnp.bfloat16

D_MODEL = 1024
D_A = 512
H_A = 8
CHUNK = 128
D_B = 512
H_B = 8
HD = 64
KV_B = 2
G_B = 4
KV_W = KV_B * HD
CMP_STRIDE = 16
CMP_BLK = 32
CMP_HID = 128
SEL_BLK = 64
N_SEL = 16
N_LOCAL = 2
WINDOW = 512
Q_BLOCK = 128
EPS = 1e-6
NEG = -1e30
SCALE = HD ** -0.5

LANES = 128
N_PACK = 3456
VMEM_LIMIT = 56 * 1024 * 1024


def _dot(a, b):
    return jnp.dot(a, b, preferred_element_type=F32)


def _dot_nt(a, b):
    return lax.dot_general(a, b, (((1,), (1,)), ((), ())), preferred_element_type=F32)


def _group_mean_sq(t, g_ref):
    t2 = t * t
    hi = t2.astype(BF16)
    lo = (t2 - hi.astype(F32)).astype(BF16)
    g = g_ref[...]
    return (_dot(hi, g) + _dot(lo, g)) * (1.0 / HD)


def _silu(z):
    return z * jax.nn.sigmoid(z)


def _key_feats(pos):
    lane = lax.broadcasted_iota(jnp.int32, (pos.shape[0], LANES), 1)
    sh = SEL_BLK.bit_length() - 1
    blk = (jnp.right_shift(pos, sh) * SEL_BLK).astype(F32)
    off = jnp.bitwise_and(pos, SEL_BLK - 1).astype(F32)
    return jnp.where(lane == HD, blk, jnp.where(lane == HD + 1, off,
                     jnp.where(jnp.logical_or(lane == HD + 2, lane == HD + 3), 1.0, 0.0)))


def _key_aug(k, pos):
    lane = lax.broadcasted_iota(jnp.int32, k.shape, 1)
    feats = _key_feats(pos)
    g0 = jnp.where(lane < HD, k, feats)
    g1 = jnp.where(lane < HD, pltpu.roll(k, HD, 1), feats)
    return jnp.concatenate([g0, g1], axis=1).astype(BF16)


def _proj_values(x_ref, nin_ref, w_ref, bg_ref, lng_ref, lnb_ref, wc_ref, bc_ref, qn_ref, ksn_ref,
                 kwn_ref, ona_ref, g512_ref, g128_ref):
    tn = x_ref.shape[0]
    x = x_ref[...]
    ms = jnp.mean(x * x, axis=-1, keepdims=True)
    xb = (x * lax.rsqrt(ms + EPS) * nin_ref[...]).astype(BF16)

    def proj(lo, n):
        return _dot(xb, w_ref[:, lo:lo + n])

    u = proj(0, D_A)
    v = proj(D_A, D_A)
    mu = jnp.mean(v, axis=-1, keepdims=True)
    vc = v - mu
    var = jnp.mean(vc * vc, axis=-1, keepdims=True)
    vn = vc * lax.rsqrt(var + EPS) * lng_ref[...] + lnb_ref[...]
    vnb = vn.astype(BF16)
    lane = lax.broadcasted_iota(jnp.int32, (CHUNK, LANES), 1)
    rows = []
    for c in range(tn // CHUNK):
        vcb = vnb[c * CHUNK:(c + 1) * CHUNK]
        cols = []
        for m in range(D_A // LANES):
            va = vcb[:, m * LANES:(m + 1) * LANES]
            sa = _dot(wc_ref[2 * m], va)
            sb = _dot(wc_ref[2 * m + 1], va)
            cols.append(jnp.where(lane < HD, sa, sb))
        rows.append(jnp.concatenate(cols, axis=1) + bc_ref[...])
    s = jnp.concatenate(rows, axis=0)
    o = u * s
    o = o * lax.rsqrt(jnp.mean(o * o, axis=-1, keepdims=True) + EPS) * ona_ref[...]
    za = proj(2 * D_A, D_A)
    oa = (o * _silu(za)).astype(BF16)

    q = proj(3 * D_A, D_B)
    q = q * lax.rsqrt(_group_mean_sq(q, g512_ref) + EPS) * qn_ref[...]
    szb = _silu(proj(3 * D_A + D_B, D_B)).astype(BF16)
    base = 3 * D_A + 2 * D_B
    kc = proj(base, KV_W)
    vc = proj(base + KV_W, KV_W)
    ks = proj(base + 2 * KV_W, KV_W)
    ks = ks * lax.rsqrt(_group_mean_sq(ks, g128_ref) + EPS) * ksn_ref[...]
    vs = proj(base + 3 * KV_W, KV_W)
    kw = proj(base + 4 * KV_W, KV_W)
    kw = kw * lax.rsqrt(_group_mean_sq(kw, g128_ref) + EPS) * kwn_ref[...]
    vw = proj(base + 5 * KV_W, KV_W)
    gates = jax.nn.sigmoid(proj(base + 6 * KV_W, LANES) + bg_ref[...])
    return oa, vn, q, szb, gates, kc, vc, ks, vs, kw, vw


N_PROJ_CONSTS = 13


def _proj_sample_kernel(*refs):
    ins, outs = refs[:1 + N_PROJ_CONSTS], refs[1 + N_PROJ_CONSTS:]
    for ref, val in zip(outs, _proj_values(*ins)):
        ref[...] = val


def _proj_prompt_kernel(*refs, seq):
    ins = refs[:1 + N_PROJ_CONSTS]
    (oa_ref, szb_ref, qt_ref, gt_ref, ksa_ref, kwa_ref, kcr_ref, vcr_ref,
     kct_ref, vct_ref, kst_ref, vst_ref, kwt_ref, vwt_ref) = refs[1 + N_PROJ_CONSTS:]
    oa, _, q, szb, gates, kc, vc, ks, vs, kw, vw = _proj_values(*ins)
    tn = oa.shape[0]
    t0 = lax.rem(pl.program_id(0) * tn, seq)
    oa_ref[...] = oa
    szb_ref[...] = szb
    kcr_ref[...] = kc
    vcr_ref[...] = vc
    for ref, val in ((kct_ref, kc), (vct_ref, vc), (kst_ref, ks), (vst_ref, vs), (kwt_ref, kw),
                     (vwt_ref, vw)):
        ref[0] = val.T
    gt_ref[0] = gates.T[:3 * H_B]
    pos_rows = t0 + lax.broadcasted_iota(jnp.int32, (tn, 1), 0)
    ksa_ref[0] = _key_aug(ks, pos_rows)
    kwa_ref[0] = _key_aug(kw, pos_rows)
    pos = t0 + lax.broadcasted_iota(jnp.int32, (HD, tn), 1)
    row = lax.broadcasted_iota(jnp.int32, (HD, tn), 0)
    sh = SEL_BLK.bit_length() - 1
    blk = (jnp.right_shift(pos, sh) * SEL_BLK).astype(F32)
    off = jnp.bitwise_and(pos, SEL_BLK - 1).astype(F32)
    qs = q * SCALE
    for m in range(D_B // LANES):
        qt = qs[:, m * LANES:(m + 1) * LANES].T
        for i in range(LANES // HD):
            h = (LANES // HD) * m + i
            slope = 2.0 ** -(h + 1)
            feats = jnp.where(row < 2, slope, jnp.where(row == 2, -slope * blk,
                              jnp.where(row == 3, -slope * off, 0.0)))
            qt_ref[0, h, :HD, :] = qt[i * HD:(i + 1) * HD].astype(BF16)
            qt_ref[0, h, HD:, :] = feats.astype(BF16)


def _row_tile(n):
    for tn in (512, 256, 128):
        if n % tn == 0:
            return tn
    raise ValueError(f"row count {n} must be a multiple of {CHUNK}")


def _proj_consts(wts):
    return [wts["nin"], wts["w_in"], wts["bg"], wts["lng"], wts["lnb"], wts["wc"], wts["bc"],
            wts["qn"], wts["ksn"], wts["kwn"], wts["ona"], wts["g512"], wts["g128"]]


def _projection_sample(x2d, wts):
    n = x2d.shape[0]
    tn = _row_tile(n)
    full = lambda a: pl.BlockSpec(a.shape, lambda i: (0,) * a.ndim)
    consts = _proj_consts(wts)
    row = lambda w: pl.BlockSpec((tn, w), lambda i: (i, 0))
    out_w = [(D_A, BF16), (D_A, F32), (D_B, F32), (D_B, BF16), (LANES, F32)] + [(KV_W, F32)] * 6
    return pl.pallas_call(
        _proj_sample_kernel,
        grid=(n // tn,),
        in_specs=[row(D_MODEL)] + [full(a) for a in consts],
        out_specs=[row(w) for w, _ in out_w],
        out_shape=[jax.ShapeDtypeStruct((n, w), dt) for w, dt in out_w],
        compiler_params=pltpu.CompilerParams(dimension_semantics=("arbitrary",),
                                             vmem_limit_bytes=VMEM_LIMIT),
        name="projection_sample",
    )(x2d, *consts)


def _projection_prompt(x2d, b, seq, wts):
    n = x2d.shape[0]
    tn = _row_tile(seq)
    tps = seq // tn
    full = lambda a: pl.BlockSpec(a.shape, lambda i: (0,) * a.ndim)
    consts = _proj_consts(wts)
    row = lambda w: pl.BlockSpec((tn, w), lambda i: (i, 0))
    minor = lambda r: pl.BlockSpec((1, r, tn), lambda i: (i // tps, 0, i % tps))
    specs = [(row(D_A), (n, D_A), BF16), (row(D_B), (n, D_B), BF16),
             (pl.BlockSpec((1, H_B, LANES, tn), lambda i: (i // tps, 0, 0, i % tps)),
              (b, H_B, LANES, seq), BF16),
             (minor(3 * H_B), (b, 3 * H_B, seq), F32)]
    specs += [(pl.BlockSpec((1, tn, 2 * KV_W), lambda i: (i // tps, i % tps, 0)),
               (b, seq, 2 * KV_W), BF16)] * 2
    specs += [(row(KV_W), (n, KV_W), F32)] * 2
    specs += [(minor(KV_W), (b, KV_W, seq), F32)] * 6
    return pl.pallas_call(
        functools.partial(_proj_prompt_kernel, seq=seq),
        grid=(n // tn,),
        in_specs=[row(D_MODEL)] + [full(a) for a in consts],
        out_specs=[s for s, _, _ in specs],
        out_shape=[jax.ShapeDtypeStruct(shape, dt) for _, shape, dt in specs],
        compiler_params=pltpu.CompilerParams(dimension_semantics=("arbitrary",),
                                             vmem_limit_bytes=VMEM_LIMIT),
        name="projection_prompt",
    )(x2d, *consts)


def _merge_kernel(x_ref, oa_ref, ob_ref, szb_ref, onb_ref, w_ref, y_ref):
    ob = ob_ref[...]
    ob = ob * lax.rsqrt(jnp.mean(ob * ob, axis=-1, keepdims=True) + EPS) * onb_ref[...]
    ob = (ob * szb_ref[...].astype(F32)).astype(BF16)
    y_ref[...] = x_ref[...] + _dot(oa_ref[...], w_ref[:D_A, :]) + _dot(ob, w_ref[D_A:, :])


def _merge(x2d, oa, ob, szb, wts):
    n = x2d.shape[0]
    tn = _row_tile(n)
    row = lambda w: pl.BlockSpec((tn, w), lambda i: (i, 0))
    full = lambda a: pl.BlockSpec(a.shape, lambda i: (0,) * a.ndim)
    return pl.pallas_call(
        _merge_kernel,
        grid=(n // tn,),
        in_specs=[row(D_MODEL), row(D_A), row(D_B), row(D_B), full(wts["onb"]), full(wts["w_out"])],
        out_specs=row(D_MODEL),
        out_shape=jax.ShapeDtypeStruct((n, D_MODEL), F32),
        compiler_params=pltpu.CompilerParams(dimension_semantics=("arbitrary",),
                                             vmem_limit_bytes=VMEM_LIMIT),
        name="merge",
    )(x2d, oa, ob, szb, wts["onb"], wts["w_out"])


def _compress_math(xb, w1_ref, pea_ref, peb_ref, w2_ref):
    h1 = _dot(xb, w1_ref[0])
    h2 = _dot(xb, w1_ref[1])
    r = h2.shape[0]
    h2n = pltpu.roll(h2, r - 1, 0)
    pe = _dot(pea_ref[...], w1_ref[0]) + _dot(peb_ref[...], w1_ref[1])
    pre = h1 + h2n + pe[0:1]
    return _dot(_silu(pre).astype(BF16), w2_ref[...])


def _compress_kernel(rk_ref, rv_ref, w1k_ref, w1v_ref, peak_ref, pebk_ref, peav_ref, pebv_ref,
                     w2k_ref, w2v_ref, kn_ref, g128_ref, kc_ref, vc_ref):
    kc = _compress_math(rk_ref[0].astype(BF16), w1k_ref, peak_ref, pebk_ref, w2k_ref)
    kc = kc * lax.rsqrt(_group_mean_sq(kc, g128_ref) + EPS) * kn_ref[...]
    end = CMP_STRIDE * lax.broadcasted_iota(jnp.int32, (kc.shape[0], 1), 0) + (CMP_BLK - 1)
    kc_ref[0] = _key_aug(kc, end)
    vc = _compress_math(rv_ref[0].astype(BF16), w1v_ref, peav_ref, pebv_ref, w2v_ref)
    vc_ref[0] = vc.T.astype(BF16)


def _cmp_consts(wts):
    return [wts["w1k"], wts["w1v"], wts["peak"], wts["pebk"], wts["peav"], wts["pebv"],
            wts["w2k"], wts["w2v"], wts["kcn"], wts["g128"]]


def _compress_prompt(rk, rv, wts):
    b, r, w = rk.shape
    consts = _cmp_consts(wts)
    full = lambda a: pl.BlockSpec(a.shape, lambda i: (0,) * a.ndim)
    blk = pl.BlockSpec((1, r, w), lambda i: (i, 0, 0))
    return pl.pallas_call(
        _compress_kernel,
        grid=(b,),
        in_specs=[blk, blk] + [full(a) for a in consts],
        out_specs=[pl.BlockSpec((1, r, 2 * KV_W), lambda i: (i, 0, 0)),
                   pl.BlockSpec((1, KV_W, r), lambda i: (i, 0, 0))],
        out_shape=[jax.ShapeDtypeStruct((b, r, 2 * KV_W), BF16),
                   jax.ShapeDtypeStruct((b, KV_W, r), BF16)],
        compiler_params=pltpu.CompilerParams(dimension_semantics=("arbitrary",),
                                             vmem_limit_bytes=VMEM_LIMIT),
        name="compress_prompt",
    )(rk, rv, *consts)


def _topk_rows(score, k):
    n = score.shape[0]
    ridx = lax.broadcasted_iota(jnp.int32, score.shape, 0).astype(F32)
    sel = jnp.zeros(score.shape, jnp.bool_)
    for _ in range(k):
        m = jnp.max(score, axis=0, keepdims=True)
        idx = jnp.min(jnp.where(score == m, ridx, float(n)), axis=0, keepdims=True)
        hit = ridx == idx
        sel = jnp.logical_or(sel, hit)
        score = jnp.where(hit, -jnp.inf, score)
    return sel


HEADS_AHEAD = 4


def _ahead(n, produce, consume):
    pending = {h: produce(h) for h in range(min(HEADS_AHEAD, n))}
    out = []
    for h in range(n):
        if h + HEADS_AHEAD < n:
            pending[h + HEADS_AHEAD] = produce(h + HEADS_AHEAD)
        out.append(consume(h, pending.pop(h)))
    return out


def _online_update(carry, s, vt):
    m, l, acc = carry
    m_new = jnp.maximum(m, jnp.max(s, axis=0, keepdims=True))
    alpha = jnp.exp(m - m_new)
    p = jnp.exp(s - m_new)
    l = alpha * l + jnp.sum(p, axis=0, keepdims=True)
    acc = alpha * acc + _dot(vt, p.astype(BF16))
    return m_new, l, acc


def _pattn_kernel(qt_ref, gt_ref, ks_ref, vst_ref, kw_ref, vwt_ref, kc_ref, vct_ref, cmask_ref,
                  dmask_ref, wmask_ref, o_ref, imp_sc, bias_sc):
    t = pl.program_id(1)
    n_sb = bias_sc.shape[1]
    pair = dmask_ref.shape[1]
    q0 = t * Q_BLOCK
    qpos = q0 + lax.broadcasted_iota(jnp.int32, (1, Q_BLOCK), 1)
    gls = [slice(g * LANES, (g + 1) * LANES) for g in range(KV_B)]
    gds = [slice(g * HD, (g + 1) * HD) for g in range(KV_B)]
    qts = [qt_ref[0, h] for h in range(H_B)]

    cmask = cmask_ref[0]
    kcs = [kc_ref[0, :, gls[g]] for g in range(KV_B)]
    vcts = [vct_ref[0, gds[g], :] for g in range(KV_B)]

    def cmp_softmax(h, sc):
        pc = jnp.exp(sc - jnp.maximum(jnp.max(sc, axis=0, keepdims=True), 0.1 * NEG))
        lc = jnp.sum(pc, axis=0, keepdims=True)
        pc = pc * (1.0 / jnp.where(lc > 0.0, lc, 1.0))
        return _dot(vcts[h // G_B], pc.astype(BF16)), pc

    cmp_out = _ahead(H_B, lambda h: _dot(kcs[h // G_B], qts[h]) + cmask, cmp_softmax)
    o_c = [o for o, _ in cmp_out]
    for g in range(KV_B):
        imp_sc[g] = sum(pc for _, pc in cmp_out[G_B * g:G_B * (g + 1)])
        ratio = SEL_BLK // CMP_STRIDE
        imp = sum(imp_sc[g, pl.ds(r, n_sb, stride=ratio), :] for r in range(ratio))
        bi = lax.broadcasted_iota(jnp.int32, (n_sb, Q_BLOCK), 0)
        cur = jnp.right_shift(qpos, SEL_BLK.bit_length() - 1)
        forced = jnp.logical_or(bi == 0, bi > cur - N_LOCAL)
        score = jnp.where(bi > cur, -1e9, jnp.where(forced, 1e9, imp))
        bias_sc[g] = jnp.where(_topk_rows(score, min(N_SEL, n_sb)), 0.0, NEG)

    span = wmask_ref.shape[1]
    w0 = pl.multiple_of(jnp.maximum(q0 - WINDOW, 0), Q_BLOCK)
    wmask = wmask_ref[0]
    kws = [kw_ref[0, pl.ds(w0, span), gls[g]] for g in range(KV_B)]
    vwts = [vwt_ref[0, gds[g], pl.ds(w0, span)].astype(BF16) for g in range(KV_B)]

    def win_softmax(h, s):
        p = jnp.exp(s - jnp.max(s, axis=0, keepdims=True))
        l = jnp.sum(p, axis=0, keepdims=True)
        return _dot(vwts[h // G_B], p.astype(BF16)) * (1.0 / l)

    qtg = [jnp.concatenate(qts[G_B * g:G_B * (g + 1)], axis=1) for g in range(KV_B)]
    head = lambda wide, h: wide[h // G_B][:, (h % G_B) * Q_BLOCK:(h % G_B + 1) * Q_BLOCK]
    s_w = [_dot(kws[g], qtg[g]) for g in range(KV_B)]
    o_w = [win_softmax(h, head(s_w, h) + wmask) for h in range(H_B)]

    def sel_step(p, carry, diag):
        k0 = pl.multiple_of(p * pair, pair)
        bias, kt, vt = [], [], []
        for g in range(KV_B):
            rows = [jnp.broadcast_to(bias_sc[g, pl.ds((pair // SEL_BLK) * p + i, 1), :],
                                     (SEL_BLK, Q_BLOCK)) for i in range(pair // SEL_BLK)]
            bias.append(jnp.concatenate(rows, axis=0) + dmask_ref[0] if diag
                        else jnp.concatenate(rows, axis=0))
            kt.append(ks_ref[0, pl.ds(k0, pair), gls[g]])
            vt.append(vst_ref[0, gds[g], pl.ds(k0, pair)].astype(BF16))
        wide = [_dot(kt[g], qtg[g]) for g in range(KV_B)]
        return tuple(_online_update(carry[h], head(wide, h) + bias[h // G_B], vt[h // G_B])
                     for h in range(H_B))

    init = (jnp.full((1, Q_BLOCK), NEG, F32), jnp.zeros((1, Q_BLOCK), F32), jnp.zeros((HD, Q_BLOCK), F32))
    last = t // 2
    carry = lax.fori_loop(0, last, lambda p, c: sel_step(p, c, False), (init,) * H_B)
    o_s = [a * (1.0 / l) for _, l, a in sel_step(last, carry, True)]

    for m in range(H_B // 2):
        halves = [gt_ref[0, h:h + 1, :] * o_c[h] + gt_ref[0, H_B + h:H_B + h + 1, :] * o_s[h]
                  + gt_ref[0, 2 * H_B + h:2 * H_B + h + 1, :] * o_w[h] for h in (2 * m, 2 * m + 1)]
        o_ref[:, m * LANES:(m + 1) * LANES] = jnp.concatenate(halves, axis=0).T


def _prompt_attention(qt, gt, ksa, vst, kwa, vwt, kca, vct):
    b, _, _, t = qt.shape
    nb = kca.shape[1]
    n_sb = t // SEL_BLK
    nt = t // Q_BLOCK
    pair, span, n_wt = 2 * Q_BLOCK, WINDOW + Q_BLOCK, WINDOW // Q_BLOCK
    assert t % pair == 0 and t >= span
    q = np.arange(Q_BLOCK)[None, None, :]
    bias = lambda ok: jnp.asarray(np.where(ok, 0.0, NEG), F32)
    ci = np.arange(nb)[None, :, None]
    tq = Q_BLOCK * np.arange(nt)[:, None, None] + q
    cmask = bias((tq >= CMP_STRIDE * ci + CMP_BLK - 1) & (ci < nb - 1))
    r = np.arange(pair)[None, :, None]
    dmask = bias(r <= Q_BLOCK * np.arange(2)[:, None, None] + q)
    r = np.arange(span)[None, :, None]
    dist = Q_BLOCK * np.arange(n_wt + 1)[:, None, None] + q - r
    wmask = bias((dist >= 0) & (dist < WINDOW))
    per_b = lambda a: pl.BlockSpec((1,) + a.shape[1:], lambda i, j: (i,) + (0,) * (a.ndim - 1))
    return pl.pallas_call(
        _pattn_kernel,
        grid=(b, nt),
        in_specs=[pl.BlockSpec((1, H_B, LANES, Q_BLOCK), lambda i, j: (i, 0, 0, j)),
                  pl.BlockSpec((1, 3 * H_B, Q_BLOCK), lambda i, j: (i, 0, j)),
                  per_b(ksa), per_b(vst), per_b(kwa), per_b(vwt), per_b(kca), per_b(vct),
                  pl.BlockSpec((1, nb, Q_BLOCK), lambda i, j: (j, 0, 0)),
                  pl.BlockSpec((1, pair, Q_BLOCK), lambda i, j: (lax.rem(j, 2), 0, 0)),
                  pl.BlockSpec((1, span, Q_BLOCK), lambda i, j: (jnp.minimum(j, n_wt), 0, 0))],
        out_specs=pl.BlockSpec((Q_BLOCK, D_B), lambda i, j: (i * (t // Q_BLOCK) + j, 0)),
        out_shape=jax.ShapeDtypeStruct((b * t, D_B), F32),
        scratch_shapes=[pltpu.VMEM((KV_B, nb, Q_BLOCK), F32), pltpu.VMEM((KV_B, n_sb, Q_BLOCK), F32)],
        compiler_params=pltpu.CompilerParams(dimension_semantics=("arbitrary", "arbitrary"),
                                             vmem_limit_bytes=VMEM_LIMIT),
        name="prompt_attention",
    )(qt, gt, ksa, vst, kwa, vwt, kca, vct, cmask, dmask, wmask)


def _page_copy(hbm, buf, sem, page, slot, which, p):
    return pltpu.make_async_copy(hbm.at[page], buf.at[slot, which, p], sem.at[slot, which])


def _gather_pages(pt_ref, hbm_a, hbm_b, buf, sem):
    b = pl.program_id(0)
    nb = pl.num_programs(0)
    n_pages = pt_ref.shape[1]
    slot = lax.rem(b, 2)

    def issue(bb, sl):
        for p in range(n_pages):
            page = pt_ref[bb, p]
            _page_copy(hbm_a, buf, sem, page, sl, 0, p).start()
            _page_copy(hbm_b, buf, sem, page, sl, 1, p).start()

    @pl.when(b == 0)
    def _():
        issue(0, 0)

    @pl.when(b + 1 < nb)
    def _():
        issue(b + 1, 1 - slot)

    for p in range(n_pages):
        _page_copy(hbm_a, buf, sem, 0, slot, 0, p).wait()
        _page_copy(hbm_b, buf, sem, 0, slot, 1, p).wait()
    return slot


SUBLANES = 8
RELAYOUT_UNROLL = True


def _sublane_transpose(tiles):
    tiles = list(tiles)
    sub = lax.broadcasted_iota(jnp.int32, tiles[0].shape, 0)
    k = 1
    while k < SUBLANES:
        hi = jnp.bitwise_and(sub, k) != 0
        for c in range(SUBLANES):
            if c & k:
                continue
            a, b = tiles[c], tiles[c | k]
            tiles[c] = jnp.where(hi, pltpu.roll(b, k, 0), a)
            tiles[c | k] = jnp.where(hi, b, pltpu.roll(a, SUBLANES - k, 0))
        k *= 2
    return tiles


def _chunk_rows(buf, slot, which, lhs_sc):
    n_pages, _, page = buf.shape[2:]
    cpp = page // CMP_STRIDE
    assert cpp == SUBLANES and CMP_STRIDE % SUBLANES == 0

    def pair(pp, carry):
        per_j = []
        for i in range(2):
            x = buf[slot, which, 2 * pp + i].T
            cols = [None] * CMP_STRIDE
            for h in range(CMP_STRIDE // SUBLANES):
                tiles = [x[CMP_STRIDE * c + SUBLANES * h:CMP_STRIDE * c + SUBLANES * (h + 1)]
                         for c in range(cpp)]
                cols[SUBLANES * h:SUBLANES * (h + 1)] = _sublane_transpose(tiles)
            per_j.append(cols)
        r0 = pl.multiple_of(2 * cpp * pp, 2 * cpp)
        for j in range(CMP_STRIDE):
            lhs_sc[which, pl.ds(r0, 2 * cpp), KV_W * j:KV_W * (j + 1)] = (
                jnp.concatenate([per_j[0][j], per_j[1][j]], axis=0).astype(BF16))
        return carry

    lax.fori_loop(0, n_pages // 2, pair, 0, unroll=RELAYOUT_UNROLL)


def _scmp_kernel(pt_ref, kc_hbm, vc_hbm, qk_ref, slope_ref, irow_ref, w1k_ref, w1v_ref, peak_ref,
                 pebk_ref, peav_ref, pebv_ref, w2k_ref, w2v_ref, kn_ref, g128_ref, gsum_ref,
                 oc_ref, imp_ref, buf, sem, lhs_sc, *, past_len):
    slot = _gather_pages(pt_ref, kc_hbm, vc_hbm, buf, sem)
    _chunk_rows(buf, slot, 0, lhs_sc)
    _chunk_rows(buf, slot, 1, lhs_sc)
    kc = _compress_math(lhs_sc[0], w1k_ref, peak_ref, pebk_ref, w2k_ref)
    kc = kc * lax.rsqrt(_group_mean_sq(kc, g128_ref) + EPS) * kn_ref[...]
    vc = _compress_math(lhs_sc[1], w1v_ref, peav_ref, pebv_ref, w2v_ref)
    nb = kc.shape[0]
    nr = qk_ref.shape[1]

    s = _dot_nt(qk_ref[0], kc.astype(BF16))
    ci = lax.broadcasted_iota(jnp.int32, (nr, nb), 1)
    end = (CMP_STRIDE * ci + (CMP_BLK - 1)).astype(F32)
    dist = (past_len + irow_ref[:, 0:1]) - end
    valid = jnp.logical_and(dist >= 0.0, ci < nb - 1)
    s = jnp.where(valid, s - slope_ref[:, 0:1] * dist, NEG)
    p = jnp.where(valid, jnp.exp(s - jnp.max(s, axis=-1, keepdims=True)), 0.0)
    l = jnp.sum(p, axis=-1, keepdims=True)
    p = p * (1.0 / jnp.where(l > 0.0, l, 1.0))
    oc_ref[0] = _dot(p.astype(BF16), vc.astype(BF16))

    tq = nr // H_B
    parts = []
    for g in range(KV_B):
        base = g * G_B * tq
        parts.append(sum(p[base + hh * tq:base + (hh + 1) * tq] for hh in range(G_B)))
    imp = jnp.concatenate(parts, axis=0)
    hi = imp.astype(BF16)
    r1 = imp - hi.astype(F32)
    mid = r1.astype(BF16)
    lo = (r1 - mid.astype(F32)).astype(BF16)
    gs = gsum_ref[...]
    imp_ref[0] = _dot(hi, gs) + _dot(mid, gs) + _dot(lo, gs)


def _sample_compressed(page_table, kc_pages, vc_pages, qk, slope, irow, wts, past_len):
    b, n_pages = page_table.shape
    page = kc_pages.shape[2]
    cpp = page // CMP_STRIDE
    nb = n_pages * cpp
    nr = qk.shape[1]
    ratio = SEL_BLK // CMP_STRIDE
    n_blk = -(-(nb // ratio + 1) // LANES) * LANES
    gsum = jnp.asarray(np.arange(nb)[:, None] // ratio == np.arange(n_blk)[None, :], BF16)
    consts = [slope, irow] + _cmp_consts(wts) + [gsum]
    full = lambda a: pl.BlockSpec(a.shape, lambda i, pt: (0,) * a.ndim)
    grid_spec = pltpu.PrefetchScalarGridSpec(
        num_scalar_prefetch=1,
        grid=(b,),
        in_specs=[pl.BlockSpec(memory_space=pl.ANY), pl.BlockSpec(memory_space=pl.ANY),
                  pl.BlockSpec((1, nr, LANES), lambda i, pt: (i, 0, 0))] + [full(a) for a in consts],
        out_specs=[pl.BlockSpec((1, nr, KV_W), lambda i, pt: (i, 0, 0)),
                   pl.BlockSpec((1, nr // G_B, n_blk), lambda i, pt: (i, 0, 0))],
        scratch_shapes=[pltpu.VMEM((2, 2, n_pages, KV_W, page), F32),
                        pltpu.SemaphoreType.DMA((2, 2)),
                        pltpu.VMEM((2, nb, CMP_STRIDE * KV_W), BF16)],
    )
    return pl.pallas_call(
        functools.partial(_scmp_kernel, past_len=float(past_len)),
        grid_spec=grid_spec,
        out_shape=[jax.ShapeDtypeStruct((b, nr, KV_W), F32),
                   jax.ShapeDtypeStruct((b, nr // G_B, n_blk), F32)],
        compiler_params=pltpu.CompilerParams(dimension_semantics=("arbitrary",),
                                             vmem_limit_bytes=VMEM_LIMIT),
        name="sample_compressed",
    )(page_table, kc_pages, vc_pages, qk, *consts)


def _topk_lanes_kernel(imp_ref, out_ref, *, n_blocks, cur, k):
    imp = imp_ref[...]
    bi = lax.broadcasted_iota(jnp.int32, imp.shape, 1).astype(F32)
    forced = jnp.logical_or(bi == 0, bi > cur - N_LOCAL)
    score = jnp.where(bi > cur, -1e9, jnp.where(forced, 1e9, imp))
    score = jnp.where(bi < n_blocks, score, -jnp.inf)
    width = imp.shape[1]
    sel = jnp.zeros(imp.shape, jnp.bool_)
    for _ in range(k):
        m = jnp.max(score, axis=-1, keepdims=True)
        idx = jnp.min(jnp.where(score == m, bi, float(width)), axis=-1, keepdims=True)
        hit = bi == idx
        sel = jnp.logical_or(sel, hit)
        score = jnp.where(hit, -jnp.inf, score)
    out_ref[...] = jnp.where(sel, 0.0, NEG)


def _sample_topk(imp2d, n_blocks, cur):
    n, w = imp2d.shape
    tr = 256 if n % 256 == 0 else n
    blk = pl.BlockSpec((tr, w), lambda i: (i, 0))
    return pl.pallas_call(
        functools.partial(_topk_lanes_kernel, n_blocks=n_blocks, cur=cur, k=min(N_SEL, n_blocks)),
        grid=(n // tr,),
        in_specs=[blk],
        out_specs=blk,
        out_shape=jax.ShapeDtypeStruct((n, w), F32),
        compiler_params=pltpu.CompilerParams(dimension_semantics=("arbitrary",)),
        name="sample_topk",
    )(imp2d)


def _ssel_kernel(pt_ref, ks_hbm, vs_hbm, qsel_ref, knewt_ref, vnew_ref, kwst_ref, vwst_ref, kwnewt_ref,
                 vwnew_ref, oc_ref, gsel_ref, slope_ref, irow_ref, o_ref,
                 buf, sem, s_sc, oh_sc, *, past_len):
    b = pl.program_id(0)
    n_pages, _, page = buf.shape[2:]
    nr = qsel_ref.shape[1]
    ppt = 2 if n_pages % 2 == 0 else 1
    tile = ppt * page

    @pl.when(b == 0)
    def _():
        bi = lax.broadcasted_iota(jnp.int32, oh_sc.shape, 0)
        ki = lax.broadcasted_iota(jnp.int32, oh_sc.shape, 1)
        oh_sc[...] = jnp.where(jnp.right_shift(ki, SEL_BLK.bit_length() - 1) == bi, 1.0, 0.0).astype(BF16)

    slot = _gather_pages(pt_ref, ks_hbm, vs_hbm, buf, sem)
    slope = slope_ref[:, 0:1]
    irow = irow_ref[:, 0:1]
    qpos = past_len + irow
    qall = qsel_ref[0]
    qk = qall[:, :KV_W]

    def key_tile(which, j):
        return jnp.concatenate([buf[slot, which, ppt * j + i] for i in range(ppt)], axis=1).astype(BF16)

    lane = lax.broadcasted_iota(jnp.int32, (nr, tile), 1).astype(F32)
    mrun = jnp.full((nr, tile), NEG, F32)
    for j in range(n_pages // ppt):
        kcat = jnp.concatenate([key_tile(0, j), oh_sc[:, j * tile:(j + 1) * tile]], axis=0)
        s = _dot(qall, kcat) - slope * (qpos - (float(j * tile) + lane))
        s_sc[:, j * tile:(j + 1) * tile] = s
        mrun = jnp.maximum(mrun, s)
    nl = lax.broadcasted_iota(jnp.int32, (nr, LANES), 1).astype(F32)
    dn = irow - nl
    s_new = jnp.where(dn >= 0.0, _dot(qk, knewt_ref[0]) - slope * dn, NEG)
    m = jnp.maximum(jnp.max(mrun, axis=-1, keepdims=True), jnp.max(s_new, axis=-1, keepdims=True))

    p_new = jnp.exp(s_new - m)
    acc = _dot(p_new.astype(BF16), vnew_ref[0])
    lrun = jnp.zeros((nr, tile), F32)
    for j in range(n_pages // ppt):
        p = jnp.exp(s_sc[:, j * tile:(j + 1) * tile] - m)
        acc = acc + _dot_nt(p.astype(BF16), key_tile(1, j))
        lrun = lrun + p
    l = jnp.sum(lrun, axis=-1, keepdims=True) + jnp.sum(p_new, axis=-1, keepdims=True)
    o_s = acc * (1.0 / l)

    n_buf = kwst_ref.shape[2]
    wl = lax.broadcasted_iota(jnp.int32, (nr, n_buf), 1).astype(F32)
    dw = (n_buf + irow) - wl
    s_w = jnp.where(jnp.logical_and(dw >= 0.0, dw < WINDOW),
                    _dot(qk, kwst_ref[0].astype(BF16)) - slope * dw, NEG)
    s_wn = jnp.where(dn >= 0.0, _dot(qk, kwnewt_ref[0]) - slope * dn, NEG)
    m_w = jnp.maximum(jnp.max(s_w, axis=-1, keepdims=True), jnp.max(s_wn, axis=-1, keepdims=True))
    p_w = jnp.exp(s_w - m_w)
    p_wn = jnp.exp(s_wn - m_w)
    l_w = jnp.sum(p_w, axis=-1, keepdims=True) + jnp.sum(p_wn, axis=-1, keepdims=True)
    o_w = (_dot_nt(p_w.astype(BF16), vwst_ref[0].astype(BF16))
           + _dot(p_wn.astype(BF16), vwnew_ref[0])) * (1.0 / l_w)

    gsel = gsel_ref[0]
    o_ref[0] = gsel[:, 0:1] * oc_ref[0] + gsel[:, 1:2] * o_s + gsel[:, 2:3] * o_w


def _sample_selected(page_table, ks_pages, vs_pages, qsel, knewt, vnew, kwst, vwst, kwnewt, vwnew, oc,
                     gsel, slope, irow, past_len):
    b, n_pages = page_table.shape
    page = ks_pages.shape[2]
    n_keys = n_pages * page
    nr = qsel.shape[1]
    per_b = lambda a: pl.BlockSpec((1,) + a.shape[1:], lambda i, pt: (i,) + (0,) * (a.ndim - 1))
    full = lambda a: pl.BlockSpec(a.shape, lambda i, pt: (0,) * a.ndim)
    grid_spec = pltpu.PrefetchScalarGridSpec(
        num_scalar_prefetch=1,
        grid=(b,),
        in_specs=[pl.BlockSpec(memory_space=pl.ANY), pl.BlockSpec(memory_space=pl.ANY)]
                 + [per_b(a) for a in (qsel, knewt, vnew, kwst, vwst, kwnewt, vwnew, oc, gsel)]
                 + [full(slope), full(irow)],
        out_specs=pl.BlockSpec((1, nr, KV_W), lambda i, pt: (i, 0, 0)),
        scratch_shapes=[pltpu.VMEM((2, 2, n_pages, KV_W, page), F32),
                        pltpu.SemaphoreType.DMA((2, 2)),
                        pltpu.VMEM((nr, n_keys), F32),
                        pltpu.VMEM((n_keys // SEL_BLK, n_keys), BF16)],
    )
    return pl.pallas_call(
        functools.partial(_ssel_kernel, past_len=float(past_len)),
        grid_spec=grid_spec,
        out_shape=jax.ShapeDtypeStruct((b, nr, KV_W), F32),
        compiler_params=pltpu.CompilerParams(dimension_semantics=("arbitrary",),
                                             vmem_limit_bytes=VMEM_LIMIT),
        name="sample_selected",
    )(page_table, ks_pages, vs_pages, qsel, knewt, vnew, kwst, vwst, kwnewt, vwnew, oc, gsel, slope, irow)


def _slopes():
    return np.asarray([2.0 ** -(h + 1) for h in range(H_B)], np.float32)


def _block_ones(n):
    idx = np.arange(n) // HD
    return jnp.asarray(idx[:, None] == idx[None, :], BF16)


def _layer_weights(l, t_chunk, norm_in, w_in, b_gate, ln_v_g, ln_v_b, w_spatial, b_spatial, q_norm,
                   k_norm_cmp, k_norm_sel, k_norm_win, cmp_pe_k, cmp_pe_v, w_cmp_k1, w_cmp_k2,
                   w_cmp_v1, w_cmp_v2, out_norm_a, out_norm_b, w_out):
    w = w_in[l]
    n_main = 3 * D_A + 2 * D_B
    n_gate = 3 * H_B
    w_pack = jnp.concatenate([w[:, :n_main], w[:, n_main + n_gate:], w[:, n_main:n_main + n_gate],
                              jnp.zeros((D_MODEL, N_PACK - w.shape[1]), F32)], axis=1).astype(BF16)
    c = t_chunk
    rep = CHUNK // c
    tri = jnp.where(jnp.tril(jnp.ones((c, c), bool)), w_spatial[l][:, :c, :c], 0.0)
    wc = jnp.einsum("ab,hij->haibj", jnp.eye(rep, dtype=F32), tri).reshape(H_A, CHUNK, CHUNK).astype(BF16)
    bc = jnp.tile(jnp.repeat(b_spatial[l][:, :c].T, D_A // H_A, axis=1), (rep, 1))

    def big1(w1):
        w1r = w1.reshape(2, CMP_STRIDE, HD, CMP_HID)
        return jnp.einsum("ajdh,ck->ajcdkh", w1r, jnp.eye(KV_B, dtype=F32)).reshape(
            2, CMP_STRIDE * KV_W, KV_B * CMP_HID).astype(BF16)

    def big2(w2):
        return jnp.einsum("hd,kc->khcd", w2, jnp.eye(KV_B, dtype=F32)).reshape(
            KV_B * CMP_HID, KV_W).astype(BF16)

    def pe_rows(pe):
        halves = pe.reshape(2, CMP_STRIDE, 1, HD)
        flat = jnp.broadcast_to(halves, (2, CMP_STRIDE, KV_B, HD)).reshape(2, 1, CMP_STRIDE * KV_W)
        flat = jnp.broadcast_to(flat, (2, 8, CMP_STRIDE * KV_W)).astype(BF16)
        return flat[0], flat[1]

    peak, pebk = pe_rows(cmp_pe_k[l])
    peav, pebv = pe_rows(cmp_pe_v[l])
    row = lambda a, reps: jnp.tile(a, reps)[None, :]
    bg = jnp.concatenate([b_gate[l], jnp.zeros((LANES - n_gate,), F32)])[None, :]
    return dict(
        nin=norm_in[l][None, :], w_in=w_pack, bg=bg, lng=ln_v_g[l][None, :], lnb=ln_v_b[l][None, :],
        wc=wc, bc=bc, qn=row(q_norm[l], H_B), ksn=row(k_norm_sel[l], KV_B), kwn=row(k_norm_win[l], KV_B),
        kcn=row(k_norm_cmp[l], KV_B), ona=out_norm_a[l][None, :], onb=out_norm_b[l][None, :],
        g512=_block_ones(D_B), g128=_block_ones(KV_W),
        w1k=big1(w_cmp_k1[l]), w1v=big1(w_cmp_v1[l]), w2k=big2(w_cmp_k2[l]), w2v=big2(w_cmp_v2[l]),
        peak=peak, pebk=pebk, peav=peav, pebv=pebv, w_out=w_out[l].astype(BF16))


def _prompt_layer(x, wts):
    b, t, _ = x.shape
    n = b * t
    x2d = x.reshape(n, D_MODEL)
    (oa, szb, qt, gt, ksa, kwa, kc_r, vc_r,
     kct, vct, kst, vst, kwt, vwt) = _projection_prompt(x2d, b, t, wts)
    nc = t // CMP_STRIDE
    kca, vcct = _compress_prompt(kc_r.reshape(b, nc, CMP_STRIDE * KV_W),
                                 vc_r.reshape(b, nc, CMP_STRIDE * KV_W), wts)
    ob = _prompt_attention(qt, gt, ksa, vst, kwa, vwt, kca, vcct)
    y = _merge(x2d, oa, ob, szb, wts).reshape(b, t, D_MODEL)
    heads = lambda a: jnp.transpose(a.reshape(b, KV_B, HD, a.shape[2]), (0, 3, 1, 2))
    n_keep = min(WINDOW, t)
    return y, (heads(kct), heads(vct), heads(kst), heads(vst),
               heads(kwt[:, :, t - n_keep:]), heads(vwt[:, :, t - n_keep:]))


def _sample_layer(x, l, cache_k_cmp, cache_v_cmp, cache_k_sel, cache_v_sel, k_win_buf, v_win_buf,
                  page_table, wts):
    b, t, _ = x.shape
    n = b * t
    n_pool, page = cache_k_cmp.shape[1], cache_k_cmp.shape[2]
    n_pages = page_table.shape[1]
    past_len = n_pages * page
    x2d = x.reshape(n, D_MODEL)
    oa, vn, q, szb, gates, kc_r, vc_r, ks, vs, kw, vw = _projection_sample(x2d, wts)

    nr = H_B * t
    sl = _slopes()
    slope = jnp.asarray(np.broadcast_to(np.repeat(sl, t)[:, None], (nr, LANES)).copy())
    irow = jnp.asarray(np.broadcast_to(np.tile(np.arange(t, dtype=np.float32), H_B)[:, None], (nr, LANES)).copy())
    qh = jnp.transpose((q * SCALE).reshape(b, t, KV_B, G_B, HD), (0, 2, 3, 1, 4))
    qk = jnp.einsum("bghtd,gk->bghtkd", qh, jnp.eye(KV_B, dtype=F32)).reshape(b, nr, KV_W).astype(BF16)

    minor_pos = lambda a: jnp.transpose(a, (0, 2, 3, 1)).reshape(a.shape[0], KV_W, a.shape[1])
    oc, imp = _sample_compressed(page_table, minor_pos(cache_k_cmp[l]), minor_pos(cache_v_cmp[l]),
                                 qk, slope, irow, wts, past_len)

    n_sb = -(-(past_len + t) // SEL_BLK)
    n_pb = past_len // SEL_BLK
    cur = past_len // SEL_BLK
    wpad = imp.shape[2]
    bias = _sample_topk(imp.reshape(b * KV_B * t, wpad), n_sb, cur).reshape(b, KV_B, 1, t, wpad)
    bias = jnp.broadcast_to(bias[..., :n_pb], (b, KV_B, G_B, t, n_pb)).reshape(b, nr, n_pb)
    qsel = jnp.concatenate([qk, bias.astype(BF16)], axis=-1)

    pad_rows = lambda a: jnp.pad(a.reshape(b, t, KV_W).astype(BF16), ((0, 0), (0, LANES - t), (0, 0)))
    pad_cols = lambda a: jnp.transpose(pad_rows(a), (0, 2, 1))
    gsel = jnp.transpose(gates[:, :3 * H_B].reshape(b, t, 3, H_B), (0, 3, 1, 2)).reshape(b, nr, 3)
    gsel = jnp.pad(gsel, ((0, 0), (0, 0), (0, LANES - 3)))
    o = _sample_selected(page_table, minor_pos(cache_k_sel[l]), minor_pos(cache_v_sel[l]), qsel,
                         pad_cols(ks), pad_rows(vs), minor_pos(k_win_buf), minor_pos(v_win_buf),
                         pad_cols(kw), pad_rows(vw), oc, gsel, slope, irow, past_len)
    o5 = o.reshape(b, KV_B, G_B, t, KV_B, HD)
    ob = jnp.stack([o5[:, g, :, :, g, :] for g in range(KV_B)], axis=1)
    ob = jnp.transpose(ob, (0, 3, 1, 2, 4)).reshape(n, D_B)
    y = _merge(x2d, oa, ob, szb, wts).reshape(b, t, D_MODEL)
    heads = lambda a: a.reshape(b, t, KV_B, HD)
    kw_all = jnp.concatenate([k_win_buf, heads(kw)], axis=1)
    vw_all = jnp.concatenate([v_win_buf, heads(vw)], axis=1)
    return y, (heads(kc_r), heads(vc_r), heads(ks), heads(vs), kw_all[:, t:], vw_all[:, t:],
               vn.reshape(b, t, D_A))


def kernel(x_prompt, x_sample, cache_k_cmp, cache_v_cmp, cache_k_sel, cache_v_sel, state_k_win, state_v_win, page_table, norm_in, w_in, b_gate, ln_v_g, ln_v_b, w_spatial, b_spatial, q_norm, k_norm_cmp, k_norm_sel, k_norm_win, cmp_pe_k, cmp_pe_v, w_cmp_k1, w_cmp_k2, w_cmp_v1, w_cmp_v2, out_norm_a, out_norm_b, w_out):
    params = (norm_in, w_in, b_gate, ln_v_g, ln_v_b, w_spatial, b_spatial, q_norm, k_norm_cmp,
              k_norm_sel, k_norm_win, cmp_pe_k, cmp_pe_v, w_cmp_k1, w_cmp_k2, w_cmp_v1, w_cmp_v2,
              out_norm_a, out_norm_b, w_out)
    depth = w_in.shape[0]
    h_p, h_s = x_prompt, x_sample
    st_p, st_s = [], []
    for l in range(depth):
        wts_p = _layer_weights(l, min(h_p.shape[1], CHUNK), *params)
        wts_s = _layer_weights(l, min(h_s.shape[1], CHUNK), *params)
        h_p, sp = _prompt_layer(h_p, wts_p)
        h_s, ss = _sample_layer(h_s, l, cache_k_cmp, cache_v_cmp, cache_k_sel, cache_v_sel,
                                state_k_win[l], state_v_win[l], page_table, wts_s)
        st_p.append(sp)
        st_s.append(ss)
    new_p = [jnp.stack(ts) for ts in zip(*st_p)]
    new_s = [jnp.stack(ts) for ts in zip(*st_s)]
    return (h_p, h_s, new_p[0], new_p[1], new_p[2], new_p[3], new_p[4], new_p[5],
            new_s[0], new_s[1], new_s[2], new_s[3], new_s[4], new_s[5], new_s[6])
```

```python
import functools

import numpy as np
import jax
import jax.numpy as jnp
from jax import lax
from jax.experimental import pallas as pl
from jax.experimental.pallas import tpu as pltpu

F32 = jnp.float32
BF16 = jnp.bfloat16

D_MODEL = 1024
D_A = 512
H_A = 8
CHUNK = 128
D_B = 512
H_B = 8
HD = 64
KV_B = 2
G_B = 4
KV_W = KV_B * HD
CMP_STRIDE = 16
CMP_BLK = 32
CMP_HID = 128
SEL_BLK = 64
N_SEL = 16
N_LOCAL = 2
WINDOW = 512
Q_BLOCK = 128
EPS = 1e-6
NEG = -1e30
SCALE = HD ** -0.5

LANES = 128
N_PACK = 3456
VMEM_LIMIT = 56 * 1024 * 1024


def _dot(a, b):
    return jnp.dot(a, b, preferred_element_type=F32)


def _dot_nt(a, b):
    return lax.dot_general(a, b, (((1,), (1,)), ((), ())), preferred_element_type=F32)


def _group_mean_sq(t, g_ref):
    t2 = t * t
    hi = t2.astype(BF16)
    lo = (t2 - hi.astype(F32)).astype(BF16)
    g = g_ref[...]
    return (_dot(hi, g) + _dot(lo, g)) * (1.0 / HD)


def _silu(z):
    return z * jax.nn.sigmoid(z)


def _key_feats(pos):
    lane = lax.broadcasted_iota(jnp.int32, (pos.shape[0], LANES), 1)
    sh = SEL_BLK.bit_length() - 1
    blk = (jnp.right_shift(pos, sh) * SEL_BLK).astype(F32)
    off = jnp.bitwise_and(pos, SEL_BLK - 1).astype(F32)
    return jnp.where(lane == HD, blk, jnp.where(lane == HD + 1, off,
                     jnp.where(jnp.logical_or(lane == HD + 2, lane == HD + 3), 1.0, 0.0)))


def _key_aug(k, pos):
    lane = lax.broadcasted_iota(jnp.int32, k.shape, 1)
    feats = _key_feats(pos)
    g0 = jnp.where(lane < HD, k, feats)
    g1 = jnp.where(lane < HD, pltpu.roll(k, HD, 1), feats)
    return jnp.concatenate([g0, g1], axis=1).astype(BF16)


def _proj_values(x_ref, nin_ref, w_ref, bg_ref, lng_ref, lnb_ref, wc_ref, bc_ref, qn_ref, ksn_ref,
                 kwn_ref, ona_ref, g512_ref, g128_ref):
    tn = x_ref.shape[0]
    x = x_ref[...]
    ms = jnp.mean(x * x, axis=-1, keepdims=True)
    xb = (x * lax.rsqrt(ms + EPS) * nin_ref[...]).astype(BF16)

    def proj(lo, n):
        return _dot(xb, w_ref[:, lo:lo + n])

    u = proj(0, D_A)
    v = proj(D_A, D_A)
    mu = jnp.mean(v, axis=-1, keepdims=True)
    vc = v - mu
    var = jnp.mean(vc * vc, axis=-1, keepdims=True)
    vn = vc * lax.rsqrt(var + EPS) * lng_ref[...] + lnb_ref[...]
    vnb = vn.astype(BF16)
    lane = lax.broadcasted_iota(jnp.int32, (CHUNK, LANES), 1)
    rows = []
    for c in range(tn // CHUNK):
        vcb = vnb[c * CHUNK:(c + 1) * CHUNK]
        cols = []
        for m in range(D_A // LANES):
            va = vcb[:, m * LANES:(m + 1) * LANES]
            sa = _dot(wc_ref[2 * m], va)
            sb = _dot(wc_ref[2 * m + 1], va)
            cols.append(jnp.where(lane < HD, sa, sb))
        rows.append(jnp.concatenate(cols, axis=1) + bc_ref[...])
    s = jnp.concatenate(rows, axis=0)
    o = u * s
    o = o * lax.rsqrt(jnp.mean(o * o, axis=-1, keepdims=True) + EPS) * ona_ref[...]
    za = proj(2 * D_A, D_A)
    oa = (o * _silu(za)).astype(BF16)

    q = proj(3 * D_A, D_B)
    q = q * lax.rsqrt(_group_mean_sq(q, g512_ref) + EPS) * qn_ref[...]
    szb = _silu(proj(3 * D_A + D_B, D_B)).astype(BF16)
    base = 3 * D_A + 2 * D_B
    kc = proj(base, KV_W)
    vc = proj(base + KV_W, KV_W)
    ks = proj(base + 2 * KV_W, KV_W)
    ks = ks * lax.rsqrt(_group_mean_sq(ks, g128_ref) + EPS) * ksn_ref[...]
    vs = proj(base + 3 * KV_W, KV_W)
    kw = proj(base + 4 * KV_W, KV_W)
    kw = kw * lax.rsqrt(_group_mean_sq(kw, g128_ref) + EPS) * kwn_ref[...]
    vw = proj(base + 5 * KV_W, KV_W)
    gates = jax.nn.sigmoid(proj(base + 6 * KV_W, LANES) + bg_ref[...])
    return oa, vn, q, szb, gates, kc, vc, ks, vs, kw, vw


N_PROJ_CONSTS = 13


def _proj_sample_kernel(*refs):
    ins, outs = refs[:1 + N_PROJ_CONSTS], refs[1 + N_PROJ_CONSTS:]
    for ref, val in zip(outs, _proj_values(*ins)):
        ref[...] = val


def _proj_prompt_kernel(*refs, seq):
    ins = refs[:1 + N_PROJ_CONSTS]
    (oa_ref, szb_ref, qt_ref, gt_ref, ksa_ref, kwa_ref, kcr_ref, vcr_ref,
     kct_ref, vct_ref, kst_ref, vst_ref, kwt_ref, vwt_ref) = refs[1 + N_PROJ_CONSTS:]
    oa, _, q, szb, gates, kc, vc, ks, vs, kw, vw = _proj_values(*ins)
    tn = oa.shape[0]
    t0 = lax.rem(pl.program_id(0) * tn, seq)
    oa_ref[...] = oa
    szb_ref[...] = szb
    kcr_ref[...] = kc
    vcr_ref[...] = vc
    for ref, val in ((kct_ref, kc), (vct_ref, vc), (kst_ref, ks), (vst_ref, vs), (kwt_ref, kw),
                     (vwt_ref, vw)):
        ref[0] = val.T
    gt_ref[0] = gates.T[:3 * H_B]
    pos_rows = t0 + lax.broadcasted_iota(jnp.int32, (tn, 1), 0)
    ksa_ref[0] = _key_aug(ks, pos_rows)
    kwa_ref[0] = _key_aug(kw, pos_rows)
    pos = t0 + lax.broadcasted_iota(jnp.int32, (HD, tn), 1)
    row = lax.broadcasted_iota(jnp.int32, (HD, tn), 0)
    sh = SEL_BLK.bit_length() - 1
    blk = (jnp.right_shift(pos, sh) * SEL_BLK).astype(F32)
    off = jnp.bitwise_and(pos, SEL_BLK - 1).astype(F32)
    qs = q * SCALE
    for m in range(D_B // LANES):
        qt = qs[:, m * LANES:(m + 1) * LANES].T
        for i in range(LANES // HD):
            h = (LANES // HD) * m + i
            slope = 2.0 ** -(h + 1)
            feats = jnp.where(row < 2, slope, jnp.where(row == 2, -slope * blk,
                              jnp.where(row == 3, -slope * off, 0.0)))
            qt_ref[0, h, :HD, :] = qt[i * HD:(i + 1) * HD].astype(BF16)
            qt_ref[0, h, HD:, :] = feats.astype(BF16)


def _row_tile(n):
    sizes = [tn for tn in (1024, 512, 256, 128) if n % tn == 0]
    for tn in sizes:
        if n // tn >= 2:
            return tn
    if sizes:
        return sizes[-1]
    raise ValueError(f"row count {n} must be a multiple of {CHUNK}")


def _proj_consts(wts):
    return [wts["nin"], wts["w_in"], wts["bg"], wts["lng"], wts["lnb"], wts["wc"], wts["bc"],
            wts["qn"], wts["ksn"], wts["kwn"], wts["ona"], wts["g512"], wts["g128"]]


def _projection_sample(x2d, wts):
    n = x2d.shape[0]
    tn = _row_tile(n)
    full = lambda a: pl.BlockSpec(a.shape, lambda i: (0,) * a.ndim)
    consts = _proj_consts(wts)
    row = lambda w: pl.BlockSpec((tn, w), lambda i: (i, 0))
    out_w = [(D_A, BF16), (D_A, F32), (D_B, F32), (D_B, BF16), (LANES, F32)] + [(KV_W, F32)] * 6
    return pl.pallas_call(
        _proj_sample_kernel,
        grid=(n // tn,),
        in_specs=[row(D_MODEL)] + [full(a) for a in consts],
        out_specs=[row(w) for w, _ in out_w],
        out_shape=[jax.ShapeDtypeStruct((n, w), dt) for w, dt in out_w],
        compiler_params=pltpu.CompilerParams(dimension_semantics=("arbitrary",),
                                             vmem_limit_bytes=VMEM_LIMIT),
        name="projection_sample",
    )(x2d, *consts)


def _projection_prompt(x2d, b, seq, wts):
    n = x2d.shape[0]
    tn = _row_tile(seq)
    tps = seq // tn
    full = lambda a: pl.BlockSpec(a.shape, lambda i: (0,) * a.ndim)
    consts = _proj_consts(wts)
    row = lambda w: pl.BlockSpec((tn, w), lambda i: (i, 0))
    minor = lambda r: pl.BlockSpec((1, r, tn), lambda i: (i // tps, 0, i % tps))
    specs = [(row(D_A), (n, D_A), BF16), (row(D_B), (n, D_B), BF16),
             (pl.BlockSpec((1, H_B, LANES, tn), lambda i: (i // tps, 0, 0, i % tps)),
              (b, H_B, LANES, seq), BF16),
             (minor(3 * H_B), (b, 3 * H_B, seq), F32)]
    specs += [(pl.BlockSpec((1, tn, 2 * KV_W), lambda i: (i // tps, i % tps, 0)),
               (b, seq, 2 * KV_W), BF16)] * 2
    specs += [(row(KV_W), (n, KV_W), F32)] * 2
    specs += [(minor(KV_W), (b, KV_W, seq), F32)] * 6
    return pl.pallas_call(
        functools.partial(_proj_prompt_kernel, seq=seq),
        grid=(n // tn,),
        in_specs=[row(D_MODEL)] + [full(a) for a in consts],
        out_specs=[s for s, _, _ in specs],
        out_shape=[jax.ShapeDtypeStruct(shape, dt) for _, shape, dt in specs],
        compiler_params=pltpu.CompilerParams(dimension_semantics=("arbitrary",),
                                             vmem_limit_bytes=VMEM_LIMIT),
        name="projection_prompt",
    )(x2d, *consts)


def _merge_kernel(x_ref, oa_ref, ob_ref, szb_ref, onb_ref, w_ref, y_ref):
    ob = ob_ref[...]
    ob = ob * lax.rsqrt(jnp.mean(ob * ob, axis=-1, keepdims=True) + EPS) * onb_ref[...]
    ob = (ob * szb_ref[...].astype(F32)).astype(BF16)
    y_ref[...] = x_ref[...] + _dot(oa_ref[...], w_ref[:D_A, :]) + _dot(ob, w_ref[D_A:, :])


def _merge(x2d, oa, ob, szb, wts):
    n = x2d.shape[0]
    tn = _row_tile(n)
    row = lambda w: pl.BlockSpec((tn, w), lambda i: (i, 0))
    full = lambda a: pl.BlockSpec(a.shape, lambda i: (0,) * a.ndim)
    return pl.pallas_call(
        _merge_kernel,
        grid=(n // tn,),
        in_specs=[row(D_MODEL), row(D_A), row(D_B), row(D_B), full(wts["onb"]), full(wts["w_out"])],
        out_specs=row(D_MODEL),
        out_shape=jax.ShapeDtypeStruct((n, D_MODEL), F32),
        compiler_params=pltpu.CompilerParams(dimension_semantics=("arbitrary",),
                                             vmem_limit_bytes=VMEM_LIMIT),
        name="merge",
    )(x2d, oa, ob, szb, wts["onb"], wts["w_out"])


def _compress_math(xb, w1_ref, pea_ref, peb_ref, w2_ref):
    h1 = _dot(xb, w1_ref[0])
    h2 = _dot(xb, w1_ref[1])
    r = h2.shape[0]
    h2n = pltpu.roll(h2, r - 1, 0)
    pe = _dot(pea_ref[...], w1_ref[0]) + _dot(peb_ref[...], w1_ref[1])
    pre = h1 + h2n + pe[0:1]
    return _dot(_silu(pre).astype(BF16), w2_ref[...])


def _compress_kernel(rk_ref, rv_ref, w1k_ref, w1v_ref, peak_ref, pebk_ref, peav_ref, pebv_ref,
                     w2k_ref, w2v_ref, kn_ref, g128_ref, kc_ref, vc_ref):
    kc = _compress_math(rk_ref[0].astype(BF16), w1k_ref, peak_ref, pebk_ref, w2k_ref)
    kc = kc * lax.rsqrt(_group_mean_sq(kc, g128_ref) + EPS) * kn_ref[...]
    end = CMP_STRIDE * lax.broadcasted_iota(jnp.int32, (kc.shape[0], 1), 0) + (CMP_BLK - 1)
    kc_ref[0] = _key_aug(kc, end)
    vc = _compress_math(rv_ref[0].astype(BF16), w1v_ref, peav_ref, pebv_ref, w2v_ref)
    vc_ref[0] = vc.T.astype(BF16)


def _cmp_consts(wts):
    return [wts["w1k"], wts["w1v"], wts["peak"], wts["pebk"], wts["peav"], wts["pebv"],
            wts["w2k"], wts["w2v"], wts["kcn"], wts["g128"]]


def _compress_prompt(rk, rv, wts):
    b, r, w = rk.shape
    consts = _cmp_consts(wts)
    full = lambda a: pl.BlockSpec(a.shape, lambda i: (0,) * a.ndim)
    blk = pl.BlockSpec((1, r, w), lambda i: (i, 0, 0))
    return pl.pallas_call(
        _compress_kernel,
        grid=(b,),
        in_specs=[blk, blk] + [full(a) for a in consts],
        out_specs=[pl.BlockSpec((1, r, 2 * KV_W), lambda i: (i, 0, 0)),
                   pl.BlockSpec((1, KV_W, r), lambda i: (i, 0, 0))],
        out_shape=[jax.ShapeDtypeStruct((b, r, 2 * KV_W), BF16),
                   jax.ShapeDtypeStruct((b, KV_W, r), BF16)],
        compiler_params=pltpu.CompilerParams(dimension_semantics=("arbitrary",),
                                             vmem_limit_bytes=VMEM_LIMIT),
        name="compress_prompt",
    )(rk, rv, *consts)


def _topk_rows(score, k):
    n = score.shape[0]
    ridx = lax.broadcasted_iota(jnp.int32, score.shape, 0).astype(F32)
    sel = jnp.zeros(score.shape, jnp.bool_)
    for _ in range(k):
        m = jnp.max(score, axis=0, keepdims=True)
        idx = jnp.min(jnp.where(score == m, ridx, float(n)), axis=0, keepdims=True)
        hit = ridx == idx
        sel = jnp.logical_or(sel, hit)
        score = jnp.where(hit, -jnp.inf, score)
    return sel


def _online_update(carry, s, vt):
    m, l, acc = carry
    m_new = jnp.maximum(m, jnp.max(s, axis=0, keepdims=True))
    alpha = jnp.exp(m - m_new)
    p = jnp.exp(s - m_new)
    l = alpha * l + jnp.sum(p, axis=0, keepdims=True)
    acc = alpha * acc + _dot(vt, p.astype(BF16))
    return m_new, l, acc


def _pattn_kernel(qt_ref, gt_ref, ks_ref, vst_ref, kw_ref, vwt_ref, kc_ref, vct_ref, cmask_ref,
                  dmask_ref, wmask_ref, o_ref, imp_sc, bias_sc):
    t = pl.program_id(1)
    n_sb = bias_sc.shape[1]
    pair = dmask_ref.shape[1]
    q0 = t * Q_BLOCK
    qpos = q0 + lax.broadcasted_iota(jnp.int32, (1, Q_BLOCK), 1)
    gls = [slice(g * LANES, (g + 1) * LANES) for g in range(KV_B)]
    gds = [slice(g * HD, (g + 1) * HD) for g in range(KV_B)]
    qts = [qt_ref[0, h] for h in range(H_B)]

    cmask = cmask_ref[0]
    kcs = [kc_ref[0, :, gls[g]] for g in range(KV_B)]
    vcts = [vct_ref[0, gds[g], :] for g in range(KV_B)]

    def cmp_softmax(h, sc):
        pc = jnp.exp(sc - jnp.maximum(jnp.max(sc, axis=0, keepdims=True), 0.1 * NEG))
        lc = jnp.sum(pc, axis=0, keepdims=True)
        pc = pc * (1.0 / jnp.where(lc > 0.0, lc, 1.0))
        return _dot(vcts[h // G_B], pc.astype(BF16)), pc

    qtg = [jnp.concatenate(qts[G_B * g:G_B * (g + 1)], axis=1) for g in range(KV_B)]
    head = lambda wide, h: wide[h // G_B][:, (h % G_B) * Q_BLOCK:(h % G_B + 1) * Q_BLOCK]
    s_c = [_dot(kcs[g], qtg[g]) for g in range(KV_B)]
    cmp_out = [cmp_softmax(h, head(s_c, h) + cmask) for h in range(H_B)]
    o_c = [o for o, _ in cmp_out]
    for g in range(KV_B):
        imp_sc[g] = sum(pc for _, pc in cmp_out[G_B * g:G_B * (g + 1)])
        ratio = SEL_BLK // CMP_STRIDE
        imp = sum(imp_sc[g, pl.ds(r, n_sb, stride=ratio), :] for r in range(ratio))
        bi = lax.broadcasted_iota(jnp.int32, (n_sb, Q_BLOCK), 0)
        cur = jnp.right_shift(qpos, SEL_BLK.bit_length() - 1)
        forced = jnp.logical_or(bi == 0, bi > cur - N_LOCAL)
        score = jnp.where(bi > cur, -1e9, jnp.where(forced, 1e9, imp))
        bias_sc[g] = jnp.where(_topk_rows(score, min(N_SEL, n_sb)), 0.0, NEG)

    span = wmask_ref.shape[1]
    w0 = pl.multiple_of(jnp.maximum(q0 - WINDOW, 0), Q_BLOCK)
    wmask = wmask_ref[0]
    kws = [kw_ref[0, pl.ds(w0, span), gls[g]] for g in range(KV_B)]
    vwts = [vwt_ref[0, gds[g], pl.ds(w0, span)].astype(BF16) for g in range(KV_B)]

    def win_softmax(h, s):
        p = jnp.exp(s - jnp.max(s, axis=0, keepdims=True))
        l = jnp.sum(p, axis=0, keepdims=True)
        return _dot(vwts[h // G_B], p.astype(BF16)) * (1.0 / l)

    s_w = [_dot(kws[g], qtg[g]) for g in range(KV_B)]
    o_w = [win_softmax(h, head(s_w, h) + wmask) for h in range(H_B)]

    def sel_step(p, carry, diag):
        k0 = pl.multiple_of(p * pair, pair)
        bias, kt, vt = [], [], []
        for g in range(KV_B):
            rows = [jnp.broadcast_to(bias_sc[g, pl.ds((pair // SEL_BLK) * p + i, 1), :],
                                     (SEL_BLK, Q_BLOCK)) for i in range(pair // SEL_BLK)]
            bias.append(jnp.concatenate(rows, axis=0) + dmask_ref[0] if diag
                        else jnp.concatenate(rows, axis=0))
            kt.append(ks_ref[0, pl.ds(k0, pair), gls[g]])
            vt.append(vst_ref[0, gds[g], pl.ds(k0, pair)].astype(BF16))
        wide = [_dot(kt[g], qtg[g]) for g in range(KV_B)]
        return tuple(_online_update(carry[h], head(wide, h) + bias[h // G_B], vt[h // G_B])
                     for h in range(H_B))

    init = (jnp.full((1, Q_BLOCK), NEG, F32), jnp.zeros((1, Q_BLOCK), F32), jnp.zeros((HD, Q_BLOCK), F32))
    last = t // 2
    carry = lax.fori_loop(0, last, lambda p, c: sel_step(p, c, False), (init,) * H_B)
    o_s = [a * (1.0 / l) for _, l, a in sel_step(last, carry, True)]

    for m in range(H_B // 2):
        halves = [gt_ref[0, h:h + 1, :] * o_c[h] + gt_ref[0, H_B + h:H_B + h + 1, :] * o_s[h]
                  + gt_ref[0, 2 * H_B + h:2 * H_B + h + 1, :] * o_w[h] for h in (2 * m, 2 * m + 1)]
        o_ref[:, m * LANES:(m + 1) * LANES] = jnp.concatenate(halves, axis=0).T


def _prompt_attention(qt, gt, ksa, vst, kwa, vwt, kca, vct):
    b, _, _, t = qt.shape
    nb = kca.shape[1]
    n_sb = t // SEL_BLK
    nt = t // Q_BLOCK
    pair, span, n_wt = 2 * Q_BLOCK, WINDOW + Q_BLOCK, WINDOW // Q_BLOCK
    assert t % pair == 0 and t >= span
    q = np.arange(Q_BLOCK)[None, None, :]
    bias = lambda ok: jnp.asarray(np.where(ok, 0.0, NEG), F32)
    ci = np.arange(nb)[None, :, None]
    tq = Q_BLOCK * np.arange(nt)[:, None, None] + q
    cmask = bias((tq >= CMP_STRIDE * ci + CMP_BLK - 1) & (ci < nb - 1))
    r = np.arange(pair)[None, :, None]
    dmask = bias(r <= Q_BLOCK * np.arange(2)[:, None, None] + q)
    r = np.arange(span)[None, :, None]
    dist = Q_BLOCK * np.arange(n_wt + 1)[:, None, None] + q - r
    wmask = bias((dist >= 0) & (dist < WINDOW))
    per_b = lambda a: pl.BlockSpec((1,) + a.shape[1:], lambda i, j: (i,) + (0,) * (a.ndim - 1))
    return pl.pallas_call(
        _pattn_kernel,
        grid=(b, nt),
        in_specs=[pl.BlockSpec((1, H_B, LANES, Q_BLOCK), lambda i, j: (i, 0, 0, j)),
                  pl.BlockSpec((1, 3 * H_B, Q_BLOCK), lambda i, j: (i, 0, j)),
                  per_b(ksa), per_b(vst), per_b(kwa), per_b(vwt), per_b(kca), per_b(vct),
                  pl.BlockSpec((1, nb, Q_BLOCK), lambda i, j: (j, 0, 0)),
                  pl.BlockSpec((1, pair, Q_BLOCK), lambda i, j: (lax.rem(j, 2), 0, 0)),
                  pl.BlockSpec((1, span, Q_BLOCK), lambda i, j: (jnp.minimum(j, n_wt), 0, 0))],
        out_specs=pl.BlockSpec((Q_BLOCK, D_B), lambda i, j: (i * (t // Q_BLOCK) + j, 0)),
        out_shape=jax.ShapeDtypeStruct((b * t, D_B), F32),
        scratch_shapes=[pltpu.VMEM((KV_B, nb, Q_BLOCK), F32), pltpu.VMEM((KV_B, n_sb, Q_BLOCK), F32)],
        compiler_params=pltpu.CompilerParams(dimension_semantics=("arbitrary", "arbitrary"),
                                             vmem_limit_bytes=VMEM_LIMIT),
        name="prompt_attention",
    )(qt, gt, ksa, vst, kwa, vwt, kca, vct, cmask, dmask, wmask)


def _page_copy(hbm, buf, sem, page, slot, which, p):
    return pltpu.make_async_copy(hbm.at[page], buf.at[slot, which, p], sem.at[slot, which])


def _gather_pages(pt_ref, hbm_a, hbm_b, buf, sem):
    b = pl.program_id(0)
    nb = pl.num_programs(0)
    n_pages = pt_ref.shape[1]
    slot = lax.rem(b, 2)

    def issue(bb, sl):
        for p in range(n_pages):
            page = pt_ref[bb, p]
            _page_copy(hbm_a, buf, sem, page, sl, 0, p).start()
            _page_copy(hbm_b, buf, sem, page, sl, 1, p).start()

    @pl.when(b == 0)
    def _():
        issue(0, 0)

    @pl.when(b + 1 < nb)
    def _():
        issue(b + 1, 1 - slot)

    for p in range(n_pages):
        _page_copy(hbm_a, buf, sem, 0, slot, 0, p).wait()
        _page_copy(hbm_b, buf, sem, 0, slot, 1, p).wait()
    return slot


SUBLANES = 8


def _sublane_transpose(tiles):
    tiles = list(tiles)
    sub = lax.broadcasted_iota(jnp.int32, tiles[0].shape, 0)
    k = 1
    while k < SUBLANES:
        hi = jnp.bitwise_and(sub, k) != 0
        for c in range(SUBLANES):
            if c & k:
                continue
            a, b = tiles[c], tiles[c | k]
            tiles[c] = jnp.where(hi, pltpu.roll(b, k, 0), a)
            tiles[c | k] = jnp.where(hi, b, pltpu.roll(a, SUBLANES - k, 0))
        k *= 2
    return tiles


def _chunk_rows(buf, slot, which, lhs_sc):
    n_pages, _, page = buf.shape[2:]
    cpp = page // CMP_STRIDE
    assert cpp == SUBLANES and CMP_STRIDE % SUBLANES == 0
    for pp in range(n_pages // 2):
        per_j = []
        for e in range(2):
            x = buf[slot, which, 2 * pp + e].T
            cols = [None] * CMP_STRIDE
            for h in range(CMP_STRIDE // SUBLANES):
                tiles = [x[CMP_STRIDE * c + SUBLANES * h:CMP_STRIDE * c + SUBLANES * (h + 1)]
                         for c in range(cpp)]
                cols[SUBLANES * h:SUBLANES * (h + 1)] = _sublane_transpose(tiles)
            per_j.append(cols)
        for j in range(CMP_STRIDE):
            lhs_sc[which, 2 * cpp * pp:2 * cpp * (pp + 1), KV_W * j:KV_W * (j + 1)] = (
                jnp.concatenate([per_j[0][j], per_j[1][j]], axis=0).astype(BF16))


def _scmp_kernel(pt_ref, kc_hbm, vc_hbm, qk_ref, slope_ref, irow_ref, w1k_ref, w1v_ref, peak_ref,
                 pebk_ref, peav_ref, pebv_ref, w2k_ref, w2v_ref, kn_ref, g128_ref, gsum_ref,
                 oc_ref, imp_ref, buf, sem, lhs_sc, *, past_len):
    slot = _gather_pages(pt_ref, kc_hbm, vc_hbm, buf, sem)
    _chunk_rows(buf, slot, 0, lhs_sc)
    _chunk_rows(buf, slot, 1, lhs_sc)
    kc = _compress_math(lhs_sc[0], w1k_ref, peak_ref, pebk_ref, w2k_ref)
    kc = kc * lax.rsqrt(_group_mean_sq(kc, g128_ref) + EPS) * kn_ref[...]
    vc = _compress_math(lhs_sc[1], w1v_ref, peav_ref, pebv_ref, w2v_ref)
    nb = kc.shape[0]
    nr = qk_ref.shape[1]

    s = _dot_nt(qk_ref[0], kc.astype(BF16))
    ci = lax.broadcasted_iota(jnp.int32, (nr, nb), 1)
    end = (CMP_STRIDE * ci + (CMP_BLK - 1)).astype(F32)
    dist = (past_len + irow_ref[:, 0:1]) - end
    valid = jnp.logical_and(dist >= 0.0, ci < nb - 1)
    s = jnp.where(valid, s - slope_ref[:, 0:1] * dist, NEG)
    p = jnp.where(valid, jnp.exp(s - jnp.max(s, axis=-1, keepdims=True)), 0.0)
    l = jnp.sum(p, axis=-1, keepdims=True)
    p = p * (1.0 / jnp.where(l > 0.0, l, 1.0))
    oc_ref[0] = _dot(p.astype(BF16), vc.astype(BF16))

    tq = nr // H_B
    parts = []
    for g in range(KV_B):
        base = g * G_B * tq
        parts.append(sum(p[base + hh * tq:base + (hh + 1) * tq] for hh in range(G_B)))
    imp = jnp.concatenate(parts, axis=0)
    hi = imp.astype(BF16)
    r1 = imp - hi.astype(F32)
    mid = r1.astype(BF16)
    lo = (r1 - mid.astype(F32)).astype(BF16)
    gs = gsum_ref[...]
    imp_ref[0] = _dot(hi, gs) + _dot(mid, gs) + _dot(lo, gs)


def _sample_compressed(page_table, kc_pages, vc_pages, qk, slope, irow, wts, past_len):
    b, n_pages = page_table.shape
    page = kc_pages.shape[2]
    cpp = page // CMP_STRIDE
    nb = n_pages * cpp
    nr = qk.shape[1]
    ratio = SEL_BLK // CMP_STRIDE
    n_blk = -(-(nb // ratio + 1) // LANES) * LANES
    gsum = jnp.asarray(np.arange(nb)[:, None] // ratio == np.arange(n_blk)[None, :], BF16)
    consts = [slope, irow] + _cmp_consts(wts) + [gsum]
    full = lambda a: pl.BlockSpec(a.shape, lambda i, pt: (0,) * a.ndim)
    row = lambda i, pt: (i, 0, 0)
    grid_spec = pltpu.PrefetchScalarGridSpec(
        num_scalar_prefetch=1,
        grid=(b,),
        in_specs=[pl.BlockSpec(memory_space=pl.ANY), pl.BlockSpec(memory_space=pl.ANY),
                  pl.BlockSpec((1, nr, LANES), row)] + [full(a) for a in consts],
        out_specs=[pl.BlockSpec((1, nr, KV_W), row), pl.BlockSpec((1, nr // G_B, n_blk), row)],
        scratch_shapes=[pltpu.VMEM((2, 2, n_pages, KV_W, page), F32),
                        pltpu.SemaphoreType.DMA((2, 2)),
                        pltpu.VMEM((2, nb, CMP_STRIDE * KV_W), BF16)],
    )
    return pl.pallas_call(
        functools.partial(_scmp_kernel, past_len=float(past_len)),
        grid_spec=grid_spec,
        out_shape=[jax.ShapeDtypeStruct((b, nr, KV_W), F32),
                   jax.ShapeDtypeStruct((b, nr // G_B, n_blk), F32)],
        compiler_params=pltpu.CompilerParams(dimension_semantics=("arbitrary",),
                                             vmem_limit_bytes=VMEM_LIMIT),
        name="sample_compressed",
    )(page_table, kc_pages, vc_pages, qk, *consts)


def _topk_lanes_kernel(imp_ref, out_ref, *, n_blocks, cur, k):
    imp = imp_ref[...]
    bi = lax.broadcasted_iota(jnp.int32, imp.shape, 1).astype(F32)
    forced = jnp.logical_or(bi == 0, bi > cur - N_LOCAL)
    score = jnp.where(bi > cur, -1e9, jnp.where(forced, 1e9, imp))
    score = jnp.where(bi < n_blocks, score, -jnp.inf)
    width = imp.shape[1]
    sel = jnp.zeros(imp.shape, jnp.bool_)
    for _ in range(k):
        m = jnp.max(score, axis=-1, keepdims=True)
        idx = jnp.min(jnp.where(score == m, bi, float(width)), axis=-1, keepdims=True)
        hit = bi == idx
        sel = jnp.logical_or(sel, hit)
        score = jnp.where(hit, -jnp.inf, score)
    out_ref[...] = jnp.where(sel, 0.0, NEG)


def _sample_topk(imp2d, n_blocks, cur):
    n, w = imp2d.shape
    tr = 256 if n % 256 == 0 else n
    blk = pl.BlockSpec((tr, w), lambda i: (i, 0))
    return pl.pallas_call(
        functools.partial(_topk_lanes_kernel, n_blocks=n_blocks, cur=cur, k=min(N_SEL, n_blocks)),
        grid=(n // tr,),
        in_specs=[blk],
        out_specs=blk,
        out_shape=jax.ShapeDtypeStruct((n, w), F32),
        compiler_params=pltpu.CompilerParams(dimension_semantics=("arbitrary",)),
        name="sample_topk",
    )(imp2d)


def _shift_in(st_ref, tail_ref, out_ref, n_new):
    x = st_ref[0]
    n_buf = x.shape[1]
    rolled = pltpu.roll(x, n_buf - n_new, 1)
    lane = lax.broadcasted_iota(jnp.int32, (x.shape[0], LANES), 1)
    last = jnp.where(lane >= LANES - n_new, tail_ref[0], rolled[:, n_buf - LANES:])
    out_ref[0] = jnp.concatenate([rolled[:, :n_buf - LANES], last], axis=1)


def _ssel_kernel(pt_ref, ks_hbm, vs_hbm, qsel_ref, knewt_ref, vnew_ref, kwst_ref, vwst_ref, kwnewt_ref,
                 vwnew_ref, kwtail_ref, vwtail_ref, oc_ref, gsel_ref, slope_ref, irow_ref,
                 o_ref, kwout_ref, vwout_ref, buf, sem, s_sc, oh_sc, *, past_len, n_new):
    b = pl.program_id(0)
    _shift_in(kwst_ref, kwtail_ref, kwout_ref, n_new)
    _shift_in(vwst_ref, vwtail_ref, vwout_ref, n_new)
    n_pages, _, page = buf.shape[2:]
    nr = qsel_ref.shape[1]
    ppt = next(c for c in (4, 2, 1) if n_pages % c == 0)
    tile = ppt * page

    @pl.when(b == 0)
    def _():
        bi = lax.broadcasted_iota(jnp.int32, oh_sc.shape, 0)
        ki = lax.broadcasted_iota(jnp.int32, oh_sc.shape, 1)
        oh_sc[...] = jnp.where(jnp.right_shift(ki, SEL_BLK.bit_length() - 1) == bi, 1.0, 0.0).astype(BF16)

    slot = _gather_pages(pt_ref, ks_hbm, vs_hbm, buf, sem)
    slope = slope_ref[:, 0:1]
    irow = irow_ref[:, 0:1]
    qpos = past_len + irow
    qall = qsel_ref[0]
    qk = qall[:, :KV_W]

    def key_tile(which, j):
        return jnp.concatenate([buf[slot, which, ppt * j + i] for i in range(ppt)], axis=1).astype(BF16)

    lane = lax.broadcasted_iota(jnp.int32, (nr, tile), 1).astype(F32)
    mrun = jnp.full((nr, tile), NEG, F32)
    for j in range(n_pages // ppt):
        kcat = jnp.concatenate([key_tile(0, j), oh_sc[:, j * tile:(j + 1) * tile]], axis=0)
        s = _dot(qall, kcat) - slope * (qpos - (float(j * tile) + lane))
        s_sc[:, j * tile:(j + 1) * tile] = s
        mrun = jnp.maximum(mrun, s)
    nl = lax.broadcasted_iota(jnp.int32, (nr, LANES), 1).astype(F32)
    dn = irow - nl
    s_new = jnp.where(dn >= 0.0, _dot(qk, knewt_ref[0]) - slope * dn, NEG)
    m = jnp.maximum(jnp.max(mrun, axis=-1, keepdims=True), jnp.max(s_new, axis=-1, keepdims=True))

    p_new = jnp.exp(s_new - m)
    acc = _dot(p_new.astype(BF16), vnew_ref[0])
    lrun = jnp.zeros((nr, tile), F32)
    for j in range(n_pages // ppt):
        p = jnp.exp(s_sc[:, j * tile:(j + 1) * tile] - m)
        acc = acc + _dot_nt(p.astype(BF16), key_tile(1, j))
        lrun = lrun + p
    l = jnp.sum(lrun, axis=-1, keepdims=True) + jnp.sum(p_new, axis=-1, keepdims=True)
    o_s = acc * (1.0 / l)

    n_buf = kwst_ref.shape[2]
    wl = lax.broadcasted_iota(jnp.int32, (nr, n_buf), 1).astype(F32)
    dw = (n_buf + irow) - wl
    s_w = jnp.where(jnp.logical_and(dw >= 0.0, dw < WINDOW),
                    _dot(qk, kwst_ref[0].astype(BF16)) - slope * dw, NEG)
    s_wn = jnp.where(dn >= 0.0, _dot(qk, kwnewt_ref[0]) - slope * dn, NEG)
    m_w = jnp.maximum(jnp.max(s_w, axis=-1, keepdims=True), jnp.max(s_wn, axis=-1, keepdims=True))
    p_w = jnp.exp(s_w - m_w)
    p_wn = jnp.exp(s_wn - m_w)
    l_w = jnp.sum(p_w, axis=-1, keepdims=True) + jnp.sum(p_wn, axis=-1, keepdims=True)
    o_w = (_dot_nt(p_w.astype(BF16), vwst_ref[0].astype(BF16))
           + _dot(p_wn.astype(BF16), vwnew_ref[0])) * (1.0 / l_w)

    gsel = gsel_ref[0]
    o_ref[0] = gsel[:, 0:1] * oc_ref[0] + gsel[:, 1:2] * o_s + gsel[:, 2:3] * o_w


def _sample_selected(page_table, ks_pages, vs_pages, qsel, knewt, vnew, kwst, vwst, kwnewt, vwnew, kwtail,
                     vwtail, oc, gsel, slope, irow, past_len, n_new):
    b, n_pages = page_table.shape
    page = ks_pages.shape[2]
    n_keys = n_pages * page
    nr = qsel.shape[1]
    assert kwst.shape[2] % LANES == 0 and n_new <= LANES
    per_b = lambda a: pl.BlockSpec((1,) + a.shape[1:], lambda i, pt: (i,) + (0,) * (a.ndim - 1))
    full = lambda a: pl.BlockSpec(a.shape, lambda i, pt: (0,) * a.ndim)
    grid_spec = pltpu.PrefetchScalarGridSpec(
        num_scalar_prefetch=1,
        grid=(b,),
        in_specs=[pl.BlockSpec(memory_space=pl.ANY), pl.BlockSpec(memory_space=pl.ANY)]
                 + [per_b(a) for a in (qsel, knewt, vnew, kwst, vwst, kwnewt, vwnew, kwtail, vwtail, oc, gsel)]
                 + [full(slope), full(irow)],
        out_specs=[pl.BlockSpec((1, nr, KV_W), lambda i, pt: (i, 0, 0)), per_b(kwst), per_b(vwst)],
        scratch_shapes=[pltpu.VMEM((2, 2, n_pages, KV_W, page), F32),
                        pltpu.SemaphoreType.DMA((2, 2)),
                        pltpu.VMEM((nr, n_keys), F32),
                        pltpu.VMEM((n_keys // SEL_BLK, n_keys), BF16)],
    )
    return pl.pallas_call(
        functools.partial(_ssel_kernel, past_len=float(past_len), n_new=n_new),
        grid_spec=grid_spec,
        out_shape=[jax.ShapeDtypeStruct((b, nr, KV_W), F32),
                   jax.ShapeDtypeStruct(kwst.shape, F32), jax.ShapeDtypeStruct(vwst.shape, F32)],
        compiler_params=pltpu.CompilerParams(dimension_semantics=("arbitrary",),
                                             vmem_limit_bytes=VMEM_LIMIT),
        name="sample_selected",
    )(page_table, ks_pages, vs_pages, qsel, knewt, vnew, kwst, vwst, kwnewt, vwnew, kwtail, vwtail, oc,
      gsel, slope, irow)


def _slopes():
    return np.asarray([2.0 ** -(h + 1) for h in range(H_B)], np.float32)


def _block_ones(n):
    idx = np.arange(n) // HD
    return jnp.asarray(idx[:, None] == idx[None, :], BF16)


def _layer_weights(l, t_chunk, norm_in, w_in, b_gate, ln_v_g, ln_v_b, w_spatial, b_spatial, q_norm,
                   k_norm_cmp, k_norm_sel, k_norm_win, cmp_pe_k, cmp_pe_v, w_cmp_k1, w_cmp_k2,
                   w_cmp_v1, w_cmp_v2, out_norm_a, out_norm_b, w_out):
    w = w_in[l]
    n_main = 3 * D_A + 2 * D_B
    n_gate = 3 * H_B
    w_pack = jnp.concatenate([w[:, :n_main], w[:, n_main + n_gate:], w[:, n_main:n_main + n_gate],
                              jnp.zeros((D_MODEL, N_PACK - w.shape[1]), F32)], axis=1).astype(BF16)
    c = t_chunk
    rep = CHUNK // c
    tri = jnp.where(jnp.tril(jnp.ones((c, c), bool)), w_spatial[l][:, :c, :c], 0.0)
    wc = jnp.einsum("ab,hij->haibj", jnp.eye(rep, dtype=F32), tri).reshape(H_A, CHUNK, CHUNK).astype(BF16)
    bc = jnp.tile(jnp.repeat(b_spatial[l][:, :c].T, D_A // H_A, axis=1), (rep, 1))

    def big1(w1):
        w1r = w1.reshape(2, CMP_STRIDE, HD, CMP_HID)
        return jnp.einsum("ajdh,ck->ajcdkh", w1r, jnp.eye(KV_B, dtype=F32)).reshape(
            2, CMP_STRIDE * KV_W, KV_B * CMP_HID).astype(BF16)

    def big2(w2):
        return jnp.einsum("hd,kc->khcd", w2, jnp.eye(KV_B, dtype=F32)).reshape(
            KV_B * CMP_HID, KV_W).astype(BF16)

    def pe_rows(pe):
        halves = pe.reshape(2, CMP_STRIDE, 1, HD)
        flat = jnp.broadcast_to(halves, (2, CMP_STRIDE, KV_B, HD)).reshape(2, 1, CMP_STRIDE * KV_W)
        flat = jnp.broadcast_to(flat, (2, 8, CMP_STRIDE * KV_W)).astype(BF16)
        return flat[0], flat[1]

    peak, pebk = pe_rows(cmp_pe_k[l])
    peav, pebv = pe_rows(cmp_pe_v[l])
    row = lambda a, reps: jnp.tile(a, reps)[None, :]
    bg = jnp.concatenate([b_gate[l], jnp.zeros((LANES - n_gate,), F32)])[None, :]
    return dict(
        nin=norm_in[l][None, :], w_in=w_pack, bg=bg, lng=ln_v_g[l][None, :], lnb=ln_v_b[l][None, :],
        wc=wc, bc=bc, qn=row(q_norm[l], H_B), ksn=row(k_norm_sel[l], KV_B), kwn=row(k_norm_win[l], KV_B),
        kcn=row(k_norm_cmp[l], KV_B), ona=out_norm_a[l][None, :], onb=out_norm_b[l][None, :],
        g512=_block_ones(D_B), g128=_block_ones(KV_W),
        w1k=big1(w_cmp_k1[l]), w1v=big1(w_cmp_v1[l]), w2k=big2(w_cmp_k2[l]), w2v=big2(w_cmp_v2[l]),
        peak=peak, pebk=pebk, peav=peav, pebv=pebv, w_out=w_out[l].astype(BF16))


def _prompt_layer(x, wts):
    b, t, _ = x.shape
    n = b * t
    x2d = x.reshape(n, D_MODEL)
    (oa, szb, qt, gt, ksa, kwa, kc_r, vc_r,
     kct, vct, kst, vst, kwt, vwt) = _projection_prompt(x2d, b, t, wts)
    nc = t // CMP_STRIDE
    kca, vcct = _compress_prompt(kc_r.reshape(b, nc, CMP_STRIDE * KV_W),
                                 vc_r.reshape(b, nc, CMP_STRIDE * KV_W), wts)
    ob = _prompt_attention(qt, gt, ksa, vst, kwa, vwt, kca, vcct)
    y = _merge(x2d, oa, ob, szb, wts).reshape(b, t, D_MODEL)
    heads = lambda a: jnp.transpose(a.reshape(b, KV_B, HD, a.shape[2]), (0, 3, 1, 2))
    n_keep = min(WINDOW, t)
    return y, (heads(kct), heads(vct), heads(kst), heads(vst),
               heads(kwt[:, :, t - n_keep:]), heads(vwt[:, :, t - n_keep:]))


def _sample_layer(x, l, cache_k_cmp, cache_v_cmp, cache_k_sel, cache_v_sel, k_win_buf, v_win_buf,
                  page_table, wts):
    b, t, _ = x.shape
    n = b * t
    n_pool, page = cache_k_cmp.shape[1], cache_k_cmp.shape[2]
    n_pages = page_table.shape[1]
    past_len = n_pages * page
    x2d = x.reshape(n, D_MODEL)
    oa, vn, q, szb, gates, kc_r, vc_r, ks, vs, kw, vw = _projection_sample(x2d, wts)

    nr = H_B * t
    sl = _slopes()
    slope = jnp.asarray(np.broadcast_to(np.repeat(sl, t)[:, None], (nr, LANES)).copy())
    irow = jnp.asarray(np.broadcast_to(np.tile(np.arange(t, dtype=np.float32), H_B)[:, None], (nr, LANES)).copy())
    qh = jnp.transpose((q * SCALE).reshape(b, t, KV_B, G_B, HD), (0, 2, 3, 1, 4))
    qk = jnp.einsum("bghtd,gk->bghtkd", qh, jnp.eye(KV_B, dtype=F32)).reshape(b, nr, KV_W).astype(BF16)

    minor_pos = lambda a: jnp.transpose(a, (0, 2, 3, 1)).reshape(a.shape[0], KV_W, a.shape[1])
    oc, imp = _sample_compressed(page_table, minor_pos(cache_k_cmp[l]), minor_pos(cache_v_cmp[l]),
                                 qk, slope, irow, wts, past_len)

    n_sb = -(-(past_len + t) // SEL_BLK)
    n_pb = past_len // SEL_BLK
    cur = past_len // SEL_BLK
    wpad = imp.shape[2]
    bias = _sample_topk(imp.reshape(b * KV_B * t, wpad), n_sb, cur).reshape(b, KV_B, 1, t, wpad)
    bias = jnp.broadcast_to(bias[..., :n_pb], (b, KV_B, G_B, t, n_pb)).reshape(b, nr, n_pb)
    qsel = jnp.concatenate([qk, bias.astype(BF16)], axis=-1)

    pad_rows = lambda a: jnp.pad(a.reshape(b, t, KV_W).astype(BF16), ((0, 0), (0, LANES - t), (0, 0)))
    pad_cols = lambda a: jnp.transpose(pad_rows(a), (0, 2, 1))
    gsel = jnp.transpose(gates[:, :3 * H_B].reshape(b, t, 3, H_B), (0, 3, 1, 2)).reshape(b, nr, 3)
    gsel = jnp.pad(gsel, ((0, 0), (0, 0), (0, LANES - 3)))
    tail = lambda a: jnp.pad(jnp.transpose(a.reshape(b, t, KV_W), (0, 2, 1)), ((0, 0), (0, 0), (LANES - t, 0)))
    o, kw_next, vw_next = _sample_selected(
        page_table, minor_pos(cache_k_sel[l]), minor_pos(cache_v_sel[l]), qsel, pad_cols(ks), pad_rows(vs),
        minor_pos(k_win_buf), minor_pos(v_win_buf), pad_cols(kw), pad_rows(vw), tail(kw), tail(vw), oc, gsel,
        slope, irow, past_len, t)
    o5 = o.reshape(b, KV_B, G_B, t, KV_B, HD)
    ob = jnp.stack([o5[:, g, :, :, g, :] for g in range(KV_B)], axis=1)
    ob = jnp.transpose(ob, (0, 3, 1, 2, 4)).reshape(n, D_B)
    y = _merge(x2d, oa, ob, szb, wts).reshape(b, t, D_MODEL)
    heads = lambda a: a.reshape(b, t, KV_B, HD)
    major_pos = lambda a: jnp.transpose(a.reshape(b, KV_B, HD, a.shape[2]), (0, 3, 1, 2))
    return y, (heads(kc_r), heads(vc_r), heads(ks), heads(vs), major_pos(kw_next), major_pos(vw_next),
               vn.reshape(b, t, D_A))


def kernel(x_prompt, x_sample, cache_k_cmp, cache_v_cmp, cache_k_sel, cache_v_sel, state_k_win, state_v_win, page_table, norm_in, w_in, b_gate, ln_v_g, ln_v_b, w_spatial, b_spatial, q_norm, k_norm_cmp, k_norm_sel, k_norm_win, cmp_pe_k, cmp_pe_v, w_cmp_k1, w_cmp_k2, w_cmp_v1, w_cmp_v2, out_norm_a, out_norm_b, w_out):
    params = (norm_in, w_in, b_gate, ln_v_g, ln_v_b, w_spatial, b_spatial, q_norm, k_norm_cmp,
              k_norm_sel, k_norm_win, cmp_pe_k, cmp_pe_v, w_cmp_k1, w_cmp_k2, w_cmp_v1, w_cmp_v2,
              out_norm_a, out_norm_b, w_out)
    depth = w_in.shape[0]
    h_p, h_s = x_prompt, x_sample
    st_p, st_s = [], []
    for l in range(depth):
        wts_p = _layer_weights(l, min(h_p.shape[1], CHUNK), *params)
        wts_s = _layer_weights(l, min(h_s.shape[1], CHUNK), *params)
        h_p, sp = _prompt_layer(h_p, wts_p)
        h_s, ss = _sample_layer(h_s, l, cache_k_cmp, cache_v_cmp, cache_k_sel, cache_v_sel,
                                state_k_win[l], state_v_win[l], page_table, wts_s)
        st_p.append(sp)
        st_s.append(ss)
    new_p = [jnp.stack(ts) for ts in zip(*st_p)]
    new_s = [jnp.stack(ts) for ts in zip(*st_s)]
    return (h_p, h_s, new_p[0], new_p[1], new_p[2], new_p[3], new_p[4], new_p[5],
            new_s[0], new_s[1], new_s[2], new_s[3], new_s[4], new_s[5], new_s[6])
```

```python
import functools

import numpy as np
import jax
import jax.numpy as jnp
from jax import lax
from jax.experimental import pallas as pl
from jax.experimental.pallas import tpu as pltpu

F32 = jnp.float32
BF16 = jnp.bfloat16

D_MODEL = 1024
D_A = 512
H_A = 8
CHUNK = 128
D_B = 512
H_B = 8
HD = 64
KV_B = 2
G_B = 4
KV_W = KV_B * HD
CMP_STRIDE = 16
CMP_BLK = 32
CMP_HID = 128
SEL_BLK = 64
N_SEL = 16
N_LOCAL = 2
WINDOW = 512
Q_BLOCK = 128
EPS = 1e-6
NEG = -1e30
SCALE = HD ** -0.5

LANES = 128
N_PACK = 3456
VMEM_LIMIT = 56 * 1024 * 1024


def _dot(a, b):
    return jnp.dot(a, b, preferred_element_type=F32)


def _dot_nt(a, b):
    return lax.dot_general(a, b, (((1,), (1,)), ((), ())), preferred_element_type=F32)


def _group_mean_sq(t, g_ref):
    t2 = t * t
    hi = t2.astype(BF16)
    lo = (t2 - hi.astype(F32)).astype(BF16)
    g = g_ref[...]
    return (_dot(hi, g) + _dot(lo, g)) * (1.0 / HD)


def _silu(z):
    return z * jax.nn.sigmoid(z)


def _key_feats(pos):
    lane = lax.broadcasted_iota(jnp.int32, (pos.shape[0], LANES), 1)
    sh = SEL_BLK.bit_length() - 1
    blk = (jnp.right_shift(pos, sh) * SEL_BLK).astype(F32)
    off = jnp.bitwise_and(pos, SEL_BLK - 1).astype(F32)
    return jnp.where(lane == HD, blk, jnp.where(lane == HD + 1, off,
                     jnp.where(jnp.logical_or(lane == HD + 2, lane == HD + 3), 1.0, 0.0)))


def _key_aug(k, pos):
    lane = lax.broadcasted_iota(jnp.int32, k.shape, 1)
    feats = _key_feats(pos)
    g0 = jnp.where(lane < HD, k, feats)
    g1 = jnp.where(lane < HD, pltpu.roll(k, HD, 1), feats)
    return jnp.concatenate([g0, g1], axis=1).astype(BF16)


def _proj_values(x_ref, nin_ref, w_ref, bg_ref, lng_ref, lnb_ref, wc_ref, bc_ref, qn_ref, ksn_ref,
                 kwn_ref, ona_ref, g512_ref, g128_ref):
    tn = x_ref.shape[0]
    x = x_ref[...]
    ms = jnp.mean(x * x, axis=-1, keepdims=True)
    xb = (x * lax.rsqrt(ms + EPS) * nin_ref[...]).astype(BF16)

    def proj(lo, n):
        return _dot(xb, w_ref[:, lo:lo + n])

    u = proj(0, D_A)
    v = proj(D_A, D_A)
    mu = jnp.mean(v, axis=-1, keepdims=True)
    vc = v - mu
    var = jnp.mean(vc * vc, axis=-1, keepdims=True)
    vn = vc * lax.rsqrt(var + EPS) * lng_ref[...] + lnb_ref[...]
    vnb = vn.astype(BF16)
    lane = lax.broadcasted_iota(jnp.int32, (CHUNK, LANES), 1)
    rows = []
    for c in range(tn // CHUNK):
        vcb = vnb[c * CHUNK:(c + 1) * CHUNK]
        cols = []
        for m in range(D_A // LANES):
            va = vcb[:, m * LANES:(m + 1) * LANES]
            sa = _dot(wc_ref[2 * m], va)
            sb = _dot(wc_ref[2 * m + 1], va)
            cols.append(jnp.where(lane < HD, sa, sb))
        rows.append(jnp.concatenate(cols, axis=1) + bc_ref[...])
    s = jnp.concatenate(rows, axis=0)
    o = u * s
    o = o * lax.rsqrt(jnp.mean(o * o, axis=-1, keepdims=True) + EPS) * ona_ref[...]
    za = proj(2 * D_A, D_A)
    oa = (o * _silu(za)).astype(BF16)

    q = proj(3 * D_A, D_B)
    q = q * lax.rsqrt(_group_mean_sq(q, g512_ref) + EPS) * qn_ref[...]
    szb = _silu(proj(3 * D_A + D_B, D_B)).astype(BF16)
    base = 3 * D_A + 2 * D_B
    kc = proj(base, KV_W)
    vc = proj(base + KV_W, KV_W)
    ks = proj(base + 2 * KV_W, KV_W)
    ks = ks * lax.rsqrt(_group_mean_sq(ks, g128_ref) + EPS) * ksn_ref[...]
    vs = proj(base + 3 * KV_W, KV_W)
    kw = proj(base + 4 * KV_W, KV_W)
    kw = kw * lax.rsqrt(_group_mean_sq(kw, g128_ref) + EPS) * kwn_ref[...]
    vw = proj(base + 5 * KV_W, KV_W)
    gates = jax.nn.sigmoid(proj(base + 6 * KV_W, LANES) + bg_ref[...])
    return oa, vn, q, szb, gates, kc, vc, ks, vs, kw, vw


N_PROJ_CONSTS = 13


def _proj_sample_kernel(*refs):
    ins, outs = refs[:1 + N_PROJ_CONSTS], refs[1 + N_PROJ_CONSTS:]
    for ref, val in zip(outs, _proj_values(*ins)):
        ref[...] = val


def _proj_prompt_kernel(*refs, seq):
    ins = refs[:1 + N_PROJ_CONSTS]
    (oa_ref, szb_ref, qt_ref, gt_ref, ksa_ref, kwa_ref, kcr_ref, vcr_ref,
     kct_ref, vct_ref, kst_ref, vst_ref, kwt_ref, vwt_ref) = refs[1 + N_PROJ_CONSTS:]
    oa, _, q, szb, gates, kc, vc, ks, vs, kw, vw = _proj_values(*ins)
    tn = oa.shape[0]
    t0 = lax.rem(pl.program_id(0) * tn, seq)
    oa_ref[...] = oa
    szb_ref[...] = szb
    kcr_ref[...] = kc
    vcr_ref[...] = vc
    for ref, val in ((kct_ref, kc), (vct_ref, vc), (kst_ref, ks), (vst_ref, vs), (kwt_ref, kw),
                     (vwt_ref, vw)):
        ref[0] = val.T
    gt_ref[0] = gates.T[:3 * H_B]
    pos_rows = t0 + lax.broadcasted_iota(jnp.int32, (tn, 1), 0)
    ksa_ref[0] = _key_aug(ks, pos_rows)
    kwa_ref[0] = _key_aug(kw, pos_rows)
    pos = t0 + lax.broadcasted_iota(jnp.int32, (HD, tn), 1)
    row = lax.broadcasted_iota(jnp.int32, (HD, tn), 0)
    sh = SEL_BLK.bit_length() - 1
    blk = (jnp.right_shift(pos, sh) * SEL_BLK).astype(F32)
    off = jnp.bitwise_and(pos, SEL_BLK - 1).astype(F32)
    qs = q * SCALE
    for m in range(D_B // LANES):
        qt = qs[:, m * LANES:(m + 1) * LANES].T
        for i in range(LANES // HD):
            h = (LANES // HD) * m + i
            slope = 2.0 ** -(h + 1)
            feats = jnp.where(row < 2, slope, jnp.where(row == 2, -slope * blk,
                              jnp.where(row == 3, -slope * off, 0.0)))
            qt_ref[0, h, :HD, :] = qt[i * HD:(i + 1) * HD].astype(BF16)
            qt_ref[0, h, HD:, :] = feats.astype(BF16)


def _row_tile(n):
    sizes = [tn for tn in (1024, 512, 256, 128) if n % tn == 0]
    for tn in sizes:
        if n // tn >= 2:
            return tn
    if sizes:
        return sizes[-1]
    raise ValueError(f"row count {n} must be a multiple of {CHUNK}")


def _proj_consts(wts):
    return [wts["nin"], wts["w_in"], wts["bg"], wts["lng"], wts["lnb"], wts["wc"], wts["bc"],
            wts["qn"], wts["ksn"], wts["kwn"], wts["ona"], wts["g512"], wts["g128"]]


def _projection_sample(x2d, wts):
    n = x2d.shape[0]
    tn = _row_tile(n)
    full = lambda a: pl.BlockSpec(a.shape, lambda i: (0,) * a.ndim)
    consts = _proj_consts(wts)
    row = lambda w: pl.BlockSpec((tn, w), lambda i: (i, 0))
    out_w = [(D_A, BF16), (D_A, F32), (D_B, F32), (D_B, BF16), (LANES, F32)] + [(KV_W, F32)] * 6
    return pl.pallas_call(
        _proj_sample_kernel,
        grid=(n // tn,),
        in_specs=[row(D_MODEL)] + [full(a) for a in consts],
        out_specs=[row(w) for w, _ in out_w],
        out_shape=[jax.ShapeDtypeStruct((n, w), dt) for w, dt in out_w],
        compiler_params=pltpu.CompilerParams(dimension_semantics=("arbitrary",),
                                             vmem_limit_bytes=VMEM_LIMIT),
        name="projection_sample",
    )(x2d, *consts)


def _projection_prompt(x2d, b, seq, wts):
    n = x2d.shape[0]
    tn = _row_tile(seq)
    tps = seq // tn
    full = lambda a: pl.BlockSpec(a.shape, lambda i: (0,) * a.ndim)
    consts = _proj_consts(wts)
    row = lambda w: pl.BlockSpec((tn, w), lambda i: (i, 0))
    minor = lambda r: pl.BlockSpec((1, r, tn), lambda i: (i // tps, 0, i % tps))
    specs = [(row(D_A), (n, D_A), BF16), (row(D_B), (n, D_B), BF16),
             (pl.BlockSpec((1, H_B, LANES, tn), lambda i: (i // tps, 0, 0, i % tps)),
              (b, H_B, LANES, seq), BF16),
             (minor(3 * H_B), (b, 3 * H_B, seq), F32)]
    specs += [(pl.BlockSpec((1, tn, 2 * KV_W), lambda i: (i // tps, i % tps, 0)),
               (b, seq, 2 * KV_W), BF16)] * 2
    specs += [(row(KV_W), (n, KV_W), F32)] * 2
    specs += [(minor(KV_W), (b, KV_W, seq), F32)] * 6
    return pl.pallas_call(
        functools.partial(_proj_prompt_kernel, seq=seq),
        grid=(n // tn,),
        in_specs=[row(D_MODEL)] + [full(a) for a in consts],
        out_specs=[s for s, _, _ in specs],
        out_shape=[jax.ShapeDtypeStruct(shape, dt) for _, shape, dt in specs],
        compiler_params=pltpu.CompilerParams(dimension_semantics=("arbitrary",),
                                             vmem_limit_bytes=VMEM_LIMIT),
        name="projection_prompt",
    )(x2d, *consts)


def _merge_kernel(x_ref, oa_ref, ob_ref, szb_ref, onb_ref, w_ref, y_ref):
    ob = ob_ref[...]
    ob = ob * lax.rsqrt(jnp.mean(ob * ob, axis=-1, keepdims=True) + EPS) * onb_ref[...]
    ob = (ob * szb_ref[...].astype(F32)).astype(BF16)
    y_ref[...] = x_ref[...] + _dot(oa_ref[...], w_ref[:D_A, :]) + _dot(ob, w_ref[D_A:, :])


def _merge(x2d, oa, ob, szb, wts):
    n = x2d.shape[0]
    tn = _row_tile(n)
    row = lambda w: pl.BlockSpec((tn, w), lambda i: (i, 0))
    full = lambda a: pl.BlockSpec(a.shape, lambda i: (0,) * a.ndim)
    return pl.pallas_call(
        _merge_kernel,
        grid=(n // tn,),
        in_specs=[row(D_MODEL), row(D_A), row(D_B), row(D_B), full(wts["onb"]), full(wts["w_out"])],
        out_specs=row(D_MODEL),
        out_shape=jax.ShapeDtypeStruct((n, D_MODEL), F32),
        compiler_params=pltpu.CompilerParams(dimension_semantics=("arbitrary",),
                                             vmem_limit_bytes=VMEM_LIMIT),
        name="merge",
    )(x2d, oa, ob, szb, wts["onb"], wts["w_out"])


def _compress_math(xb, w1_ref, pea_ref, peb_ref, w2_ref):
    h1 = _dot(xb, w1_ref[0])
    h2 = _dot(xb, w1_ref[1])
    r = h2.shape[0]
    h2n = pltpu.roll(h2, r - 1, 0)
    pe = _dot(pea_ref[...], w1_ref[0]) + _dot(peb_ref[...], w1_ref[1])
    pre = h1 + h2n + pe[0:1]
    return _dot(_silu(pre).astype(BF16), w2_ref[...])


def _compress_kernel(rk_ref, rv_ref, w1k_ref, w1v_ref, peak_ref, pebk_ref, peav_ref, pebv_ref,
                     w2k_ref, w2v_ref, kn_ref, g128_ref, kc_ref, vc_ref):
    kc = _compress_math(rk_ref[0].astype(BF16), w1k_ref, peak_ref, pebk_ref, w2k_ref)
    kc = kc * lax.rsqrt(_group_mean_sq(kc, g128_ref) + EPS) * kn_ref[...]
    end = CMP_STRIDE * lax.broadcasted_iota(jnp.int32, (kc.shape[0], 1), 0) + (CMP_BLK - 1)
    kc_ref[0] = _key_aug(kc, end)
    vc = _compress_math(rv_ref[0].astype(BF16), w1v_ref, peav_ref, pebv_ref, w2v_ref)
    vc_ref[0] = vc.T.astype(BF16)


def _cmp_consts(wts):
    return [wts["w1k"], wts["w1v"], wts["peak"], wts["pebk"], wts["peav"], wts["pebv"],
            wts["w2k"], wts["w2v"], wts["kcn"], wts["g128"]]


def _compress_prompt(rk, rv, wts):
    b, r, w = rk.shape
    consts = _cmp_consts(wts)
    full = lambda a: pl.BlockSpec(a.shape, lambda i: (0,) * a.ndim)
    blk = pl.BlockSpec((1, r, w), lambda i: (i, 0, 0))
    return pl.pallas_call(
        _compress_kernel,
        grid=(b,),
        in_specs=[blk, blk] + [full(a) for a in consts],
        out_specs=[pl.BlockSpec((1, r, 2 * KV_W), lambda i: (i, 0, 0)),
                   pl.BlockSpec((1, KV_W, r), lambda i: (i, 0, 0))],
        out_shape=[jax.ShapeDtypeStruct((b, r, 2 * KV_W), BF16),
                   jax.ShapeDtypeStruct((b, KV_W, r), BF16)],
        compiler_params=pltpu.CompilerParams(dimension_semantics=("arbitrary",),
                                             vmem_limit_bytes=VMEM_LIMIT),
        name="compress_prompt",
    )(rk, rv, *consts)


def _topk_rows(score, k):
    n = score.shape[0]
    ridx = lax.broadcasted_iota(jnp.int32, score.shape, 0).astype(F32)
    sel = jnp.zeros(score.shape, jnp.bool_)
    for _ in range(k):
        m = jnp.max(score, axis=0, keepdims=True)
        idx = jnp.min(jnp.where(score == m, ridx, float(n)), axis=0, keepdims=True)
        hit = ridx == idx
        sel = jnp.logical_or(sel, hit)
        score = jnp.where(hit, -jnp.inf, score)
    return sel


SEL_GROUP = 4


def _online_update(carry, s, vt):
    m, l, acc = carry
    m_new = jnp.maximum(m, jnp.max(s, axis=0, keepdims=True))
    alpha = jnp.exp(m - m_new)
    p = jnp.exp(s - m_new)
    l = alpha * l + jnp.sum(p, axis=0, keepdims=True)
    acc = alpha * acc + _dot(vt, p.astype(BF16))
    return m_new, l, acc


def _pattn_kernel(qt_ref, gt_ref, ks_ref, vst_ref, kw_ref, vwt_ref, kc_ref, vct_ref, cmask_ref,
                  dmask_ref, wmask_ref, o_ref, imp_sc, bias_sc):
    t = pl.program_id(1)
    n_sb = bias_sc.shape[1]
    pair = dmask_ref.shape[1]
    q0 = t * Q_BLOCK
    qpos = q0 + lax.broadcasted_iota(jnp.int32, (1, Q_BLOCK), 1)
    gls = [slice(g * LANES, (g + 1) * LANES) for g in range(KV_B)]
    gds = [slice(g * HD, (g + 1) * HD) for g in range(KV_B)]
    qts = [qt_ref[0, h] for h in range(H_B)]

    cmask = cmask_ref[0]
    kcs = [kc_ref[0, :, gls[g]] for g in range(KV_B)]
    vcts = [vct_ref[0, gds[g], :] for g in range(KV_B)]

    def cmp_softmax(h, sc):
        pc = jnp.exp(sc - jnp.maximum(jnp.max(sc, axis=0, keepdims=True), 0.1 * NEG))
        lc = jnp.sum(pc, axis=0, keepdims=True)
        pc = pc * (1.0 / jnp.where(lc > 0.0, lc, 1.0))
        return _dot(vcts[h // G_B], pc.astype(BF16)), pc

    qtg = [jnp.concatenate(qts[G_B * g:G_B * (g + 1)], axis=1) for g in range(KV_B)]
    head = lambda wide, h: wide[h // G_B][:, (h % G_B) * Q_BLOCK:(h % G_B + 1) * Q_BLOCK]
    s_c = [_dot(kcs[g], qtg[g]) for g in range(KV_B)]
    cmp_out = [cmp_softmax(h, head(s_c, h) + cmask) for h in range(H_B)]
    o_c = [o for o, _ in cmp_out]
    for g in range(KV_B):
        imp_sc[g] = sum(pc for _, pc in cmp_out[G_B * g:G_B * (g + 1)])
        ratio = SEL_BLK // CMP_STRIDE
        imp = sum(imp_sc[g, pl.ds(r, n_sb, stride=ratio), :] for r in range(ratio))
        bi = lax.broadcasted_iota(jnp.int32, (n_sb, Q_BLOCK), 0)
        cur = jnp.right_shift(qpos, SEL_BLK.bit_length() - 1)
        forced = jnp.logical_or(bi == 0, bi > cur - N_LOCAL)
        score = jnp.where(bi > cur, -1e9, jnp.where(forced, 1e9, imp))
        bias_sc[g] = jnp.where(_topk_rows(score, min(N_SEL, n_sb)), 0.0, NEG)

    span = wmask_ref.shape[1]
    w0 = pl.multiple_of(jnp.maximum(q0 - WINDOW, 0), Q_BLOCK)
    wmask = wmask_ref[0]
    kws = [kw_ref[0, pl.ds(w0, span), gls[g]] for g in range(KV_B)]
    vwts = [vwt_ref[0, gds[g], pl.ds(w0, span)].astype(BF16) for g in range(KV_B)]

    def win_softmax(h, s):
        p = jnp.exp(s - jnp.max(s, axis=0, keepdims=True))
        l = jnp.sum(p, axis=0, keepdims=True)
        return _dot(vwts[h // G_B], p.astype(BF16)) * (1.0 / l)

    s_w = [_dot(kws[g], qtg[g]) for g in range(KV_B)]
    o_w = [win_softmax(h, head(s_w, h) + wmask) for h in range(H_B)]

    last = t // 2

    def sel_pairs(ps, carry):
        staged = []
        for p in ps:
            k0 = pl.multiple_of(p * pair, pair)
            diag = jnp.where(p == last, dmask_ref[0], 0.0)
            bias, vt, wide = [], [], []
            for g in range(KV_B):
                rows = [jnp.broadcast_to(bias_sc[g, pl.ds((pair // SEL_BLK) * p + i, 1), :],
                                         (SEL_BLK, Q_BLOCK)) for i in range(pair // SEL_BLK)]
                bias.append(jnp.concatenate(rows, axis=0) + diag)
                vt.append(vst_ref[0, gds[g], pl.ds(k0, pair)].astype(BF16))
                wide.append(_dot(ks_ref[0, pl.ds(k0, pair), gls[g]], qtg[g]))
            staged.append((bias, vt, wide))
        for bias, vt, wide in staged:
            carry = tuple(_online_update(carry[h], head(wide, h) + bias[h // G_B], vt[h // G_B])
                          for h in range(H_B))
        return carry

    init = (jnp.full((1, Q_BLOCK), NEG, F32), jnp.zeros((1, Q_BLOCK), F32), jnp.zeros((HD, Q_BLOCK), F32))
    n_pairs = last + 1
    group = lambda base, n: (lambda c: sel_pairs([base + i for i in range(n)], c))
    carry = lax.fori_loop(0, n_pairs // SEL_GROUP, lambda i, c: group(SEL_GROUP * i, SEL_GROUP)(c),
                          (init,) * H_B)
    done = SEL_GROUP * (n_pairs // SEL_GROUP)
    n = SEL_GROUP // 2
    while n >= 1:
        take = jnp.bitwise_and(n_pairs, n) != 0
        carry = lax.cond(take, group(done, n), lambda c: c, carry)
        done = done + jnp.where(take, n, 0)
        n //= 2
    o_s = [a * (1.0 / l) for _, l, a in carry]

    for m in range(H_B // 2):
        halves = [gt_ref[0, h:h + 1, :] * o_c[h] + gt_ref[0, H_B + h:H_B + h + 1, :] * o_s[h]
                  + gt_ref[0, 2 * H_B + h:2 * H_B + h + 1, :] * o_w[h] for h in (2 * m, 2 * m + 1)]
        o_ref[:, m * LANES:(m + 1) * LANES] = jnp.concatenate(halves, axis=0).T


def _prompt_attention(qt, gt, ksa, vst, kwa, vwt, kca, vct):
    b, _, _, t = qt.shape
    nb = kca.shape[1]
    n_sb = t // SEL_BLK
    nt = t // Q_BLOCK
    pair, span, n_wt = 2 * Q_BLOCK, WINDOW + Q_BLOCK, WINDOW // Q_BLOCK
    assert t % pair == 0 and t >= span
    q = np.arange(Q_BLOCK)[None, None, :]
    bias = lambda ok: jnp.asarray(np.where(ok, 0.0, NEG), F32)
    ci = np.arange(nb)[None, :, None]
    tq = Q_BLOCK * np.arange(nt)[:, None, None] + q
    cmask = bias((tq >= CMP_STRIDE * ci + CMP_BLK - 1) & (ci < nb - 1))
    r = np.arange(pair)[None, :, None]
    dmask = bias(r <= Q_BLOCK * np.arange(2)[:, None, None] + q)
    r = np.arange(span)[None, :, None]
    dist = Q_BLOCK * np.arange(n_wt + 1)[:, None, None] + q - r
    wmask = bias((dist >= 0) & (dist < WINDOW))
    per_b = lambda a: pl.BlockSpec((1,) + a.shape[1:], lambda i, j: (i,) + (0,) * (a.ndim - 1))
    return pl.pallas_call(
        _pattn_kernel,
        grid=(b, nt),
        in_specs=[pl.BlockSpec((1, H_B, LANES, Q_BLOCK), lambda i, j: (i, 0, 0, j)),
                  pl.BlockSpec((1, 3 * H_B, Q_BLOCK), lambda i, j: (i, 0, j)),
                  per_b(ksa), per_b(vst), per_b(kwa), per_b(vwt), per_b(kca), per_b(vct),
                  pl.BlockSpec((1, nb, Q_BLOCK), lambda i, j: (j, 0, 0)),
                  pl.BlockSpec((1, pair, Q_BLOCK), lambda i, j: (lax.rem(j, 2), 0, 0)),
                  pl.BlockSpec((1, span, Q_BLOCK), lambda i, j: (jnp.minimum(j, n_wt), 0, 0))],
        out_specs=pl.BlockSpec((Q_BLOCK, D_B), lambda i, j: (i * (t // Q_BLOCK) + j, 0)),
        out_shape=jax.ShapeDtypeStruct((b * t, D_B), F32),
        scratch_shapes=[pltpu.VMEM((KV_B, nb, Q_BLOCK), F32), pltpu.VMEM((KV_B, n_sb, Q_BLOCK), F32)],
        compiler_params=pltpu.CompilerParams(dimension_semantics=("arbitrary", "arbitrary"),
                                             vmem_limit_bytes=VMEM_LIMIT),
        name="prompt_attention",
    )(qt, gt, ksa, vst, kwa, vwt, kca, vct, cmask, dmask, wmask)


def _page_copy(hbm, buf, sem, page, slot, which, p):
    return pltpu.make_async_copy(hbm.at[page], buf.at[slot, which, p], sem.at[slot, which])


def _gather_pages(pt_ref, hbm_a, hbm_b, buf, sem):
    b = pl.program_id(0)
    nb = pl.num_programs(0)
    n_pages = pt_ref.shape[1]
    slot = lax.rem(b, 2)

    def issue(bb, sl):
        for p in range(n_pages):
            page = pt_ref[bb, p]
            _page_copy(hbm_a, buf, sem, page, sl, 0, p).start()
            _page_copy(hbm_b, buf, sem, page, sl, 1, p).start()

    @pl.when(b == 0)
    def _():
        issue(0, 0)

    @pl.when(b + 1 < nb)
    def _():
        issue(b + 1, 1 - slot)

    for p in range(n_pages):
        _page_copy(hbm_a, buf, sem, 0, slot, 0, p).wait()
        _page_copy(hbm_b, buf, sem, 0, slot, 1, p).wait()
    return slot


SUBLANES = 8


def _sublane_transpose(tiles):
    tiles = list(tiles)
    sub = lax.broadcasted_iota(jnp.int32, tiles[0].shape, 0)
    k = 1
    while k < SUBLANES:
        hi = jnp.bitwise_and(sub, k) != 0
        for c in range(SUBLANES):
            if c & k:
                continue
            a, b = tiles[c], tiles[c | k]
            tiles[c] = jnp.where(hi, pltpu.roll(b, k, 0), a)
            tiles[c | k] = jnp.where(hi, b, pltpu.roll(a, SUBLANES - k, 0))
        k *= 2
    return tiles


def _chunk_rows(buf, slot, which, lhs_sc):
    n_pages, _, page = buf.shape[2:]
    cpp = page // CMP_STRIDE
    assert cpp == SUBLANES and CMP_STRIDE % SUBLANES == 0
    for pp in range(n_pages // 2):
        per_j = []
        for e in range(2):
            x = buf[slot, which, 2 * pp + e].T
            cols = [None] * CMP_STRIDE
            for h in range(CMP_STRIDE // SUBLANES):
                tiles = [x[CMP_STRIDE * c + SUBLANES * h:CMP_STRIDE * c + SUBLANES * (h + 1)]
                         for c in range(cpp)]
                cols[SUBLANES * h:SUBLANES * (h + 1)] = _sublane_transpose(tiles)
            per_j.append(cols)
        for j in range(CMP_STRIDE):
            lhs_sc[which, 2 * cpp * pp:2 * cpp * (pp + 1), KV_W * j:KV_W * (j + 1)] = (
                jnp.concatenate([per_j[0][j], per_j[1][j]], axis=0).astype(BF16))


def _scmp_kernel(pt_ref, kc_hbm, vc_hbm, qk_ref, slope_ref, irow_ref, w1k_ref, w1v_ref, peak_ref,
                 pebk_ref, peav_ref, pebv_ref, w2k_ref, w2v_ref, kn_ref, g128_ref, gsum_ref,
                 oc_ref, imp_ref, buf, sem, lhs_sc, *, past_len):
    slot = _gather_pages(pt_ref, kc_hbm, vc_hbm, buf, sem)
    _chunk_rows(buf, slot, 0, lhs_sc)
    _chunk_rows(buf, slot, 1, lhs_sc)
    kc = _compress_math(lhs_sc[0], w1k_ref, peak_ref, pebk_ref, w2k_ref)
    kc = kc * lax.rsqrt(_group_mean_sq(kc, g128_ref) + EPS) * kn_ref[...]
    vc = _compress_math(lhs_sc[1], w1v_ref, peav_ref, pebv_ref, w2v_ref)
    nb = kc.shape[0]
    nr = qk_ref.shape[1]

    s = _dot_nt(qk_ref[0], kc.astype(BF16))
    ci = lax.broadcasted_iota(jnp.int32, (nr, nb), 1)
    end = (CMP_STRIDE * ci + (CMP_BLK - 1)).astype(F32)
    dist = (past_len + irow_ref[:, 0:1]) - end
    valid = jnp.logical_and(dist >= 0.0, ci < nb - 1)
    s = jnp.where(valid, s - slope_ref[:, 0:1] * dist, NEG)
    p = jnp.where(valid, jnp.exp(s - jnp.max(s, axis=-1, keepdims=True)), 0.0)
    l = jnp.sum(p, axis=-1, keepdims=True)
    p = p * (1.0 / jnp.where(l > 0.0, l, 1.0))
    oc_ref[0] = _dot(p.astype(BF16), vc.astype(BF16))

    tq = nr // H_B
    parts = []
    for g in range(KV_B):
        base = g * G_B * tq
        parts.append(sum(p[base + hh * tq:base + (hh + 1) * tq] for hh in range(G_B)))
    imp = jnp.concatenate(parts, axis=0)
    hi = imp.astype(BF16)
    r1 = imp - hi.astype(F32)
    mid = r1.astype(BF16)
    lo = (r1 - mid.astype(F32)).astype(BF16)
    gs = gsum_ref[...]
    imp_ref[0] = _dot(hi, gs) + _dot(mid, gs) + _dot(lo, gs)


def _sample_compressed(page_table, kc_pages, vc_pages, qk, slope, irow, wts, past_len):
    b, n_pages = page_table.shape
    page = kc_pages.shape[2]
    cpp = page // CMP_STRIDE
    nb = n_pages * cpp
    nr = qk.shape[1]
    ratio = SEL_BLK // CMP_STRIDE
    n_blk = -(-(nb // ratio + 1) // LANES) * LANES
    gsum = jnp.asarray(np.arange(nb)[:, None] // ratio == np.arange(n_blk)[None, :], BF16)
    consts = [slope, irow] + _cmp_consts(wts) + [gsum]
    full = lambda a: pl.BlockSpec(a.shape, lambda i, pt: (0,) * a.ndim)
    row = lambda i, pt: (i, 0, 0)
    grid_spec = pltpu.PrefetchScalarGridSpec(
        num_scalar_prefetch=1,
        grid=(b,),
        in_specs=[pl.BlockSpec(memory_space=pl.ANY), pl.BlockSpec(memory_space=pl.ANY),
                  pl.BlockSpec((1, nr, LANES), row)] + [full(a) for a in consts],
        out_specs=[pl.BlockSpec((1, nr, KV_W), row), pl.BlockSpec((1, nr // G_B, n_blk), row)],
        scratch_shapes=[pltpu.VMEM((2, 2, n_pages, KV_W, page), F32),
                        pltpu.SemaphoreType.DMA((2, 2)),
                        pltpu.VMEM((2, nb, CMP_STRIDE * KV_W), BF16)],
    )
    return pl.pallas_call(
        functools.partial(_scmp_kernel, past_len=float(past_len)),
        grid_spec=grid_spec,
        out_shape=[jax.ShapeDtypeStruct((b, nr, KV_W), F32),
                   jax.ShapeDtypeStruct((b, nr // G_B, n_blk), F32)],
        compiler_params=pltpu.CompilerParams(dimension_semantics=("arbitrary",),
                                             vmem_limit_bytes=VMEM_LIMIT),
        name="sample_compressed",
    )(page_table, kc_pages, vc_pages, qk, *consts)


def _topk_lanes_kernel(imp_ref, out_ref, *, n_blocks, cur, k):
    imp = imp_ref[...]
    bi = lax.broadcasted_iota(jnp.int32, imp.shape, 1).astype(F32)
    forced = jnp.logical_or(bi == 0, bi > cur - N_LOCAL)
    score = jnp.where(bi > cur, -1e9, jnp.where(forced, 1e9, imp))
    score = jnp.where(bi < n_blocks, score, -jnp.inf)
    width = imp.shape[1]
    sel = jnp.zeros(imp.shape, jnp.bool_)
    for _ in range(k):
        m = jnp.max(score, axis=-1, keepdims=True)
        idx = jnp.min(jnp.where(score == m, bi, float(width)), axis=-1, keepdims=True)
        hit = bi == idx
        sel = jnp.logical_or(sel, hit)
        score = jnp.where(hit, -jnp.inf, score)
    out_ref[...] = jnp.where(sel, 0.0, NEG)


def _sample_topk(imp2d, n_blocks, cur):
    n, w = imp2d.shape
    tr = 256 if n % 256 == 0 else n
    blk = pl.BlockSpec((tr, w), lambda i: (i, 0))
    return pl.pallas_call(
        functools.partial(_topk_lanes_kernel, n_blocks=n_blocks, cur=cur, k=min(N_SEL, n_blocks)),
        grid=(n // tr,),
        in_specs=[blk],
        out_specs=blk,
        out_shape=jax.ShapeDtypeStruct((n, w), F32),
        compiler_params=pltpu.CompilerParams(dimension_semantics=("arbitrary",)),
        name="sample_topk",
    )(imp2d)


def _shift_in(st_ref, tail_ref, out_ref, n_new):
    x = st_ref[0]
    n_buf = x.shape[1]
    rolled = pltpu.roll(x, n_buf - n_new, 1)
    lane = lax.broadcasted_iota(jnp.int32, (x.shape[0], LANES), 1)
    last = jnp.where(lane >= LANES - n_new, tail_ref[0], rolled[:, n_buf - LANES:])
    out_ref[0] = jnp.concatenate([rolled[:, :n_buf - LANES], last], axis=1)


def _ssel_kernel(pt_ref, ks_hbm, vs_hbm, qsel_ref, knewt_ref, vnew_ref, kwst_ref, vwst_ref, kwnewt_ref,
                 vwnew_ref, kwtail_ref, vwtail_ref, oc_ref, gsel_ref, slope_ref, irow_ref,
                 o_ref, kwout_ref, vwout_ref, buf, sem, s_sc, oh_sc, *, past_len, n_new):
    b = pl.program_id(0)
    _shift_in(kwst_ref, kwtail_ref, kwout_ref, n_new)
    _shift_in(vwst_ref, vwtail_ref, vwout_ref, n_new)
    n_pages, _, page = buf.shape[2:]
    nr = qsel_ref.shape[1]
    ppt = next(c for c in (4, 2, 1) if n_pages % c == 0)
    tile = ppt * page

    @pl.when(b == 0)
    def _():
        bi = lax.broadcasted_iota(jnp.int32, oh_sc.shape, 0)
        ki = lax.broadcasted_iota(jnp.int32, oh_sc.shape, 1)
        oh_sc[...] = jnp.where(jnp.right_shift(ki, SEL_BLK.bit_length() - 1) == bi, 1.0, 0.0).astype(BF16)

    slot = _gather_pages(pt_ref, ks_hbm, vs_hbm, buf, sem)
    slope = slope_ref[:, 0:1]
    irow = irow_ref[:, 0:1]
    qpos = past_len + irow
    qall = qsel_ref[0]
    qk = qall[:, :KV_W]

    def key_tile(which, j):
        return jnp.concatenate([buf[slot, which, ppt * j + i] for i in range(ppt)], axis=1).astype(BF16)

    lane = lax.broadcasted_iota(jnp.int32, (nr, tile), 1).astype(F32)
    mrun = jnp.full((nr, tile), NEG, F32)
    for j in range(n_pages // ppt):
        kcat = jnp.concatenate([key_tile(0, j), oh_sc[:, j * tile:(j + 1) * tile]], axis=0)
        s = _dot(qall, kcat) - slope * (qpos - (float(j * tile) + lane))
        s_sc[:, j * tile:(j + 1) * tile] = s
        mrun = jnp.maximum(mrun, s)
    nl = lax.broadcasted_iota(jnp.int32, (nr, LANES), 1).astype(F32)
    dn = irow - nl
    s_new = jnp.where(dn >= 0.0, _dot(qk, knewt_ref[0]) - slope * dn, NEG)
    m = jnp.maximum(jnp.max(mrun, axis=-1, keepdims=True), jnp.max(s_new, axis=-1, keepdims=True))

    p_new = jnp.exp(s_new - m)
    acc = _dot(p_new.astype(BF16), vnew_ref[0])
    lrun = jnp.zeros((nr, tile), F32)
    for j in range(n_pages // ppt):
        p = jnp.exp(s_sc[:, j * tile:(j + 1) * tile] - m)
        acc = acc + _dot_nt(p.astype(BF16), key_tile(1, j))
        lrun = lrun + p
    l = jnp.sum(lrun, axis=-1, keepdims=True) + jnp.sum(p_new, axis=-1, keepdims=True)
    o_s = acc * (1.0 / l)

    n_buf = kwst_ref.shape[2]
    wl = lax.broadcasted_iota(jnp.int32, (nr, n_buf), 1).astype(F32)
    dw = (n_buf + irow) - wl
    s_w = jnp.where(jnp.logical_and(dw >= 0.0, dw < WINDOW),
                    _dot(qk, kwst_ref[0].astype(BF16)) - slope * dw, NEG)
    s_wn = jnp.where(dn >= 0.0, _dot(qk, kwnewt_ref[0]) - slope * dn, NEG)
    m_w = jnp.maximum(jnp.max(s_w, axis=-1, keepdims=True), jnp.max(s_wn, axis=-1, keepdims=True))
    p_w = jnp.exp(s_w - m_w)
    p_wn = jnp.exp(s_wn - m_w)
    l_w = jnp.sum(p_w, axis=-1, keepdims=True) + jnp.sum(p_wn, axis=-1, keepdims=True)
    o_w = (_dot_nt(p_w.astype(BF16), vwst_ref[0].astype(BF16))
           + _dot(p_wn.astype(BF16), vwnew_ref[0])) * (1.0 / l_w)

    gsel = gsel_ref[0]
    o_ref[0] = gsel[:, 0:1] * oc_ref[0] + gsel[:, 1:2] * o_s + gsel[:, 2:3] * o_w


def _sample_selected(page_table, ks_pages, vs_pages, qsel, knewt, vnew, kwst, vwst, kwnewt, vwnew, kwtail,
                     vwtail, oc, gsel, slope, irow, past_len, n_new):
    b, n_pages = page_table.shape
    page = ks_pages.shape[2]
    n_keys = n_pages * page
    nr = qsel.shape[1]
    assert kwst.shape[2] % LANES == 0 and n_new <= LANES
    per_b = lambda a: pl.BlockSpec((1,) + a.shape[1:], lambda i, pt: (i,) + (0,) * (a.ndim - 1))
    full = lambda a: pl.BlockSpec(a.shape, lambda i, pt: (0,) * a.ndim)
    grid_spec = pltpu.PrefetchScalarGridSpec(
        num_scalar_prefetch=1,
        grid=(b,),
        in_specs=[pl.BlockSpec(memory_space=pl.ANY), pl.BlockSpec(memory_space=pl.ANY)]
                 + [per_b(a) for a in (qsel, knewt, vnew, kwst, vwst, kwnewt, vwnew, kwtail, vwtail, oc, gsel)]
                 + [full(slope), full(irow)],
        out_specs=[pl.BlockSpec((1, nr, KV_W), lambda i, pt: (i, 0, 0)), per_b(kwst), per_b(vwst)],
        scratch_shapes=[pltpu.VMEM((2, 2, n_pages, KV_W, page), F32),
                        pltpu.SemaphoreType.DMA((2, 2)),
                        pltpu.VMEM((nr, n_keys), F32),
                        pltpu.VMEM((n_keys // SEL_BLK, n_keys), BF16)],
    )
    return pl.pallas_call(
        functools.partial(_ssel_kernel, past_len=float(past_len), n_new=n_new),
        grid_spec=grid_spec,
        out_shape=[jax.ShapeDtypeStruct((b, nr, KV_W), F32),
                   jax.ShapeDtypeStruct(kwst.shape, F32), jax.ShapeDtypeStruct(vwst.shape, F32)],
        compiler_params=pltpu.CompilerParams(dimension_semantics=("arbitrary",),
                                             vmem_limit_bytes=VMEM_LIMIT),
        name="sample_selected",
    )(page_table, ks_pages, vs_pages, qsel, knewt, vnew, kwst, vwst, kwnewt, vwnew, kwtail, vwtail, oc,
      gsel, slope, irow)


def _slopes():
    return np.asarray([2.0 ** -(h + 1) for h in range(H_B)], np.float32)


def _block_ones(n):
    idx = np.arange(n) // HD
    return jnp.asarray(idx[:, None] == idx[None, :], BF16)


def _layer_weights(l, t_chunk, norm_in, w_in, b_gate, ln_v_g, ln_v_b, w_spatial, b_spatial, q_norm,
                   k_norm_cmp, k_norm_sel, k_norm_win, cmp_pe_k, cmp_pe_v, w_cmp_k1, w_cmp_k2,
                   w_cmp_v1, w_cmp_v2, out_norm_a, out_norm_b, w_out):
    w = w_in[l]
    n_main = 3 * D_A + 2 * D_B
    n_gate = 3 * H_B
    w_pack = jnp.concatenate([w[:, :n_main], w[:, n_main + n_gate:], w[:, n_main:n_main + n_gate],
                              jnp.zeros((D_MODEL, N_PACK - w.shape[1]), F32)], axis=1).astype(BF16)
    c = t_chunk
    rep = CHUNK // c
    tri = jnp.where(jnp.tril(jnp.ones((c, c), bool)), w_spatial[l][:, :c, :c], 0.0)
    wc = jnp.einsum("ab,hij->haibj", jnp.eye(rep, dtype=F32), tri).reshape(H_A, CHUNK, CHUNK).astype(BF16)
    bc = jnp.tile(jnp.repeat(b_spatial[l][:, :c].T, D_A // H_A, axis=1), (rep, 1))

    def big1(w1):
        w1r = w1.reshape(2, CMP_STRIDE, HD, CMP_HID)
        return jnp.einsum("ajdh,ck->ajcdkh", w1r, jnp.eye(KV_B, dtype=F32)).reshape(
            2, CMP_STRIDE * KV_W, KV_B * CMP_HID).astype(BF16)

    def big2(w2):
        return jnp.einsum("hd,kc->khcd", w2, jnp.eye(KV_B, dtype=F32)).reshape(
            KV_B * CMP_HID, KV_W).astype(BF16)

    def pe_rows(pe):
        halves = pe.reshape(2, CMP_STRIDE, 1, HD)
        flat = jnp.broadcast_to(halves, (2, CMP_STRIDE, KV_B, HD)).reshape(2, 1, CMP_STRIDE * KV_W)
        flat = jnp.broadcast_to(flat, (2, 8, CMP_STRIDE * KV_W)).astype(BF16)
        return flat[0], flat[1]

    peak, pebk = pe_rows(cmp_pe_k[l])
    peav, pebv = pe_rows(cmp_pe_v[l])
    row = lambda a, reps: jnp.tile(a, reps)[None, :]
    bg = jnp.concatenate([b_gate[l], jnp.zeros((LANES - n_gate,), F32)])[None, :]
    return dict(
        nin=norm_in[l][None, :], w_in=w_pack, bg=bg, lng=ln_v_g[l][None, :], lnb=ln_v_b[l][None, :],
        wc=wc, bc=bc, qn=row(q_norm[l], H_B), ksn=row(k_norm_sel[l], KV_B), kwn=row(k_norm_win[l], KV_B),
        kcn=row(k_norm_cmp[l], KV_B), ona=out_norm_a[l][None, :], onb=out_norm_b[l][None, :],
        g512=_block_ones(D_B), g128=_block_ones(KV_W),
        w1k=big1(w_cmp_k1[l]), w1v=big1(w_cmp_v1[l]), w2k=big2(w_cmp_k2[l]), w2v=big2(w_cmp_v2[l]),
        peak=peak, pebk=pebk, peav=peav, pebv=pebv, w_out=w_out[l].astype(BF16))


def _prompt_layer(x, wts):
    b, t, _ = x.shape
    n = b * t
    x2d = x.reshape(n, D_MODEL)
    (oa, szb, qt, gt, ksa, kwa, kc_r, vc_r,
     kct, vct, kst, vst, kwt, vwt) = _projection_prompt(x2d, b, t, wts)
    nc = t // CMP_STRIDE
    kca, vcct = _compress_prompt(kc_r.reshape(b, nc, CMP_STRIDE * KV_W),
                                 vc_r.reshape(b, nc, CMP_STRIDE * KV_W), wts)
    ob = _prompt_attention(qt, gt, ksa, vst, kwa, vwt, kca, vcct)
    y = _merge(x2d, oa, ob, szb, wts).reshape(b, t, D_MODEL)
    heads = lambda a: jnp.transpose(a.reshape(b, KV_B, HD, a.shape[2]), (0, 3, 1, 2))
    n_keep = min(WINDOW, t)
    return y, (heads(kct), heads(vct), heads(kst), heads(vst),
               heads(kwt[:, :, t - n_keep:]), heads(vwt[:, :, t - n_keep:]))


def _sample_layer(x, l, cache_k_cmp, cache_v_cmp, cache_k_sel, cache_v_sel, k_win_buf, v_win_buf,
                  page_table, wts):
    b, t, _ = x.shape
    n = b * t
    n_pool, page = cache_k_cmp.shape[1], cache_k_cmp.shape[2]
    n_pages = page_table.shape[1]
    past_len = n_pages * page
    x2d = x.reshape(n, D_MODEL)
    oa, vn, q, szb, gates, kc_r, vc_r, ks, vs, kw, vw = _projection_sample(x2d, wts)

    nr = H_B * t
    sl = _slopes()
    slope = jnp.asarray(np.broadcast_to(np.repeat(sl, t)[:, None], (nr, LANES)).copy())
    irow = jnp.asarray(np.broadcast_to(np.tile(np.arange(t, dtype=np.float32), H_B)[:, None], (nr, LANES)).copy())
    qh = jnp.transpose((q * SCALE).reshape(b, t, KV_B, G_B, HD), (0, 2, 3, 1, 4))
    qk = jnp.einsum("bghtd,gk->bghtkd", qh, jnp.eye(KV_B, dtype=F32)).reshape(b, nr, KV_W).astype(BF16)

    minor_pos = lambda a: jnp.transpose(a, (0, 2, 3, 1)).reshape(a.shape[0], KV_W, a.shape[1])
    oc, imp = _sample_compressed(page_table, minor_pos(cache_k_cmp[l]), minor_pos(cache_v_cmp[l]),
                                 qk, slope, irow, wts, past_len)

    n_sb = -(-(past_len + t) // SEL_BLK)
    n_pb = past_len // SEL_BLK
    cur = past_len // SEL_BLK
    wpad = imp.shape[2]
    bias = _sample_topk(imp.reshape(b * KV_B * t, wpad), n_sb, cur).reshape(b, KV_B, 1, t, wpad)
    bias = jnp.broadcast_to(bias[..., :n_pb], (b, KV_B, G_B, t, n_pb)).reshape(b, nr, n_pb)
    qsel = jnp.concatenate([qk, bias.astype(BF16)], axis=-1)

    pad_rows = lambda a: jnp.pad(a.reshape(b, t, KV_W).astype(BF16), ((0, 0), (0, LANES - t), (0, 0)))
    pad_cols = lambda a: jnp.transpose(pad_rows(a), (0, 2, 1))
    gsel = jnp.transpose(gates[:, :3 * H_B].reshape(b, t, 3, H_B), (0, 3, 1, 2)).reshape(b, nr, 3)
    gsel = jnp.pad(gsel, ((0, 0), (0, 0), (0, LANES - 3)))
    tail = lambda a: jnp.pad(jnp.transpose(a.reshape(b, t, KV_W), (0, 2, 1)), ((0, 0), (0, 0), (LANES - t, 0)))
    o, kw_next, vw_next = _sample_selected(
        page_table, minor_pos(cache_k_sel[l]), minor_pos(cache_v_sel[l]), qsel, pad_cols(ks), pad_rows(vs),
        minor_pos(k_win_buf), minor_pos(v_win_buf), pad_cols(kw), pad_rows(vw), tail(kw), tail(vw), oc, gsel,
        slope, irow, past_len, t)
    o5 = o.reshape(b, KV_B, G_B, t, KV_B, HD)
    ob = jnp.stack([o5[:, g, :, :, g, :] for g in range(KV_B)], axis=1)
    ob = jnp.transpose(ob, (0, 3, 1, 2, 4)).reshape(n, D_B)
    y = _merge(x2d, oa, ob, szb, wts).reshape(b, t, D_MODEL)
    heads = lambda a: a.reshape(b, t, KV_B, HD)
    major_pos = lambda a: jnp.transpose(a.reshape(b, KV_B, HD, a.shape[2]), (0, 3, 1, 2))
    return y, (heads(kc_r), heads(vc_r), heads(ks), heads(vs), major_pos(kw_next), major_pos(vw_next),
               vn.reshape(b, t, D_A))


def kernel(x_prompt, x_sample, cache_k_cmp, cache_v_cmp, cache_k_sel, cache_v_sel, state_k_win, state_v_win, page_table, norm_in, w_in, b_gate, ln_v_g, ln_v_b, w_spatial, b_spatial, q_norm, k_norm_cmp, k_norm_sel, k_norm_win, cmp_pe_k, cmp_pe_v, w_cmp_k1, w_cmp_k2, w_cmp_v1, w_cmp_v2, out_norm_a, out_norm_b, w_out):
    params = (norm_in, w_in, b_gate, ln_v_g, ln_v_b, w_spatial, b_spatial, q_norm, k_norm_cmp,
              k_norm_sel, k_norm_win, cmp_pe_k, cmp_pe_v, w_cmp_k1, w_cmp_k2, w_cmp_v1, w_cmp_v2,
              out_norm_a, out_norm_b, w_out)
    depth = w_in.shape[0]
    h_p, h_s = x_prompt, x_sample
    st_p, st_s = [], []
    for l in range(depth):
        wts_p = _layer_weights(l, min(h_p.shape[1], CHUNK), *params)
        wts_s = _layer_weights(l, min(h_s.shape[1], CHUNK), *params)
        h_p, sp = _prompt_layer(h_p, wts_p)
        h_s, ss = _sample_layer(h_s, l, cache_k_cmp, cache_v_cmp, cache_k_sel, cache_v_sel,
                                state_k_win[l], state_v_win[l], page_table, wts_s)
        st_p.append(sp)
        st_s.append(ss)
    new_p = [jnp.stack(ts) for ts in zip(*st_p)]
    new_s = [jnp.stack(ts) for ts in zip(*st_s)]
    return (h_p, h_s, new_p[0], new_p[1], new_p[2], new_p[3], new_p[4], new_p[5],
            new_s[0], new_s[1], new_s[2], new_s[3], new_s[4], new_s[5], new_s[6])
```

```python
import functools

import numpy as np
import jax
import jax.numpy as jnp
from jax import lax
from jax.experimental import pallas as pl
from jax.experimental.pallas import tpu as pltpu

F32 = jnp.float32
BF16 = jnp.bfloat16

D_MODEL = 1024
D_A = 512
H_A = 8
CHUNK = 128
D_B = 512
H_B = 8
HD = 64
KV_B = 2
G_B = 4
KV_W = KV_B * HD
CMP_STRIDE = 16
CMP_BLK = 32
CMP_HID = 128
SEL_BLK = 64
N_SEL = 16
N_LOCAL = 2
WINDOW = 512
Q_BLOCK = 128
EPS = 1e-6
NEG = -1e30
SCALE = HD ** -0.5

LANES = 128
N_PACK = 3456
VMEM_LIMIT = 56 * 1024 * 1024


def _dot(a, b):
    return jnp.dot(a, b, preferred_element_type=F32)


def _dot_nt(a, b):
    return lax.dot_general(a, b, (((1,), (1,)), ((), ())), preferred_element_type=F32)


def _group_mean_sq(t, g_ref):
    t2 = t * t
    hi = t2.astype(BF16)
    lo = (t2 - hi.astype(F32)).astype(BF16)
    g = g_ref[...]
    return (_dot(hi, g) + _dot(lo, g)) * (1.0 / HD)


def _silu(z):
    return z * jax.nn.sigmoid(z)


def _key_feats(pos):
    lane = lax.broadcasted_iota(jnp.int32, (pos.shape[0], LANES), 1)
    sh = SEL_BLK.bit_length() - 1
    blk = (jnp.right_shift(pos, sh) * SEL_BLK).astype(F32)
    off = jnp.bitwise_and(pos, SEL_BLK - 1).astype(F32)
    return jnp.where(lane == HD, blk, jnp.where(lane == HD + 1, off,
                     jnp.where(jnp.logical_or(lane == HD + 2, lane == HD + 3), 1.0, 0.0)))


def _key_aug(k, pos):
    lane = lax.broadcasted_iota(jnp.int32, k.shape, 1)
    feats = _key_feats(pos)
    g0 = jnp.where(lane < HD, k, feats)
    g1 = jnp.where(lane < HD, pltpu.roll(k, HD, 1), feats)
    return jnp.concatenate([g0, g1], axis=1).astype(BF16)


def _proj_values(x_ref, nin_ref, w_ref, bg_ref, lng_ref, lnb_ref, wc_ref, bc_ref, qn_ref, ksn_ref,
                 kwn_ref, ona_ref, g512_ref, g128_ref):
    tn = x_ref.shape[0]
    x = x_ref[...]
    ms = jnp.mean(x * x, axis=-1, keepdims=True)
    xb = (x * lax.rsqrt(ms + EPS) * nin_ref[...]).astype(BF16)

    def proj(lo, n):
        return _dot(xb, w_ref[:, lo:lo + n])

    u = proj(0, D_A)
    v = proj(D_A, D_A)
    mu = jnp.mean(v, axis=-1, keepdims=True)
    vc = v - mu
    var = jnp.mean(vc * vc, axis=-1, keepdims=True)
    vn = vc * lax.rsqrt(var + EPS) * lng_ref[...] + lnb_ref[...]
    vnb = vn.astype(BF16)
    lane = lax.broadcasted_iota(jnp.int32, (CHUNK, LANES), 1)
    rows = []
    for c in range(tn // CHUNK):
        vcb = vnb[c * CHUNK:(c + 1) * CHUNK]
        cols = []
        for m in range(D_A // LANES):
            va = vcb[:, m * LANES:(m + 1) * LANES]
            sa = _dot(wc_ref[2 * m], va)
            sb = _dot(wc_ref[2 * m + 1], va)
            cols.append(jnp.where(lane < HD, sa, sb))
        rows.append(jnp.concatenate(cols, axis=1) + bc_ref[...])
    s = jnp.concatenate(rows, axis=0)
    o = u * s
    o = o * lax.rsqrt(jnp.mean(o * o, axis=-1, keepdims=True) + EPS) * ona_ref[...]
    za = proj(2 * D_A, D_A)
    oa = (o * _silu(za)).astype(BF16)

    q = proj(3 * D_A, D_B)
    q = q * lax.rsqrt(_group_mean_sq(q, g512_ref) + EPS) * qn_ref[...]
    szb = _silu(proj(3 * D_A + D_B, D_B)).astype(BF16)
    base = 3 * D_A + 2 * D_B
    kc = proj(base, KV_W)
    vc = proj(base + KV_W, KV_W)
    ks = proj(base + 2 * KV_W, KV_W)
    ks = ks * lax.rsqrt(_group_mean_sq(ks, g128_ref) + EPS) * ksn_ref[...]
    vs = proj(base + 3 * KV_W, KV_W)
    kw = proj(base + 4 * KV_W, KV_W)
    kw = kw * lax.rsqrt(_group_mean_sq(kw, g128_ref) + EPS) * kwn_ref[...]
    vw = proj(base + 5 * KV_W, KV_W)
    gates = jax.nn.sigmoid(proj(base + 6 * KV_W, LANES) + bg_ref[...])
    return oa, vn, q, szb, gates, kc, vc, ks, vs, kw, vw


N_PROJ_CONSTS = 13


def _proj_sample_kernel(*refs):
    ins, outs = refs[:1 + N_PROJ_CONSTS], refs[1 + N_PROJ_CONSTS:]
    for ref, val in zip(outs, _proj_values(*ins)):
        ref[...] = val


def _proj_prompt_kernel(*refs, seq):
    ins = refs[:1 + N_PROJ_CONSTS]
    (oa_ref, szb_ref, qt_ref, gt_ref, ksa_ref, kwa_ref, kcr_ref, vcr_ref,
     kct_ref, vct_ref, kst_ref, vst_ref, kwt_ref, vwt_ref) = refs[1 + N_PROJ_CONSTS:]
    oa, _, q, szb, gates, kc, vc, ks, vs, kw, vw = _proj_values(*ins)
    tn = oa.shape[0]
    t0 = lax.rem(pl.program_id(0) * tn, seq)
    oa_ref[...] = oa
    szb_ref[...] = szb
    kcr_ref[...] = kc
    vcr_ref[...] = vc
    for ref, val in ((kct_ref, kc), (vct_ref, vc), (kst_ref, ks), (vst_ref, vs), (kwt_ref, kw),
                     (vwt_ref, vw)):
        ref[0] = val.T
    gt_ref[0] = gates.T[:3 * H_B]
    pos_rows = t0 + lax.broadcasted_iota(jnp.int32, (tn, 1), 0)
    ksa_ref[0] = _key_aug(ks, pos_rows)
    kwa_ref[0] = _key_aug(kw, pos_rows)
    pos = t0 + lax.broadcasted_iota(jnp.int32, (HD, tn), 1)
    row = lax.broadcasted_iota(jnp.int32, (HD, tn), 0)
    sh = SEL_BLK.bit_length() - 1
    blk = (jnp.right_shift(pos, sh) * SEL_BLK).astype(F32)
    off = jnp.bitwise_and(pos, SEL_BLK - 1).astype(F32)
    qs = q * SCALE
    for m in range(D_B // LANES):
        qt = qs[:, m * LANES:(m + 1) * LANES].T
        for i in range(LANES // HD):
            h = (LANES // HD) * m + i
            slope = 2.0 ** -(h + 1)
            feats = jnp.where(row < 2, slope, jnp.where(row == 2, -slope * blk,
                              jnp.where(row == 3, -slope * off, 0.0)))
            qt_ref[0, h, :HD, :] = qt[i * HD:(i + 1) * HD].astype(BF16)
            qt_ref[0, h, HD:, :] = feats.astype(BF16)


def _row_tile(n):
    sizes = [tn for tn in (1024, 512, 256, 128) if n % tn == 0]
    for tn in sizes:
        if n // tn >= 2:
            return tn
    if sizes:
        return sizes[-1]
    raise ValueError(f"row count {n} must be a multiple of {CHUNK}")


def _proj_consts(wts):
    return [wts["nin"], wts["w_in"], wts["bg"], wts["lng"], wts["lnb"], wts["wc"], wts["bc"],
            wts["qn"], wts["ksn"], wts["kwn"], wts["ona"], wts["g512"], wts["g128"]]


def _projection_sample(x2d, wts):
    n = x2d.shape[0]
    tn = _row_tile(n)
    full = lambda a: pl.BlockSpec(a.shape, lambda i: (0,) * a.ndim)
    consts = _proj_consts(wts)
    row = lambda w: pl.BlockSpec((tn, w), lambda i: (i, 0))
    out_w = [(D_A, BF16), (D_A, F32), (D_B, F32), (D_B, BF16), (LANES, F32)] + [(KV_W, F32)] * 6
    return pl.pallas_call(
        _proj_sample_kernel,
        grid=(n // tn,),
        in_specs=[row(D_MODEL)] + [full(a) for a in consts],
        out_specs=[row(w) for w, _ in out_w],
        out_shape=[jax.ShapeDtypeStruct((n, w), dt) for w, dt in out_w],
        compiler_params=pltpu.CompilerParams(dimension_semantics=("arbitrary",),
                                             vmem_limit_bytes=VMEM_LIMIT),
        name="projection_sample",
    )(x2d, *consts)


def _projection_prompt(x2d, b, seq, wts):
    n = x2d.shape[0]
    tn = _row_tile(seq)
    tps = seq // tn
    full = lambda a: pl.BlockSpec(a.shape, lambda i: (0,) * a.ndim)
    consts = _proj_consts(wts)
    row = lambda w: pl.BlockSpec((tn, w), lambda i: (i, 0))
    minor = lambda r: pl.BlockSpec((1, r, tn), lambda i: (i // tps, 0, i % tps))
    specs = [(row(D_A), (n, D_A), BF16), (row(D_B), (n, D_B), BF16),
             (pl.BlockSpec((1, H_B, LANES, tn), lambda i: (i // tps, 0, 0, i % tps)),
              (b, H_B, LANES, seq), BF16),
             (minor(3 * H_B), (b, 3 * H_B, seq), F32)]
    specs += [(pl.BlockSpec((1, tn, 2 * KV_W), lambda i: (i // tps, i % tps, 0)),
               (b, seq, 2 * KV_W), BF16)] * 2
    specs += [(row(KV_W), (n, KV_W), F32)] * 2
    specs += [(minor(KV_W), (b, KV_W, seq), F32)] * 6
    return pl.pallas_call(
        functools.partial(_proj_prompt_kernel, seq=seq),
        grid=(n // tn,),
        in_specs=[row(D_MODEL)] + [full(a) for a in consts],
        out_specs=[s for s, _, _ in specs],
        out_shape=[jax.ShapeDtypeStruct(shape, dt) for _, shape, dt in specs],
        compiler_params=pltpu.CompilerParams(dimension_semantics=("arbitrary",),
                                             vmem_limit_bytes=VMEM_LIMIT),
        name="projection_prompt",
    )(x2d, *consts)


def _merge_kernel(x_ref, oa_ref, ob_ref, szb_ref, onb_ref, w_ref, y_ref):
    ob = ob_ref[...]
    ob = ob * lax.rsqrt(jnp.mean(ob * ob, axis=-1, keepdims=True) + EPS) * onb_ref[...]
    ob = (ob * szb_ref[...].astype(F32)).astype(BF16)
    y_ref[...] = x_ref[...] + _dot(oa_ref[...], w_ref[:D_A, :]) + _dot(ob, w_ref[D_A:, :])


def _merge(x2d, oa, ob, szb, wts):
    n = x2d.shape[0]
    tn = _row_tile(n)
    row = lambda w: pl.BlockSpec((tn, w), lambda i: (i, 0))
    full = lambda a: pl.BlockSpec(a.shape, lambda i: (0,) * a.ndim)
    return pl.pallas_call(
        _merge_kernel,
        grid=(n // tn,),
        in_specs=[row(D_MODEL), row(D_A), row(D_B), row(D_B), full(wts["onb"]), full(wts["w_out"])],
        out_specs=row(D_MODEL),
        out_shape=jax.ShapeDtypeStruct((n, D_MODEL), F32),
        compiler_params=pltpu.CompilerParams(dimension_semantics=("arbitrary",),
                                             vmem_limit_bytes=VMEM_LIMIT),
        name="merge",
    )(x2d, oa, ob, szb, wts["onb"], wts["w_out"])


def _compress_math(xb, w1_ref, pea_ref, peb_ref, w2_ref):
    h1 = _dot(xb, w1_ref[0])
    h2 = _dot(xb, w1_ref[1])
    r = h2.shape[0]
    h2n = pltpu.roll(h2, r - 1, 0)
    pe = _dot(pea_ref[...], w1_ref[0]) + _dot(peb_ref[...], w1_ref[1])
    pre = h1 + h2n + pe[0:1]
    return _dot(_silu(pre).astype(BF16), w2_ref[...])


def _compress_kernel(rk_ref, rv_ref, w1k_ref, w1v_ref, peak_ref, pebk_ref, peav_ref, pebv_ref,
                     w2k_ref, w2v_ref, kn_ref, g128_ref, kc_ref, vc_ref):
    nc = rk_ref.shape[0] // CMP_STRIDE

    def chunk_rows(ref):
        return jnp.concatenate([ref[pl.ds(j, nc, stride=CMP_STRIDE), :] for j in range(CMP_STRIDE)],
                               axis=1).astype(BF16)

    kc = _compress_math(chunk_rows(rk_ref), w1k_ref, peak_ref, pebk_ref, w2k_ref)
    kc = kc * lax.rsqrt(_group_mean_sq(kc, g128_ref) + EPS) * kn_ref[...]
    end = CMP_STRIDE * lax.broadcasted_iota(jnp.int32, (kc.shape[0], 1), 0) + (CMP_BLK - 1)
    kc_ref[0] = _key_aug(kc, end)
    vc = _compress_math(chunk_rows(rv_ref), w1v_ref, peav_ref, pebv_ref, w2v_ref)
    vc_ref[0] = vc.T.astype(BF16)


def _cmp_consts(wts):
    return [wts["w1k"], wts["w1v"], wts["peak"], wts["pebk"], wts["peav"], wts["pebv"],
            wts["w2k"], wts["w2v"], wts["kcn"], wts["g128"]]


def _compress_prompt(rk, rv, b, wts):
    seq = rk.shape[0] // b
    r = seq // CMP_STRIDE
    consts = _cmp_consts(wts)
    full = lambda a: pl.BlockSpec(a.shape, lambda i: (0,) * a.ndim)
    blk = pl.BlockSpec((seq, KV_W), lambda i: (i, 0))
    return pl.pallas_call(
        _compress_kernel,
        grid=(b,),
        in_specs=[blk, blk] + [full(a) for a in consts],
        out_specs=[pl.BlockSpec((1, r, 2 * KV_W), lambda i: (i, 0, 0)),
                   pl.BlockSpec((1, KV_W, r), lambda i: (i, 0, 0))],
        out_shape=[jax.ShapeDtypeStruct((b, r, 2 * KV_W), BF16),
                   jax.ShapeDtypeStruct((b, KV_W, r), BF16)],
        compiler_params=pltpu.CompilerParams(dimension_semantics=("arbitrary",),
                                             vmem_limit_bytes=VMEM_LIMIT),
        name="compress_prompt",
    )(rk, rv, *consts)


def _topk_rows(score, k):
    n = score.shape[0]
    ridx = lax.broadcasted_iota(jnp.int32, score.shape, 0).astype(F32)
    sel = jnp.zeros(score.shape, jnp.bool_)
    for _ in range(k):
        m = jnp.max(score, axis=0, keepdims=True)
        idx = jnp.min(jnp.where(score == m, ridx, float(n)), axis=0, keepdims=True)
        hit = ridx == idx
        sel = jnp.logical_or(sel, hit)
        score = jnp.where(hit, -jnp.inf, score)
    return sel


Q_TILE = 256
SEL_PAIR = 256
SEL_GROUP = 4


def _online_update(carry, s, vt):
    m, l, acc = carry
    m_new = jnp.maximum(m, jnp.max(s, axis=0, keepdims=True))
    alpha = jnp.exp(m - m_new)
    p = jnp.exp(s - m_new)
    l = alpha * l + jnp.sum(p, axis=0, keepdims=True)
    acc = alpha * acc + _dot(vt, p.astype(BF16))
    return m_new, l, acc


def _pattn_kernel(qt_ref, gt_ref, ks_ref, vst_ref, kw_ref, vwt_ref, kc_ref, vct_ref, cmask_ref,
                  dmask_ref, wmask_ref, o_ref, imp_sc, bias_sc):
    t = pl.program_id(1)
    n_sb = bias_sc.shape[1]
    pair = dmask_ref.shape[1]
    qn = qt_ref.shape[3]
    q0 = t * qn
    qpos = q0 + lax.broadcasted_iota(jnp.int32, (1, qn), 1)
    gls = [slice(g * LANES, (g + 1) * LANES) for g in range(KV_B)]
    gds = [slice(g * HD, (g + 1) * HD) for g in range(KV_B)]
    qts = [qt_ref[0, h] for h in range(H_B)]

    cmask = cmask_ref[0]
    kcs = [kc_ref[0, :, gls[g]] for g in range(KV_B)]
    vcts = [vct_ref[0, gds[g], :] for g in range(KV_B)]

    def cmp_softmax(h, sc):
        pc = jnp.exp(sc - jnp.maximum(jnp.max(sc, axis=0, keepdims=True), 0.1 * NEG))
        lc = jnp.sum(pc, axis=0, keepdims=True)
        pc = pc * (1.0 / jnp.where(lc > 0.0, lc, 1.0))
        return _dot(vcts[h // G_B], pc.astype(BF16)), pc

    qtg = [jnp.concatenate(qts[G_B * g:G_B * (g + 1)], axis=1) for g in range(KV_B)]
    head = lambda wide, h: wide[h // G_B][:, (h % G_B) * qn:(h % G_B + 1) * qn]
    s_c = [_dot(kcs[g], qtg[g]) for g in range(KV_B)]
    cmp_out = [cmp_softmax(h, head(s_c, h) + cmask) for h in range(H_B)]
    o_c = [o for o, _ in cmp_out]
    slabs = qn // LANES
    for g in range(KV_B):
        imp_c = sum(pc for _, pc in cmp_out[G_B * g:G_B * (g + 1)])
        ratio = SEL_BLK // CMP_STRIDE
        cols = []
        for m in range(slabs):
            imp_sc[g * slabs + m] = imp_c[:, m * LANES:(m + 1) * LANES]
            cols.append(sum(imp_sc[g * slabs + m, pl.ds(r, n_sb, stride=ratio), :] for r in range(ratio)))
        imp = jnp.concatenate(cols, axis=1)
        bi = lax.broadcasted_iota(jnp.int32, (n_sb, qn), 0)
        cur = jnp.right_shift(qpos, SEL_BLK.bit_length() - 1)
        forced = jnp.logical_or(bi == 0, bi > cur - N_LOCAL)
        score = jnp.where(bi > cur, -1e9, jnp.where(forced, 1e9, imp))
        bias_sc[g] = jnp.where(_topk_rows(score, min(N_SEL, n_sb)), 0.0, NEG)

    span = wmask_ref.shape[1]
    w0 = pl.multiple_of(jnp.maximum(q0 - WINDOW, 0), LANES)
    wmask = wmask_ref[0]
    kws = [kw_ref[0, pl.ds(w0, span), gls[g]] for g in range(KV_B)]
    vwts = [vwt_ref[0, gds[g], pl.ds(w0, span)].astype(BF16) for g in range(KV_B)]

    def win_softmax(h, s):
        p = jnp.exp(s - jnp.max(s, axis=0, keepdims=True))
        l = jnp.sum(p, axis=0, keepdims=True)
        return _dot(vwts[h // G_B], p.astype(BF16)) * (1.0 / l)

    s_w = [_dot(kws[g], qtg[g]) for g in range(KV_B)]
    o_w = [win_softmax(h, head(s_w, h) + wmask) for h in range(H_B)]

    last = q0 // pair

    def sel_pairs(ps, carry):
        staged = []
        for p in ps:
            k0 = pl.multiple_of(p * pair, pair)
            diag = jnp.where(p == last, dmask_ref[0], 0.0)
            bias, vt, wide = [], [], []
            for g in range(KV_B):
                rows = [jnp.broadcast_to(bias_sc[g, pl.ds((pair // SEL_BLK) * p + i, 1), :],
                                         (SEL_BLK, qn)) for i in range(pair // SEL_BLK)]
                bias.append(jnp.concatenate(rows, axis=0) + diag)
                vt.append(vst_ref[0, gds[g], pl.ds(k0, pair)].astype(BF16))
                wide.append(_dot(ks_ref[0, pl.ds(k0, pair), gls[g]], qtg[g]))
            staged.append((bias, vt, wide))
        for bias, vt, wide in staged:
            carry = tuple(_online_update(carry[h], head(wide, h) + bias[h // G_B], vt[h // G_B])
                          for h in range(H_B))
        return carry

    init = (jnp.full((1, qn), NEG, F32), jnp.zeros((1, qn), F32), jnp.zeros((HD, qn), F32))
    n_pairs = last + 1
    group = lambda base, n: (lambda c: sel_pairs([base + i for i in range(n)], c))
    carry = lax.fori_loop(0, n_pairs // SEL_GROUP, lambda i, c: group(SEL_GROUP * i, SEL_GROUP)(c),
                          (init,) * H_B)
    done = SEL_GROUP * (n_pairs // SEL_GROUP)
    n = SEL_GROUP // 2
    while n >= 1:
        take = jnp.bitwise_and(n_pairs, n) != 0
        carry = lax.cond(take, group(done, n), lambda c: c, carry)
        done = done + jnp.where(take, n, 0)
        n //= 2
    o_s = [a * (1.0 / l) for _, l, a in carry]

    for m in range(H_B // 2):
        halves = [gt_ref[0, h:h + 1, :] * o_c[h] + gt_ref[0, H_B + h:H_B + h + 1, :] * o_s[h]
                  + gt_ref[0, 2 * H_B + h:2 * H_B + h + 1, :] * o_w[h] for h in (2 * m, 2 * m + 1)]
        o_ref[:, m * LANES:(m + 1) * LANES] = jnp.concatenate(halves, axis=0).T


def _prompt_attention(qt, gt, ksa, vst, kwa, vwt, kca, vct):
    b, _, _, t = qt.shape
    nb = kca.shape[1]
    n_sb = t // SEL_BLK
    qn = next(c for c in (Q_TILE, LANES) if t % c == 0)
    nt = t // qn
    pair = SEL_PAIR
    span, n_wt, n_dv = WINDOW + qn, WINDOW // qn, max(pair // qn, 1)
    assert t % pair == 0 and t >= span and max(pair, qn) % min(pair, qn) == 0
    q = np.arange(qn)[None, None, :]
    bias = lambda ok: jnp.asarray(np.where(ok, 0.0, NEG), F32)
    ci = np.arange(nb)[None, :, None]
    tq = qn * np.arange(nt)[:, None, None] + q
    cmask = bias((tq >= CMP_STRIDE * ci + CMP_BLK - 1) & (ci < nb - 1))
    r = np.arange(pair)[None, :, None]
    dmask = bias(r <= qn * np.arange(n_dv)[:, None, None] + q)
    r = np.arange(span)[None, :, None]
    dist = qn * np.arange(n_wt + 1)[:, None, None] + q - r
    wmask = bias((dist >= 0) & (dist < WINDOW))
    per_b = lambda a: pl.BlockSpec((1,) + a.shape[1:], lambda i, j: (i,) + (0,) * (a.ndim - 1))
    return pl.pallas_call(
        _pattn_kernel,
        grid=(b, nt),
        in_specs=[pl.BlockSpec((1, H_B, LANES, qn), lambda i, j: (i, 0, 0, j)),
                  pl.BlockSpec((1, 3 * H_B, qn), lambda i, j: (i, 0, j)),
                  per_b(ksa), per_b(vst), per_b(kwa), per_b(vwt), per_b(kca), per_b(vct),
                  pl.BlockSpec((1, nb, qn), lambda i, j: (j, 0, 0)),
                  pl.BlockSpec((1, pair, qn), lambda i, j: (lax.rem(j, n_dv), 0, 0)),
                  pl.BlockSpec((1, span, qn), lambda i, j: (jnp.minimum(j, n_wt), 0, 0))],
        out_specs=pl.BlockSpec((qn, D_B), lambda i, j: (i * nt + j, 0)),
        out_shape=jax.ShapeDtypeStruct((b * t, D_B), F32),
        scratch_shapes=[pltpu.VMEM((KV_B * (qn // LANES), nb, LANES), F32),
                        pltpu.VMEM((KV_B, n_sb, qn), F32)],
        compiler_params=pltpu.CompilerParams(dimension_semantics=("arbitrary", "arbitrary"),
                                             vmem_limit_bytes=VMEM_LIMIT),
        name="prompt_attention",
    )(qt, gt, ksa, vst, kwa, vwt, kca, vct, cmask, dmask, wmask)


def _page_copy(hbm, buf, sem, page, slot, which, p):
    return pltpu.make_async_copy(hbm.at[page], buf.at[slot, which, p], sem.at[slot, which])


def _gather_pages(pt_ref, hbm_a, hbm_b, buf, sem):
    b = pl.program_id(0)
    nb = pl.num_programs(0)
    n_pages = pt_ref.shape[1]
    slot = lax.rem(b, 2)

    def issue(bb, sl):
        for p in range(n_pages):
            page = pt_ref[bb, p]
            _page_copy(hbm_a, buf, sem, page, sl, 0, p).start()
            _page_copy(hbm_b, buf, sem, page, sl, 1, p).start()

    @pl.when(b == 0)
    def _():
        issue(0, 0)

    @pl.when(b + 1 < nb)
    def _():
        issue(b + 1, 1 - slot)

    for p in range(n_pages):
        _page_copy(hbm_a, buf, sem, 0, slot, 0, p).wait()
        _page_copy(hbm_b, buf, sem, 0, slot, 1, p).wait()
    return slot


SUBLANES = 8


def _sublane_transpose(tiles):
    tiles = list(tiles)
    sub = lax.broadcasted_iota(jnp.int32, tiles[0].shape, 0)
    k = 1
    while k < SUBLANES:
        hi = jnp.bitwise_and(sub, k) != 0
        for c in range(SUBLANES):
            if c & k:
                continue
            a, b = tiles[c], tiles[c | k]
            tiles[c] = jnp.where(hi, pltpu.roll(b, k, 0), a)
            tiles[c | k] = jnp.where(hi, b, pltpu.roll(a, SUBLANES - k, 0))
        k *= 2
    return tiles


def _chunk_rows(buf, slot, which, lhs_sc):
    n_pages, _, page = buf.shape[2:]
    cpp = page // CMP_STRIDE
    assert cpp == SUBLANES and CMP_STRIDE % SUBLANES == 0
    for pp in range(n_pages // 2):
        per_j = []
        for e in range(2):
            x = buf[slot, which, 2 * pp + e].T
            cols = [None] * CMP_STRIDE
            for h in range(CMP_STRIDE // SUBLANES):
                tiles = [x[CMP_STRIDE * c + SUBLANES * h:CMP_STRIDE * c + SUBLANES * (h + 1)]
                         for c in range(cpp)]
                cols[SUBLANES * h:SUBLANES * (h + 1)] = _sublane_transpose(tiles)
            per_j.append(cols)
        for j in range(CMP_STRIDE):
            lhs_sc[which, 2 * cpp * pp:2 * cpp * (pp + 1), KV_W * j:KV_W * (j + 1)] = (
                jnp.concatenate([per_j[0][j], per_j[1][j]], axis=0).astype(BF16))


def _scmp_kernel(pt_ref, kc_hbm, vc_hbm, qk_ref, slope_ref, irow_ref, w1k_ref, w1v_ref, peak_ref,
                 pebk_ref, peav_ref, pebv_ref, w2k_ref, w2v_ref, kn_ref, g128_ref, gsum_ref,
                 oc_ref, imp_ref, buf, sem, lhs_sc, *, past_len):
    slot = _gather_pages(pt_ref, kc_hbm, vc_hbm, buf, sem)
    _chunk_rows(buf, slot, 0, lhs_sc)
    _chunk_rows(buf, slot, 1, lhs_sc)
    kc = _compress_math(lhs_sc[0], w1k_ref, peak_ref, pebk_ref, w2k_ref)
    kc = kc * lax.rsqrt(_group_mean_sq(kc, g128_ref) + EPS) * kn_ref[...]
    vc = _compress_math(lhs_sc[1], w1v_ref, peav_ref, pebv_ref, w2v_ref)
    nb = kc.shape[0]
    nr = qk_ref.shape[1]

    s = _dot_nt(qk_ref[0], kc.astype(BF16))
    ci = lax.broadcasted_iota(jnp.int32, (nr, nb), 1)
    end = (CMP_STRIDE * ci + (CMP_BLK - 1)).astype(F32)
    dist = (past_len + irow_ref[:, 0:1]) - end
    valid = jnp.logical_and(dist >= 0.0, ci < nb - 1)
    s = jnp.where(valid, s - slope_ref[:, 0:1] * dist, NEG)
    p = jnp.where(valid, jnp.exp(s - jnp.max(s, axis=-1, keepdims=True)), 0.0)
    l = jnp.sum(p, axis=-1, keepdims=True)
    p = p * (1.0 / jnp.where(l > 0.0, l, 1.0))
    oc_ref[0] = _dot(p.astype(BF16), vc.astype(BF16))

    tq = nr // H_B
    parts = []
    for g in range(KV_B):
        base = g * G_B * tq
        parts.append(sum(p[base + hh * tq:base + (hh + 1) * tq] for hh in range(G_B)))
    imp = jnp.concatenate(parts, axis=0)
    hi = imp.astype(BF16)
    r1 = imp - hi.astype(F32)
    mid = r1.astype(BF16)
    lo = (r1 - mid.astype(F32)).astype(BF16)
    gs = gsum_ref[...]
    imp_ref[0] = _dot(hi, gs) + _dot(mid, gs) + _dot(lo, gs)


def _sample_compressed(page_table, kc_pages, vc_pages, qk, slope, irow, wts, past_len):
    b, n_pages = page_table.shape
    page = kc_pages.shape[2]
    cpp = page // CMP_STRIDE
    nb = n_pages * cpp
    nr = qk.shape[1]
    ratio = SEL_BLK // CMP_STRIDE
    n_blk = -(-(nb // ratio + 1) // LANES) * LANES
    gsum = jnp.asarray(np.arange(nb)[:, None] // ratio == np.arange(n_blk)[None, :], BF16)
    consts = [slope, irow] + _cmp_consts(wts) + [gsum]
    full = lambda a: pl.BlockSpec(a.shape, lambda i, pt: (0,) * a.ndim)
    row = lambda i, pt: (i, 0, 0)
    grid_spec = pltpu.PrefetchScalarGridSpec(
        num_scalar_prefetch=1,
        grid=(b,),
        in_specs=[pl.BlockSpec(memory_space=pl.ANY), pl.BlockSpec(memory_space=pl.ANY),
                  pl.BlockSpec((1, nr, LANES), row)] + [full(a) for a in consts],
        out_specs=[pl.BlockSpec((1, nr, KV_W), row), pl.BlockSpec((1, nr // G_B, n_blk), row)],
        scratch_shapes=[pltpu.VMEM((2, 2, n_pages, KV_W, page), F32),
                        pltpu.SemaphoreType.DMA((2, 2)),
                        pltpu.VMEM((2, nb, CMP_STRIDE * KV_W), BF16)],
    )
    return pl.pallas_call(
        functools.partial(_scmp_kernel, past_len=float(past_len)),
        grid_spec=grid_spec,
        out_shape=[jax.ShapeDtypeStruct((b, nr, KV_W), F32),
                   jax.ShapeDtypeStruct((b, nr // G_B, n_blk), F32)],
        compiler_params=pltpu.CompilerParams(dimension_semantics=("arbitrary",),
                                             vmem_limit_bytes=VMEM_LIMIT),
        name="sample_compressed",
    )(page_table, kc_pages, vc_pages, qk, *consts)


def _topk_lanes_kernel(imp_ref, out_ref, *, n_blocks, cur, k):
    imp = imp_ref[...]
    bi = lax.broadcasted_iota(jnp.int32, imp.shape, 1).astype(F32)
    forced = jnp.logical_or(bi == 0, bi > cur - N_LOCAL)
    score = jnp.where(bi > cur, -1e9, jnp.where(forced, 1e9, imp))
    score = jnp.where(bi < n_blocks, score, -jnp.inf)
    width = imp.shape[1]
    sel = jnp.zeros(imp.shape, jnp.bool_)
    for _ in range(k):
        m = jnp.max(score, axis=-1, keepdims=True)
        idx = jnp.min(jnp.where(score == m, bi, float(width)), axis=-1, keepdims=True)
        hit = bi == idx
        sel = jnp.logical_or(sel, hit)
        score = jnp.where(hit, -jnp.inf, score)
    out_ref[...] = jnp.where(sel, 0.0, NEG)


def _sample_topk(imp2d, n_blocks, cur):
    n, w = imp2d.shape
    tr = 256 if n % 256 == 0 else n
    blk = pl.BlockSpec((tr, w), lambda i: (i, 0))
    return pl.pallas_call(
        functools.partial(_topk_lanes_kernel, n_blocks=n_blocks, cur=cur, k=min(N_SEL, n_blocks)),
        grid=(n // tr,),
        in_specs=[blk],
        out_specs=blk,
        out_shape=jax.ShapeDtypeStruct((n, w), F32),
        compiler_params=pltpu.CompilerParams(dimension_semantics=("arbitrary",)),
        name="sample_topk",
    )(imp2d)


def _shift_in(st_ref, tail_ref, out_ref, n_new):
    x = st_ref[0]
    n_buf = x.shape[1]
    rolled = pltpu.roll(x, n_buf - n_new, 1)
    lane = lax.broadcasted_iota(jnp.int32, (x.shape[0], LANES), 1)
    last = jnp.where(lane >= LANES - n_new, tail_ref[0], rolled[:, n_buf - LANES:])
    out_ref[0] = jnp.concatenate([rolled[:, :n_buf - LANES], last], axis=1)


def _ssel_kernel(pt_ref, ks_hbm, vs_hbm, qsel_ref, knewt_ref, vnew_ref, kwst_ref, vwst_ref, kwnewt_ref,
                 vwnew_ref, kwtail_ref, vwtail_ref, oc_ref, gsel_ref, slope_ref, irow_ref,
                 o_ref, kwout_ref, vwout_ref, buf, sem, s_sc, oh_sc, *, past_len, n_new):
    b = pl.program_id(0)
    _shift_in(kwst_ref, kwtail_ref, kwout_ref, n_new)
    _shift_in(vwst_ref, vwtail_ref, vwout_ref, n_new)
    n_pages, _, page = buf.shape[2:]
    nr = qsel_ref.shape[1]
    ppt = next(c for c in (4, 2, 1) if n_pages % c == 0)
    tile = ppt * page

    @pl.when(b == 0)
    def _():
        bi = lax.broadcasted_iota(jnp.int32, oh_sc.shape, 0)
        ki = lax.broadcasted_iota(jnp.int32, oh_sc.shape, 1)
        oh_sc[...] = jnp.where(jnp.right_shift(ki, SEL_BLK.bit_length() - 1) == bi, 1.0, 0.0).astype(BF16)

    slot = _gather_pages(pt_ref, ks_hbm, vs_hbm, buf, sem)
    slope = slope_ref[:, 0:1]
    irow = irow_ref[:, 0:1]
    qpos = past_len + irow
    qall = qsel_ref[0]
    qk = qall[:, :KV_W]

    def key_tile(which, j):
        return jnp.concatenate([buf[slot, which, ppt * j + i] for i in range(ppt)], axis=1).astype(BF16)

    lane = lax.broadcasted_iota(jnp.int32, (nr, tile), 1).astype(F32)
    mrun = jnp.full((nr, tile), NEG, F32)
    for j in range(n_pages // ppt):
        kcat = jnp.concatenate([key_tile(0, j), oh_sc[:, j * tile:(j + 1) * tile]], axis=0)
        s = _dot(qall, kcat) - slope * (qpos - (float(j * tile) + lane))
        s_sc[:, j * tile:(j + 1) * tile] = s
        mrun = jnp.maximum(mrun, s)
    nl = lax.broadcasted_iota(jnp.int32, (nr, LANES), 1).astype(F32)
    dn = irow - nl
    s_new = jnp.where(dn >= 0.0, _dot(qk, knewt_ref[0]) - slope * dn, NEG)
    m = jnp.maximum(jnp.max(mrun, axis=-1, keepdims=True), jnp.max(s_new, axis=-1, keepdims=True))

    p_new = jnp.exp(s_new - m)
    acc = _dot(p_new.astype(BF16), vnew_ref[0])
    lrun = jnp.zeros((nr, tile), F32)
    for j in range(n_pages // ppt):
        p = jnp.exp(s_sc[:, j * tile:(j + 1) * tile] - m)
        acc = acc + _dot_nt(p.astype(BF16), key_tile(1, j))
        lrun = lrun + p
    l = jnp.sum(lrun, axis=-1, keepdims=True) + jnp.sum(p_new, axis=-1, keepdims=True)
    o_s = acc * (1.0 / l)

    n_buf = kwst_ref.shape[2]
    wl = lax.broadcasted_iota(jnp.int32, (nr, n_buf), 1).astype(F32)
    dw = (n_buf + irow) - wl
    s_w = jnp.where(jnp.logical_and(dw >= 0.0, dw < WINDOW),
                    _dot(qk, kwst_ref[0].astype(BF16)) - slope * dw, NEG)
    s_wn = jnp.where(dn >= 0.0, _dot(qk, kwnewt_ref[0]) - slope * dn, NEG)
    m_w = jnp.maximum(jnp.max(s_w, axis=-1, keepdims=True), jnp.max(s_wn, axis=-1, keepdims=True))
    p_w = jnp.exp(s_w - m_w)
    p_wn = jnp.exp(s_wn - m_w)
    l_w = jnp.sum(p_w, axis=-1, keepdims=True) + jnp.sum(p_wn, axis=-1, keepdims=True)
    o_w = (_dot_nt(p_w.astype(BF16), vwst_ref[0].astype(BF16))
           + _dot(p_wn.astype(BF16), vwnew_ref[0])) * (1.0 / l_w)

    gsel = gsel_ref[0]
    o_ref[0] = gsel[:, 0:1] * oc_ref[0] + gsel[:, 1:2] * o_s + gsel[:, 2:3] * o_w


def _sample_selected(page_table, ks_pages, vs_pages, qsel, knewt, vnew, kwst, vwst, kwnewt, vwnew, kwtail,
                     vwtail, oc, gsel, slope, irow, past_len, n_new):
    b, n_pages = page_table.shape
    page = ks_pages.shape[2]
    n_keys = n_pages * page
    nr = qsel.shape[1]
    assert kwst.shape[2] % LANES == 0 and n_new <= LANES
    per_b = lambda a: pl.BlockSpec((1,) + a.shape[1:], lambda i, pt: (i,) + (0,) * (a.ndim - 1))
    full = lambda a: pl.BlockSpec(a.shape, lambda i, pt: (0,) * a.ndim)
    grid_spec = pltpu.PrefetchScalarGridSpec(
        num_scalar_prefetch=1,
        grid=(b,),
        in_specs=[pl.BlockSpec(memory_space=pl.ANY), pl.BlockSpec(memory_space=pl.ANY)]
                 + [per_b(a) for a in (qsel, knewt, vnew, kwst, vwst, kwnewt, vwnew, kwtail, vwtail, oc, gsel)]
                 + [full(slope), full(irow)],
        out_specs=[pl.BlockSpec((1, nr, KV_W), lambda i, pt: (i, 0, 0)), per_b(kwst), per_b(vwst)],
        scratch_shapes=[pltpu.VMEM((2, 2, n_pages, KV_W, page), F32),
                        pltpu.SemaphoreType.DMA((2, 2)),
                        pltpu.VMEM((nr, n_keys), F32),
                        pltpu.VMEM((n_keys // SEL_BLK, n_keys), BF16)],
    )
    return pl.pallas_call(
        functools.partial(_ssel_kernel, past_len=float(past_len), n_new=n_new),
        grid_spec=grid_spec,
        out_shape=[jax.ShapeDtypeStruct((b, nr, KV_W), F32),
                   jax.ShapeDtypeStruct(kwst.shape, F32), jax.ShapeDtypeStruct(vwst.shape, F32)],
        compiler_params=pltpu.CompilerParams(dimension_semantics=("arbitrary",),
                                             vmem_limit_bytes=VMEM_LIMIT),
        name="sample_selected",
    )(page_table, ks_pages, vs_pages, qsel, knewt, vnew, kwst, vwst, kwnewt, vwnew, kwtail, vwtail, oc,
      gsel, slope, irow)


def _slopes():
    return np.asarray([2.0 ** -(h + 1) for h in range(H_B)], np.float32)


def _block_ones(n):
    idx = np.arange(n) // HD
    return jnp.asarray(idx[:, None] == idx[None, :], BF16)


def _layer_weights(l, t_chunk, norm_in, w_in, b_gate, ln_v_g, ln_v_b, w_spatial, b_spatial, q_norm,
                   k_norm_cmp, k_norm_sel, k_norm_win, cmp_pe_k, cmp_pe_v, w_cmp_k1, w_cmp_k2,
                   w_cmp_v1, w_cmp_v2, out_norm_a, out_norm_b, w_out):
    w = w_in[l]
    n_main = 3 * D_A + 2 * D_B
    n_gate = 3 * H_B
    w_pack = jnp.concatenate([w[:, :n_main], w[:, n_main + n_gate:], w[:, n_main:n_main + n_gate],
                              jnp.zeros((D_MODEL, N_PACK - w.shape[1]), F32)], axis=1).astype(BF16)
    c = t_chunk
    rep = CHUNK // c
    tri = jnp.where(jnp.tril(jnp.ones((c, c), bool)), w_spatial[l][:, :c, :c], 0.0)
    wc = jnp.einsum("ab,hij->haibj", jnp.eye(rep, dtype=F32), tri).reshape(H_A, CHUNK, CHUNK).astype(BF16)
    bc = jnp.tile(jnp.repeat(b_spatial[l][:, :c].T, D_A // H_A, axis=1), (rep, 1))

    def big1(w1):
        w1r = w1.reshape(2, CMP_STRIDE, HD, CMP_HID)
        return jnp.einsum("ajdh,ck->ajcdkh", w1r, jnp.eye(KV_B, dtype=F32)).reshape(
            2, CMP_STRIDE * KV_W, KV_B * CMP_HID).astype(BF16)

    def big2(w2):
        return jnp.einsum("hd,kc->khcd", w2, jnp.eye(KV_B, dtype=F32)).reshape(
            KV_B * CMP_HID, KV_W).astype(BF16)

    def pe_rows(pe):
        halves = pe.reshape(2, CMP_STRIDE, 1, HD)
        flat = jnp.broadcast_to(halves, (2, CMP_STRIDE, KV_B, HD)).reshape(2, 1, CMP_STRIDE * KV_W)
        flat = jnp.broadcast_to(flat, (2, 8, CMP_STRIDE * KV_W)).astype(BF16)
        return flat[0], flat[1]

    peak, pebk = pe_rows(cmp_pe_k[l])
    peav, pebv = pe_rows(cmp_pe_v[l])
    row = lambda a, reps: jnp.tile(a, reps)[None, :]
    bg = jnp.concatenate([b_gate[l], jnp.zeros((LANES - n_gate,), F32)])[None, :]
    return dict(
        nin=norm_in[l][None, :], w_in=w_pack, bg=bg, lng=ln_v_g[l][None, :], lnb=ln_v_b[l][None, :],
        wc=wc, bc=bc, qn=row(q_norm[l], H_B), ksn=row(k_norm_sel[l], KV_B), kwn=row(k_norm_win[l], KV_B),
        kcn=row(k_norm_cmp[l], KV_B), ona=out_norm_a[l][None, :], onb=out_norm_b[l][None, :],
        g512=_block_ones(D_B), g128=_block_ones(KV_W),
        w1k=big1(w_cmp_k1[l]), w1v=big1(w_cmp_v1[l]), w2k=big2(w_cmp_k2[l]), w2v=big2(w_cmp_v2[l]),
        peak=peak, pebk=pebk, peav=peav, pebv=pebv, w_out=w_out[l].astype(BF16))


def _prompt_layer(x, wts):
    b, t, _ = x.shape
    n = b * t
    x2d = x.reshape(n, D_MODEL)
    (oa, szb, qt, gt, ksa, kwa, kc_r, vc_r,
     kct, vct, kst, vst, kwt, vwt) = _projection_prompt(x2d, b, t, wts)
    kca, vcct = _compress_prompt(kc_r, vc_r, b, wts)
    ob = _prompt_attention(qt, gt, ksa, vst, kwa, vwt, kca, vcct)
    y = _merge(x2d, oa, ob, szb, wts).reshape(b, t, D_MODEL)
    heads = lambda a: jnp.transpose(a.reshape(b, KV_B, HD, a.shape[2]), (0, 3, 1, 2))
    n_keep = min(WINDOW, t)
    return y, (heads(kct), heads(vct), heads(kst), heads(vst),
               heads(kwt[:, :, t - n_keep:]), heads(vwt[:, :, t - n_keep:]))


def _sample_layer(x, l, cache_k_cmp, cache_v_cmp, cache_k_sel, cache_v_sel, k_win_buf, v_win_buf,
                  page_table, wts):
    b, t, _ = x.shape
    n = b * t
    n_pool, page = cache_k_cmp.shape[1], cache_k_cmp.shape[2]
    n_pages = page_table.shape[1]
    past_len = n_pages * page
    x2d = x.reshape(n, D_MODEL)
    oa, vn, q, szb, gates, kc_r, vc_r, ks, vs, kw, vw = _projection_sample(x2d, wts)

    nr = H_B * t
    sl = _slopes()
    slope = jnp.asarray(np.broadcast_to(np.repeat(sl, t)[:, None], (nr, LANES)).copy())
    irow = jnp.asarray(np.broadcast_to(np.tile(np.arange(t, dtype=np.float32), H_B)[:, None], (nr, LANES)).copy())
    qh = jnp.transpose((q * SCALE).reshape(b, t, KV_B, G_B, HD), (0, 2, 3, 1, 4))
    qk = jnp.einsum("bghtd,gk->bghtkd", qh, jnp.eye(KV_B, dtype=F32)).reshape(b, nr, KV_W).astype(BF16)

    minor_pos = lambda a: jnp.transpose(a, (0, 2, 3, 1)).reshape(a.shape[0], KV_W, a.shape[1])
    oc, imp = _sample_compressed(page_table, minor_pos(cache_k_cmp[l]), minor_pos(cache_v_cmp[l]),
                                 qk, slope, irow, wts, past_len)

    n_sb = -(-(past_len + t) // SEL_BLK)
    n_pb = past_len // SEL_BLK
    cur = past_len // SEL_BLK
    wpad = imp.shape[2]
    bias = _sample_topk(imp.reshape(b * KV_B * t, wpad), n_sb, cur).reshape(b, KV_B, 1, t, wpad)
    bias = jnp.broadcast_to(bias[..., :n_pb], (b, KV_B, G_B, t, n_pb)).reshape(b, nr, n_pb)
    qsel = jnp.concatenate([qk, bias.astype(BF16)], axis=-1)

    pad_rows = lambda a: jnp.pad(a.reshape(b, t, KV_W).astype(BF16), ((0, 0), (0, LANES - t), (0, 0)))
    pad_cols = lambda a: jnp.transpose(pad_rows(a), (0, 2, 1))
    gsel = jnp.transpose(gates[:, :3 * H_B].reshape(b, t, 3, H_B), (0, 3, 1, 2)).reshape(b, nr, 3)
    gsel = jnp.pad(gsel, ((0, 0), (0, 0), (0, LANES - 3)))
    tail = lambda a: jnp.pad(jnp.transpose(a.reshape(b, t, KV_W), (0, 2, 1)), ((0, 0), (0, 0), (LANES - t, 0)))
    o, kw_next, vw_next = _sample_selected(
        page_table, minor_pos(cache_k_sel[l]), minor_pos(cache_v_sel[l]), qsel, pad_cols(ks), pad_rows(vs),
        minor_pos(k_win_buf), minor_pos(v_win_buf), pad_cols(kw), pad_rows(vw), tail(kw), tail(vw), oc, gsel,
        slope, irow, past_len, t)
    o5 = o.reshape(b, KV_B, G_B, t, KV_B, HD)
    ob = jnp.stack([o5[:, g, :, :, g, :] for g in range(KV_B)], axis=1)
    ob = jnp.transpose(ob, (0, 3, 1, 2, 4)).reshape(n, D_B)
    y = _merge(x2d, oa, ob, szb, wts).reshape(b, t, D_MODEL)
    heads = lambda a: a.reshape(b, t, KV_B, HD)
    major_pos = lambda a: jnp.transpose(a.reshape(b, KV_B, HD, a.shape[2]), (0, 3, 1, 2))
    return y, (heads(kc_r), heads(vc_r), heads(ks), heads(vs), major_pos(kw_next), major_pos(vw_next),
               vn.reshape(b, t, D_A))


def kernel(x_prompt, x_sample, cache_k_cmp, cache_v_cmp, cache_k_sel, cache_v_sel, state_k_win, state_v_win, page_table, norm_in, w_in, b_gate, ln_v_g, ln_v_b, w_spatial, b_spatial, q_norm, k_norm_cmp, k_norm_sel, k_norm_win, cmp_pe_k, cmp_pe_v, w_cmp_k1, w_cmp_k2, w_cmp_v1, w_cmp_v2, out_norm_a, out_norm_b, w_out):
    params = (norm_in, w_in, b_gate, ln_v_g, ln_v_b, w_spatial, b_spatial, q_norm, k_norm_cmp,
              k_norm_sel, k_norm_win, cmp_pe_k, cmp_pe_v, w_cmp_k1, w_cmp_k2, w_cmp_v1, w_cmp_v2,
              out_norm_a, out_norm_b, w_out)
    depth = w_in.shape[0]
    h_p, h_s = x_prompt, x_sample
    st_p, st_s = [], []
    for l in range(depth):
        wts_p = _layer_weights(l, min(h_p.shape[1], CHUNK), *params)
        wts_s = _layer_weights(l, min(h_s.shape[1], CHUNK), *params)
        h_p, sp = _prompt_layer(h_p, wts_p)
        h_s, ss = _sample_layer(h_s, l, cache_k_cmp, cache_v_cmp, cache_k_sel, cache_v_sel,
                                state_k_win[l], state_v_win[l], page_table, wts_s)
        st_p.append(sp)
        st_s.append(ss)
    new_p = [jnp.stack(ts) for ts in zip(*st_p)]
    new_s = [jnp.stack(ts) for ts in zip(*st_s)]
    return (h_p, h_s, new_p[0], new_p[1], new_p[2], new_p[3], new_p[4], new_p[5],
            new_s[0], new_s[1], new_s[2], new_s[3], new_s[4], new_s[5], new_s[6])
```

```python
import functools

import numpy as np
import jax
import jax.numpy as jnp
from jax import lax
from jax.experimental import pallas as pl
from jax.experimental.pallas import tpu as pltpu

F32 = jnp.float32
BF16 = jnp.bfloat16

D_MODEL = 1024
D_A = 512
H_A = 8
CHUNK = 128
D_B = 512
H_B = 8
HD = 64
KV_B = 2
G_B = 4
KV_W = KV_B * HD
CMP_STRIDE = 16
CMP_BLK = 32
CMP_HID = 128
SEL_BLK = 64
N_SEL = 16
N_LOCAL = 2
WINDOW = 512
Q_BLOCK = 128
EPS = 1e-6
NEG = -1e30
SCALE = HD ** -0.5

LANES = 128
N_PACK = 3456
VMEM_LIMIT = 56 * 1024 * 1024


def _dot(a, b):
    return jnp.dot(a, b, preferred_element_type=F32)


def _dot_nt(a, b):
    return lax.dot_general(a, b, (((1,), (1,)), ((), ())), preferred_element_type=F32)


def _group_mean_sq(t, g_ref):
    t2 = t * t
    hi = t2.astype(BF16)
    lo = (t2 - hi.astype(F32)).astype(BF16)
    g = g_ref[...]
    return (_dot(hi, g) + _dot(lo, g)) * (1.0 / HD)


def _silu(z):
    return z * jax.nn.sigmoid(z)


def _key_feats(pos):
    lane = lax.broadcasted_iota(jnp.int32, (pos.shape[0], LANES), 1)
    sh = SEL_BLK.bit_length() - 1
    blk = (jnp.right_shift(pos, sh) * SEL_BLK).astype(F32)
    off = jnp.bitwise_and(pos, SEL_BLK - 1).astype(F32)
    return jnp.where(lane == HD, blk, jnp.where(lane == HD + 1, off,
                     jnp.where(jnp.logical_or(lane == HD + 2, lane == HD + 3), 1.0, 0.0)))


def _key_aug(k, pos):
    lane = lax.broadcasted_iota(jnp.int32, k.shape, 1)
    feats = _key_feats(pos)
    g0 = jnp.where(lane < HD, k, feats)
    g1 = jnp.where(lane < HD, pltpu.roll(k, HD, 1), feats)
    return jnp.concatenate([g0, g1], axis=1).astype(BF16)


def _proj_values(x_ref, nin_ref, w_ref, bg_ref, lng_ref, lnb_ref, wc_ref, bc_ref, qn_ref, ksn_ref,
                 kwn_ref, ona_ref, g512_ref, g128_ref):
    tn = x_ref.shape[0]
    x = x_ref[...]
    ms = jnp.mean(x * x, axis=-1, keepdims=True)
    xb = (x * lax.rsqrt(ms + EPS) * nin_ref[...]).astype(BF16)

    def proj(lo, n):
        return _dot(xb, w_ref[:, lo:lo + n])

    u = proj(0, D_A)
    v = proj(D_A, D_A)
    mu = jnp.mean(v, axis=-1, keepdims=True)
    vc = v - mu
    var = jnp.mean(vc * vc, axis=-1, keepdims=True)
    vn = vc * lax.rsqrt(var + EPS) * lng_ref[...] + lnb_ref[...]
    vnb = vn.astype(BF16)
    lane = lax.broadcasted_iota(jnp.int32, (CHUNK, LANES), 1)
    rows = []
    for c in range(tn // CHUNK):
        vcb = vnb[c * CHUNK:(c + 1) * CHUNK]
        cols = []
        for m in range(D_A // LANES):
            va = vcb[:, m * LANES:(m + 1) * LANES]
            sa = _dot(wc_ref[2 * m], va)
            sb = _dot(wc_ref[2 * m + 1], va)
            cols.append(jnp.where(lane < HD, sa, sb))
        rows.append(jnp.concatenate(cols, axis=1) + bc_ref[...])
    s = jnp.concatenate(rows, axis=0)
    o = u * s
    o = o * lax.rsqrt(jnp.mean(o * o, axis=-1, keepdims=True) + EPS) * ona_ref[...]
    za = proj(2 * D_A, D_A)
    oa = (o * _silu(za)).astype(BF16)

    q = proj(3 * D_A, D_B)
    q = q * lax.rsqrt(_group_mean_sq(q, g512_ref) + EPS) * qn_ref[...]
    szb = _silu(proj(3 * D_A + D_B, D_B)).astype(BF16)
    base = 3 * D_A + 2 * D_B
    kc = proj(base, KV_W)
    vc = proj(base + KV_W, KV_W)
    ks = proj(base + 2 * KV_W, KV_W)
    ks = ks * lax.rsqrt(_group_mean_sq(ks, g128_ref) + EPS) * ksn_ref[...]
    vs = proj(base + 3 * KV_W, KV_W)
    kw = proj(base + 4 * KV_W, KV_W)
    kw = kw * lax.rsqrt(_group_mean_sq(kw, g128_ref) + EPS) * kwn_ref[...]
    vw = proj(base + 5 * KV_W, KV_W)
    gates = jax.nn.sigmoid(proj(base + 6 * KV_W, LANES) + bg_ref[...])
    return oa, vn, q, szb, gates, kc, vc, ks, vs, kw, vw


N_PROJ_CONSTS = 13


def _proj_sample_kernel(*refs):
    ins, outs = refs[:1 + N_PROJ_CONSTS], refs[1 + N_PROJ_CONSTS:]
    for ref, val in zip(outs, _proj_values(*ins)):
        ref[...] = val


def _proj_prompt_kernel(*refs, seq):
    ins = refs[:1 + N_PROJ_CONSTS]
    (oa_ref, szb_ref, qt_ref, gt_ref, ksa_ref, kwa_ref, kcr_ref, vcr_ref,
     kct_ref, vct_ref, kst_ref, vst_ref, kwt_ref, vwt_ref) = refs[1 + N_PROJ_CONSTS:]
    oa, _, q, szb, gates, kc, vc, ks, vs, kw, vw = _proj_values(*ins)
    tn = oa.shape[0]
    t0 = lax.rem(pl.program_id(0) * tn, seq)
    oa_ref[...] = oa
    szb_ref[...] = szb
    kcr_ref[...] = kc
    vcr_ref[...] = vc
    for ref, val in ((kct_ref, kc), (vct_ref, vc), (kst_ref, ks), (vst_ref, vs), (kwt_ref, kw),
                     (vwt_ref, vw)):
        ref[0] = val.T
    gt_ref[0] = gates.T[:3 * H_B]
    pos_rows = t0 + lax.broadcasted_iota(jnp.int32, (tn, 1), 0)
    ksa_ref[0] = _key_aug(ks, pos_rows)
    kwa_ref[0] = _key_aug(kw, pos_rows)
    pos = t0 + lax.broadcasted_iota(jnp.int32, (HD, tn), 1)
    row = lax.broadcasted_iota(jnp.int32, (HD, tn), 0)
    sh = SEL_BLK.bit_length() - 1
    blk = (jnp.right_shift(pos, sh) * SEL_BLK).astype(F32)
    off = jnp.bitwise_and(pos, SEL_BLK - 1).astype(F32)
    qs = q * SCALE
    for m in range(D_B // LANES):
        qt = qs[:, m * LANES:(m + 1) * LANES].T
        for i in range(LANES // HD):
            h = (LANES // HD) * m + i
            slope = 2.0 ** -(h + 1)
            feats = jnp.where(row < 2, slope, jnp.where(row == 2, -slope * blk,
                              jnp.where(row == 3, -slope * off, 0.0)))
            qt_ref[0, h, :HD, :] = qt[i * HD:(i + 1) * HD].astype(BF16)
            qt_ref[0, h, HD:, :] = feats.astype(BF16)


def _row_tile(n):
    sizes = [tn for tn in (1024, 512, 256, 128) if n % tn == 0]
    for tn in sizes:
        if n // tn >= 2:
            return tn
    if sizes:
        return sizes[-1]
    raise ValueError(f"row count {n} must be a multiple of {CHUNK}")


def _proj_consts(wts):
    return [wts["nin"], wts["w_in"], wts["bg"], wts["lng"], wts["lnb"], wts["wc"], wts["bc"],
            wts["qn"], wts["ksn"], wts["kwn"], wts["ona"], wts["g512"], wts["g128"]]


def _projection_sample(x2d, wts):
    n = x2d.shape[0]
    tn = _row_tile(n)
    full = lambda a: pl.BlockSpec(a.shape, lambda i: (0,) * a.ndim)
    consts = _proj_consts(wts)
    row = lambda w: pl.BlockSpec((tn, w), lambda i: (i, 0))
    out_w = [(D_A, BF16), (D_A, F32), (D_B, F32), (D_B, BF16), (LANES, F32)] + [(KV_W, F32)] * 6
    return pl.pallas_call(
        _proj_sample_kernel,
        grid=(n // tn,),
        in_specs=[row(D_MODEL)] + [full(a) for a in consts],
        out_specs=[row(w) for w, _ in out_w],
        out_shape=[jax.ShapeDtypeStruct((n, w), dt) for w, dt in out_w],
        compiler_params=pltpu.CompilerParams(dimension_semantics=("arbitrary",),
                                             vmem_limit_bytes=VMEM_LIMIT),
        name="projection_sample",
    )(x2d, *consts)


def _projection_prompt(x2d, b, seq, wts):
    n = x2d.shape[0]
    tn = _row_tile(seq)
    tps = seq // tn
    full = lambda a: pl.BlockSpec(a.shape, lambda i: (0,) * a.ndim)
    consts = _proj_consts(wts)
    row = lambda w: pl.BlockSpec((tn, w), lambda i: (i, 0))
    minor = lambda r: pl.BlockSpec((1, r, tn), lambda i: (i // tps, 0, i % tps))
    specs = [(row(D_A), (n, D_A), BF16), (row(D_B), (n, D_B), BF16),
             (pl.BlockSpec((1, H_B, LANES, tn), lambda i: (i // tps, 0, 0, i % tps)),
              (b, H_B, LANES, seq), BF16),
             (minor(3 * H_B), (b, 3 * H_B, seq), F32)]
    specs += [(pl.BlockSpec((1, tn, 2 * KV_W), lambda i: (i // tps, i % tps, 0)),
               (b, seq, 2 * KV_W), BF16)] * 2
    specs += [(row(KV_W), (n, KV_W), F32)] * 2
    specs += [(minor(KV_W), (b, KV_W, seq), F32)] * 6
    return pl.pallas_call(
        functools.partial(_proj_prompt_kernel, seq=seq),
        grid=(n // tn,),
        in_specs=[row(D_MODEL)] + [full(a) for a in consts],
        out_specs=[s for s, _, _ in specs],
        out_shape=[jax.ShapeDtypeStruct(shape, dt) for _, shape, dt in specs],
        compiler_params=pltpu.CompilerParams(dimension_semantics=("arbitrary",),
                                             vmem_limit_bytes=VMEM_LIMIT),
        name="projection_prompt",
    )(x2d, *consts)


def _merge_kernel(x_ref, oa_ref, ob_ref, szb_ref, onb_ref, w_ref, y_ref):
    ob = ob_ref[...]
    ob = ob * lax.rsqrt(jnp.mean(ob * ob, axis=-1, keepdims=True) + EPS) * onb_ref[...]
    ob = (ob * szb_ref[...].astype(F32)).astype(BF16)
    y_ref[...] = x_ref[...] + _dot(oa_ref[...], w_ref[:D_A, :]) + _dot(ob, w_ref[D_A:, :])


def _merge(x2d, oa, ob, szb, wts):
    n = x2d.shape[0]
    tn = _row_tile(n)
    row = lambda w: pl.BlockSpec((tn, w), lambda i: (i, 0))
    full = lambda a: pl.BlockSpec(a.shape, lambda i: (0,) * a.ndim)
    return pl.pallas_call(
        _merge_kernel,
        grid=(n // tn,),
        in_specs=[row(D_MODEL), row(D_A), row(D_B), row(D_B), full(wts["onb"]), full(wts["w_out"])],
        out_specs=row(D_MODEL),
        out_shape=jax.ShapeDtypeStruct((n, D_MODEL), F32),
        compiler_params=pltpu.CompilerParams(dimension_semantics=("arbitrary",),
                                             vmem_limit_bytes=VMEM_LIMIT),
        name="merge",
    )(x2d, oa, ob, szb, wts["onb"], wts["w_out"])


def _compress_math(xb, w1_ref, pea_ref, peb_ref, w2_ref):
    h1 = _dot(xb, w1_ref[0])
    h2 = _dot(xb, w1_ref[1])
    r = h2.shape[0]
    h2n = pltpu.roll(h2, r - 1, 0)
    pe = _dot(pea_ref[...], w1_ref[0]) + _dot(peb_ref[...], w1_ref[1])
    pre = h1 + h2n + pe[0:1]
    return _dot(_silu(pre).astype(BF16), w2_ref[...])


def _compress_kernel(rk_ref, rv_ref, w1k_ref, w1v_ref, peak_ref, pebk_ref, peav_ref, pebv_ref,
                     w2k_ref, w2v_ref, kn_ref, g128_ref, kc_ref, vc_ref):
    nc = rk_ref.shape[0] // CMP_STRIDE

    def chunk_rows(ref):
        return jnp.concatenate([ref[pl.ds(j, nc, stride=CMP_STRIDE), :] for j in range(CMP_STRIDE)],
                               axis=1).astype(BF16)

    kc = _compress_math(chunk_rows(rk_ref), w1k_ref, peak_ref, pebk_ref, w2k_ref)
    kc = kc * lax.rsqrt(_group_mean_sq(kc, g128_ref) + EPS) * kn_ref[...]
    end = CMP_STRIDE * lax.broadcasted_iota(jnp.int32, (kc.shape[0], 1), 0) + (CMP_BLK - 1)
    kc_ref[0] = _key_aug(kc, end)
    vc = _compress_math(chunk_rows(rv_ref), w1v_ref, peav_ref, pebv_ref, w2v_ref)
    vc_ref[0] = vc.T.astype(BF16)


def _cmp_consts(wts):
    return [wts["w1k"], wts["w1v"], wts["peak"], wts["pebk"], wts["peav"], wts["pebv"],
            wts["w2k"], wts["w2v"], wts["kcn"], wts["g128"]]


def _compress_prompt(rk, rv, b, wts):
    seq = rk.shape[0] // b
    r = seq // CMP_STRIDE
    consts = _cmp_consts(wts)
    full = lambda a: pl.BlockSpec(a.shape, lambda i: (0,) * a.ndim)
    blk = pl.BlockSpec((seq, KV_W), lambda i: (i, 0))
    return pl.pallas_call(
        _compress_kernel,
        grid=(b,),
        in_specs=[blk, blk] + [full(a) for a in consts],
        out_specs=[pl.BlockSpec((1, r, 2 * KV_W), lambda i: (i, 0, 0)),
                   pl.BlockSpec((1, KV_W, r), lambda i: (i, 0, 0))],
        out_shape=[jax.ShapeDtypeStruct((b, r, 2 * KV_W), BF16),
                   jax.ShapeDtypeStruct((b, KV_W, r), BF16)],
        compiler_params=pltpu.CompilerParams(dimension_semantics=("arbitrary",),
                                             vmem_limit_bytes=VMEM_LIMIT),
        name="compress_prompt",
    )(rk, rv, *consts)


def _topk_rows(score, k):
    n = score.shape[0]
    ridx = lax.broadcasted_iota(jnp.int32, score.shape, 0).astype(F32)
    sel = jnp.zeros(score.shape, jnp.bool_)
    for _ in range(k):
        m = jnp.max(score, axis=0, keepdims=True)
        idx = jnp.min(jnp.where(score == m, ridx, float(n)), axis=0, keepdims=True)
        hit = ridx == idx
        sel = jnp.logical_or(sel, hit)
        score = jnp.where(hit, -jnp.inf, score)
    return sel


Q_TILE = 256
SEL_PAIR = 256
SEL_GROUP = 4


def _online_update(carry, s, vt):
    m, l, acc = carry
    m_new = jnp.maximum(m, jnp.max(s, axis=0, keepdims=True))
    alpha = jnp.exp(m - m_new)
    p = jnp.exp(s - m_new)
    l = alpha * l + jnp.sum(p, axis=0, keepdims=True)
    acc = alpha * acc + _dot(vt, p.astype(BF16))
    return m_new, l, acc


def _pattn_kernel(qt_ref, gt_ref, ks_ref, vst_ref, kw_ref, vwt_ref, kc_ref, vct_ref, cmask_ref,
                  dmask_ref, wmask_ref, o_ref, imp_sc, bias_sc):
    t = pl.program_id(1)
    n_sb = bias_sc.shape[1]
    pair = dmask_ref.shape[1]
    qn = qt_ref.shape[3]
    q0 = t * qn
    qpos = q0 + lax.broadcasted_iota(jnp.int32, (1, qn), 1)
    gls = [slice(g * LANES, (g + 1) * LANES) for g in range(KV_B)]
    gds = [slice(g * HD, (g + 1) * HD) for g in range(KV_B)]
    qts = [qt_ref[0, h] for h in range(H_B)]

    cmask = cmask_ref[0]
    kcs = [kc_ref[0, :, gls[g]] for g in range(KV_B)]
    vcts = [vct_ref[0, gds[g], :] for g in range(KV_B)]

    def cmp_softmax(h, sc):
        pc = jnp.exp(sc - jnp.maximum(jnp.max(sc, axis=0, keepdims=True), 0.1 * NEG))
        lc = jnp.sum(pc, axis=0, keepdims=True)
        pc = pc * (1.0 / jnp.where(lc > 0.0, lc, 1.0))
        return _dot(vcts[h // G_B], pc.astype(BF16)), pc

    qtg = [jnp.concatenate(qts[G_B * g:G_B * (g + 1)], axis=1) for g in range(KV_B)]
    head = lambda wide, h: wide[h // G_B][:, (h % G_B) * qn:(h % G_B + 1) * qn]
    s_c = [_dot(kcs[g], qtg[g]) for g in range(KV_B)]
    cmp_out = [cmp_softmax(h, head(s_c, h) + cmask) for h in range(H_B)]
    o_c = [o for o, _ in cmp_out]
    slabs = qn // LANES
    for g in range(KV_B):
        imp_c = sum(pc for _, pc in cmp_out[G_B * g:G_B * (g + 1)])
        ratio = SEL_BLK // CMP_STRIDE
        cols = []
        for m in range(slabs):
            imp_sc[g * slabs + m] = imp_c[:, m * LANES:(m + 1) * LANES]
            cols.append(sum(imp_sc[g * slabs + m, pl.ds(r, n_sb, stride=ratio), :] for r in range(ratio)))
        imp = jnp.concatenate(cols, axis=1)
        bi = lax.broadcasted_iota(jnp.int32, (n_sb, qn), 0)
        cur = jnp.right_shift(qpos, SEL_BLK.bit_length() - 1)
        forced = jnp.logical_or(bi == 0, bi > cur - N_LOCAL)
        score = jnp.where(bi > cur, -1e9, jnp.where(forced, 1e9, imp))
        bias_sc[g] = jnp.where(_topk_rows(score, min(N_SEL, n_sb)), 0.0, NEG)

    span = wmask_ref.shape[1]
    w0 = pl.multiple_of(jnp.maximum(q0 - WINDOW, 0), LANES)
    wmask = wmask_ref[0]
    kws = [kw_ref[0, pl.ds(w0, span), gls[g]] for g in range(KV_B)]
    vwts = [vwt_ref[0, gds[g], pl.ds(w0, span)].astype(BF16) for g in range(KV_B)]

    def win_softmax(h, s):
        p = jnp.exp(s - jnp.max(s, axis=0, keepdims=True))
        l = jnp.sum(p, axis=0, keepdims=True)
        return _dot(vwts[h // G_B], p.astype(BF16)) * (1.0 / l)

    s_w = [_dot(kws[g], qtg[g]) for g in range(KV_B)]
    o_w = [win_softmax(h, head(s_w, h) + wmask) for h in range(H_B)]

    last = q0 // pair

    def sel_pairs(ps, carry):
        staged = []
        for p in ps:
            k0 = pl.multiple_of(p * pair, pair)
            diag = jnp.where(p == last, dmask_ref[0], 0.0)
            bias, vt, wide = [], [], []
            for g in range(KV_B):
                rows = [jnp.broadcast_to(bias_sc[g, pl.ds((pair // SEL_BLK) * p + i, 1), :],
                                         (SEL_BLK, qn)) for i in range(pair // SEL_BLK)]
                bias.append(jnp.concatenate(rows, axis=0) + diag)
                vt.append(vst_ref[0, gds[g], pl.ds(k0, pair)].astype(BF16))
                wide.append(_dot(ks_ref[0, pl.ds(k0, pair), gls[g]], qtg[g]))
            staged.append((bias, vt, wide))
        for bias, vt, wide in staged:
            carry = tuple(_online_update(carry[h], head(wide, h) + bias[h // G_B], vt[h // G_B])
                          for h in range(H_B))
        return carry

    init = (jnp.full((1, qn), NEG, F32), jnp.zeros((1, qn), F32), jnp.zeros((HD, qn), F32))
    n_pairs = last + 1
    group = lambda base, n: (lambda c: sel_pairs([base + i for i in range(n)], c))
    carry = lax.fori_loop(0, n_pairs // SEL_GROUP, lambda i, c: group(SEL_GROUP * i, SEL_GROUP)(c),
                          (init,) * H_B)
    done = SEL_GROUP * (n_pairs // SEL_GROUP)
    n = SEL_GROUP // 2
    while n >= 1:
        take = jnp.bitwise_and(n_pairs, n) != 0
        carry = lax.cond(take, group(done, n), lambda c: c, carry)
        done = done + jnp.where(take, n, 0)
        n //= 2
    o_s = [a * (1.0 / l) for _, l, a in carry]

    for m in range(H_B // 2):
        halves = [gt_ref[0, h:h + 1, :] * o_c[h] + gt_ref[0, H_B + h:H_B + h + 1, :] * o_s[h]
                  + gt_ref[0, 2 * H_B + h:2 * H_B + h + 1, :] * o_w[h] for h in (2 * m, 2 * m + 1)]
        o_ref[:, m * LANES:(m + 1) * LANES] = jnp.concatenate(halves, axis=0).T


def _prompt_attention(qt, gt, ksa, vst, kwa, vwt, kca, vct):
    b, _, _, t = qt.shape
    nb = kca.shape[1]
    n_sb = t // SEL_BLK
    qn = next(c for c in (Q_TILE, LANES) if t % c == 0)
    nt = t // qn
    pair = SEL_PAIR
    span, n_wt, n_dv = WINDOW + qn, WINDOW // qn, max(pair // qn, 1)
    assert t % pair == 0 and t >= span and pair % qn == 0
    q = np.arange(qn)[None, None, :]
    bias = lambda ok: jnp.asarray(np.where(ok, 0.0, NEG), F32)
    ci = np.arange(nb)[None, :, None]
    tq = qn * np.arange(nt)[:, None, None] + q
    cmask = bias((tq >= CMP_STRIDE * ci + CMP_BLK - 1) & (ci < nb - 1))
    r = np.arange(pair)[None, :, None]
    dmask = bias(r <= qn * np.arange(n_dv)[:, None, None] + q)
    r = np.arange(span)[None, :, None]
    dist = qn * np.arange(n_wt + 1)[:, None, None] + q - r
    wmask = bias((dist >= 0) & (dist < WINDOW))
    per_b = lambda a: pl.BlockSpec((1,) + a.shape[1:], lambda i, j: (i,) + (0,) * (a.ndim - 1))
    return pl.pallas_call(
        _pattn_kernel,
        grid=(b, nt),
        in_specs=[pl.BlockSpec((1, H_B, LANES, qn), lambda i, j: (i, 0, 0, j)),
                  pl.BlockSpec((1, 3 * H_B, qn), lambda i, j: (i, 0, j)),
                  per_b(ksa), per_b(vst), per_b(kwa), per_b(vwt), per_b(kca), per_b(vct),
                  pl.BlockSpec((1, nb, qn), lambda i, j: (j, 0, 0)),
                  pl.BlockSpec((1, pair, qn), lambda i, j: (lax.rem(j, n_dv), 0, 0)),
                  pl.BlockSpec((1, span, qn), lambda i, j: (jnp.minimum(j, n_wt), 0, 0))],
        out_specs=pl.BlockSpec((qn, D_B), lambda i, j: (i * nt + j, 0)),
        out_shape=jax.ShapeDtypeStruct((b * t, D_B), F32),
        scratch_shapes=[pltpu.VMEM((KV_B * (qn // LANES), nb, LANES), F32),
                        pltpu.VMEM((KV_B, n_sb, qn), F32)],
        compiler_params=pltpu.CompilerParams(dimension_semantics=("arbitrary", "arbitrary"),
                                             vmem_limit_bytes=VMEM_LIMIT),
        name="prompt_attention",
    )(qt, gt, ksa, vst, kwa, vwt, kca, vct, cmask, dmask, wmask)


def _page_copy(hbm, buf, sem, page, slot, which, p):
    return pltpu.make_async_copy(hbm.at[page], buf.at[slot, which, p], sem.at[slot, which])


def _gather_pages(pt_ref, hbm_a, hbm_b, buf, sem):
    b = pl.program_id(0)
    nb = pl.num_programs(0)
    n_pages = pt_ref.shape[1]
    slot = lax.rem(b, 2)

    def issue(bb, sl):
        for p in range(n_pages):
            page = pt_ref[bb, p]
            _page_copy(hbm_a, buf, sem, page, sl, 0, p).start()
            _page_copy(hbm_b, buf, sem, page, sl, 1, p).start()

    @pl.when(b == 0)
    def _():
        issue(0, 0)

    @pl.when(b + 1 < nb)
    def _():
        issue(b + 1, 1 - slot)

    for p in range(n_pages):
        _page_copy(hbm_a, buf, sem, 0, slot, 0, p).wait()
        _page_copy(hbm_b, buf, sem, 0, slot, 1, p).wait()
    return slot


SUBLANES = 8


def _sublane_transpose(tiles):
    tiles = list(tiles)
    sub = lax.broadcasted_iota(jnp.int32, tiles[0].shape, 0)
    k = 1
    while k < SUBLANES:
        hi = jnp.bitwise_and(sub, k) != 0
        for c in range(SUBLANES):
            if c & k:
                continue
            a, b = tiles[c], tiles[c | k]
            tiles[c] = jnp.where(hi, pltpu.roll(b, k, 0), a)
            tiles[c | k] = jnp.where(hi, b, pltpu.roll(a, SUBLANES - k, 0))
        k *= 2
    return tiles


def _chunk_rows(buf, slot, which, lhs_sc):
    n_pages, _, page = buf.shape[2:]
    cpp = page // CMP_STRIDE
    assert cpp == SUBLANES and CMP_STRIDE % SUBLANES == 0
    for pp in range(n_pages // 2):
        per_j = []
        for e in range(2):
            x = buf[slot, which, 2 * pp + e].astype(BF16).T.astype(F32)
            cols = [None] * CMP_STRIDE
            for h in range(CMP_STRIDE // SUBLANES):
                tiles = [x[CMP_STRIDE * c + SUBLANES * h:CMP_STRIDE * c + SUBLANES * (h + 1)]
                         for c in range(cpp)]
                cols[SUBLANES * h:SUBLANES * (h + 1)] = _sublane_transpose(tiles)
            per_j.append(cols)
        for j in range(CMP_STRIDE):
            lhs_sc[which, 2 * cpp * pp:2 * cpp * (pp + 1), KV_W * j:KV_W * (j + 1)] = (
                jnp.concatenate([per_j[0][j], per_j[1][j]], axis=0).astype(BF16))


def _scmp_kernel(pt_ref, kc_hbm, vc_hbm, qk_ref, slope_ref, irow_ref, w1k_ref, w1v_ref, peak_ref,
                 pebk_ref, peav_ref, pebv_ref, w2k_ref, w2v_ref, kn_ref, g128_ref, gsum_ref,
                 oc_ref, imp_ref, buf, sem, lhs_sc, *, past_len):
    slot = _gather_pages(pt_ref, kc_hbm, vc_hbm, buf, sem)
    _chunk_rows(buf, slot, 0, lhs_sc)
    _chunk_rows(buf, slot, 1, lhs_sc)
    kc = _compress_math(lhs_sc[0], w1k_ref, peak_ref, pebk_ref, w2k_ref)
    kc = kc * lax.rsqrt(_group_mean_sq(kc, g128_ref) + EPS) * kn_ref[...]
    vc = _compress_math(lhs_sc[1], w1v_ref, peav_ref, pebv_ref, w2v_ref)
    nb = kc.shape[0]
    nr = qk_ref.shape[1]

    s = _dot_nt(qk_ref[0], kc.astype(BF16))
    ci = lax.broadcasted_iota(jnp.int32, (nr, nb), 1)
    end = (CMP_STRIDE * ci + (CMP_BLK - 1)).astype(F32)
    dist = (past_len + irow_ref[:, 0:1]) - end
    valid = jnp.logical_and(dist >= 0.0, ci < nb - 1)
    s = jnp.where(valid, s - slope_ref[:, 0:1] * dist, NEG)
    p = jnp.where(valid, jnp.exp(s - jnp.max(s, axis=-1, keepdims=True)), 0.0)
    l = jnp.sum(p, axis=-1, keepdims=True)
    p = p * (1.0 / jnp.where(l > 0.0, l, 1.0))
    oc_ref[0] = _dot(p.astype(BF16), vc.astype(BF16))

    tq = nr // H_B
    parts = []
    for g in range(KV_B):
        base = g * G_B * tq
        parts.append(sum(p[base + hh * tq:base + (hh + 1) * tq] for hh in range(G_B)))
    imp = jnp.concatenate(parts, axis=0)
    hi = imp.astype(BF16)
    r1 = imp - hi.astype(F32)
    mid = r1.astype(BF16)
    lo = (r1 - mid.astype(F32)).astype(BF16)
    gs = gsum_ref[...]
    imp_ref[0] = _dot(hi, gs) + _dot(mid, gs) + _dot(lo, gs)


def _sample_compressed(page_table, kc_pages, vc_pages, qk, slope, irow, wts, past_len):
    b, n_pages = page_table.shape
    page = kc_pages.shape[2]
    cpp = page // CMP_STRIDE
    nb = n_pages * cpp
    nr = qk.shape[1]
    ratio = SEL_BLK // CMP_STRIDE
    n_blk = -(-(nb // ratio + 1) // LANES) * LANES
    gsum = jnp.asarray(np.arange(nb)[:, None] // ratio == np.arange(n_blk)[None, :], BF16)
    consts = [slope, irow] + _cmp_consts(wts) + [gsum]
    full = lambda a: pl.BlockSpec(a.shape, lambda i, pt: (0,) * a.ndim)
    row = lambda i, pt: (i, 0, 0)
    grid_spec = pltpu.PrefetchScalarGridSpec(
        num_scalar_prefetch=1,
        grid=(b,),
        in_specs=[pl.BlockSpec(memory_space=pl.ANY), pl.BlockSpec(memory_space=pl.ANY),
                  pl.BlockSpec((1, nr, LANES), row)] + [full(a) for a in consts],
        out_specs=[pl.BlockSpec((1, nr, KV_W), row), pl.BlockSpec((1, nr // G_B, n_blk), row)],
        scratch_shapes=[pltpu.VMEM((2, 2, n_pages, KV_W, page), F32),
                        pltpu.SemaphoreType.DMA((2, 2)),
                        pltpu.VMEM((2, nb, CMP_STRIDE * KV_W), BF16)],
    )
    return pl.pallas_call(
        functools.partial(_scmp_kernel, past_len=float(past_len)),
        grid_spec=grid_spec,
        out_shape=[jax.ShapeDtypeStruct((b, nr, KV_W), F32),
                   jax.ShapeDtypeStruct((b, nr // G_B, n_blk), F32)],
        compiler_params=pltpu.CompilerParams(dimension_semantics=("arbitrary",),
                                             vmem_limit_bytes=VMEM_LIMIT),
        name="sample_compressed",
    )(page_table, kc_pages, vc_pages, qk, *consts)


def _topk_lanes_kernel(imp_ref, out_ref, *, n_blocks, cur, k):
    imp = imp_ref[...]
    bi = lax.broadcasted_iota(jnp.int32, imp.shape, 1).astype(F32)
    forced = jnp.logical_or(bi == 0, bi > cur - N_LOCAL)
    score = jnp.where(bi > cur, -1e9, jnp.where(forced, 1e9, imp))
    score = jnp.where(bi < n_blocks, score, -jnp.inf)
    width = imp.shape[1]
    sel = jnp.zeros(imp.shape, jnp.bool_)
    for _ in range(k):
        m = jnp.max(score, axis=-1, keepdims=True)
        idx = jnp.min(jnp.where(score == m, bi, float(width)), axis=-1, keepdims=True)
        hit = bi == idx
        sel = jnp.logical_or(sel, hit)
        score = jnp.where(hit, -jnp.inf, score)
    out_ref[...] = jnp.where(sel, 0.0, NEG)


def _sample_topk(imp2d, n_blocks, cur):
    n, w = imp2d.shape
    tr = 256 if n % 256 == 0 else n
    blk = pl.BlockSpec((tr, w), lambda i: (i, 0))
    return pl.pallas_call(
        functools.partial(_topk_lanes_kernel, n_blocks=n_blocks, cur=cur, k=min(N_SEL, n_blocks)),
        grid=(n // tr,),
        in_specs=[blk],
        out_specs=blk,
        out_shape=jax.ShapeDtypeStruct((n, w), F32),
        compiler_params=pltpu.CompilerParams(dimension_semantics=("arbitrary",)),
        name="sample_topk",
    )(imp2d)


def _shift_in(st_ref, tail_ref, out_ref, n_new):
    x = st_ref[0]
    n_buf = x.shape[1]
    rolled = pltpu.roll(x, n_buf - n_new, 1)
    lane = lax.broadcasted_iota(jnp.int32, (x.shape[0], LANES), 1)
    last = jnp.where(lane >= LANES - n_new, tail_ref[0], rolled[:, n_buf - LANES:])
    out_ref[0] = jnp.concatenate([rolled[:, :n_buf - LANES], last], axis=1)


def _ssel_kernel(pt_ref, ks_hbm, vs_hbm, qsel_ref, knewt_ref, vnew_ref, kwst_ref, vwst_ref, kwnewt_ref,
                 vwnew_ref, kwtail_ref, vwtail_ref, oc_ref, gsel_ref, slope_ref, irow_ref,
                 o_ref, kwout_ref, vwout_ref, buf, sem, s_sc, oh_sc, *, past_len, n_new):
    b = pl.program_id(0)
    _shift_in(kwst_ref, kwtail_ref, kwout_ref, n_new)
    _shift_in(vwst_ref, vwtail_ref, vwout_ref, n_new)
    n_pages, _, page = buf.shape[2:]
    nr = qsel_ref.shape[1]
    ppt = next(c for c in (4, 2, 1) if n_pages % c == 0)
    tile = ppt * page

    @pl.when(b == 0)
    def _():
        bi = lax.broadcasted_iota(jnp.int32, oh_sc.shape, 0)
        ki = lax.broadcasted_iota(jnp.int32, oh_sc.shape, 1)
        oh_sc[...] = jnp.where(jnp.right_shift(ki, SEL_BLK.bit_length() - 1) == bi, 1.0, 0.0).astype(BF16)

    slot = _gather_pages(pt_ref, ks_hbm, vs_hbm, buf, sem)
    slope = slope_ref[:, 0:1]
    irow = irow_ref[:, 0:1]
    qpos = past_len + irow
    qall = qsel_ref[0]
    qk = qall[:, :KV_W]

    def key_tile(which, j):
        return jnp.concatenate([buf[slot, which, ppt * j + i] for i in range(ppt)], axis=1).astype(BF16)

    lane = lax.broadcasted_iota(jnp.int32, (nr, tile), 1).astype(F32)
    mrun = jnp.full((nr, tile), NEG, F32)
    for j in range(n_pages // ppt):
        kcat = jnp.concatenate([key_tile(0, j), oh_sc[:, j * tile:(j + 1) * tile]], axis=0)
        s = _dot(qall, kcat) - slope * (qpos - (float(j * tile) + lane))
        s_sc[:, j * tile:(j + 1) * tile] = s
        mrun = jnp.maximum(mrun, s)
    nl = lax.broadcasted_iota(jnp.int32, (nr, LANES), 1).astype(F32)
    dn = irow - nl
    s_new = jnp.where(dn >= 0.0, _dot(qk, knewt_ref[0]) - slope * dn, NEG)
    m = jnp.maximum(jnp.max(mrun, axis=-1, keepdims=True), jnp.max(s_new, axis=-1, keepdims=True))

    p_new = jnp.exp(s_new - m)
    acc = _dot(p_new.astype(BF16), vnew_ref[0])
    lrun = jnp.zeros((nr, tile), F32)
    for j in range(n_pages // ppt):
        p = jnp.exp(s_sc[:, j * tile:(j + 1) * tile] - m)
        acc = acc + _dot_nt(p.astype(BF16), key_tile(1, j))
        lrun = lrun + p
    l = jnp.sum(lrun, axis=-1, keepdims=True) + jnp.sum(p_new, axis=-1, keepdims=True)
    o_s = acc * (1.0 / l)

    n_buf = kwst_ref.shape[2]
    wl = lax.broadcasted_iota(jnp.int32, (nr, n_buf), 1).astype(F32)
    dw = (n_buf + irow) - wl
    s_w = jnp.where(jnp.logical_and(dw >= 0.0, dw < WINDOW),
                    _dot(qk, kwst_ref[0].astype(BF16)) - slope * dw, NEG)
    s_wn = jnp.where(dn >= 0.0, _dot(qk, kwnewt_ref[0]) - slope * dn, NEG)
    m_w = jnp.maximum(jnp.max(s_w, axis=-1, keepdims=True), jnp.max(s_wn, axis=-1, keepdims=True))
    p_w = jnp.exp(s_w - m_w)
    p_wn = jnp.exp(s_wn - m_w)
    l_w = jnp.sum(p_w, axis=-1, keepdims=True) + jnp.sum(p_wn, axis=-1, keepdims=True)
    o_w = (_dot_nt(p_w.astype(BF16), vwst_ref[0].astype(BF16))
           + _dot(p_wn.astype(BF16), vwnew_ref[0])) * (1.0 / l_w)

    gsel = gsel_ref[0]
    o_ref[0] = gsel[:, 0:1] * oc_ref[0] + gsel[:, 1:2] * o_s + gsel[:, 2:3] * o_w


def _sample_selected(page_table, ks_pages, vs_pages, qsel, knewt, vnew, kwst, vwst, kwnewt, vwnew, kwtail,
                     vwtail, oc, gsel, slope, irow, past_len, n_new):
    b, n_pages = page_table.shape
    page = ks_pages.shape[2]
    n_keys = n_pages * page
    nr = qsel.shape[1]
    assert kwst.shape[2] % LANES == 0 and n_new <= LANES
    per_b = lambda a: pl.BlockSpec((1,) + a.shape[1:], lambda i, pt: (i,) + (0,) * (a.ndim - 1))
    full = lambda a: pl.BlockSpec(a.shape, lambda i, pt: (0,) * a.ndim)
    grid_spec = pltpu.PrefetchScalarGridSpec(
        num_scalar_prefetch=1,
        grid=(b,),
        in_specs=[pl.BlockSpec(memory_space=pl.ANY), pl.BlockSpec(memory_space=pl.ANY)]
                 + [per_b(a) for a in (qsel, knewt, vnew, kwst, vwst, kwnewt, vwnew, kwtail, vwtail, oc, gsel)]
                 + [full(slope), full(irow)],
        out_specs=[pl.BlockSpec((1, nr, KV_W), lambda i, pt: (i, 0, 0)), per_b(kwst), per_b(vwst)],
        scratch_shapes=[pltpu.VMEM((2, 2, n_pages, KV_W, page), F32),
                        pltpu.SemaphoreType.DMA((2, 2)),
                        pltpu.VMEM((nr, n_keys), F32),
                        pltpu.VMEM((n_keys // SEL_BLK, n_keys), BF16)],
    )
    return pl.pallas_call(
        functools.partial(_ssel_kernel, past_len=float(past_len), n_new=n_new),
        grid_spec=grid_spec,
        out_shape=[jax.ShapeDtypeStruct((b, nr, KV_W), F32),
                   jax.ShapeDtypeStruct(kwst.shape, F32), jax.ShapeDtypeStruct(vwst.shape, F32)],
        compiler_params=pltpu.CompilerParams(dimension_semantics=("arbitrary",),
                                             vmem_limit_bytes=VMEM_LIMIT),
        name="sample_selected",
    )(page_table, ks_pages, vs_pages, qsel, knewt, vnew, kwst, vwst, kwnewt, vwnew, kwtail, vwtail, oc,
      gsel, slope, irow)


def _slopes():
    return np.asarray([2.0 ** -(h + 1) for h in range(H_B)], np.float32)


def _block_ones(n):
    idx = np.arange(n) // HD
    return jnp.asarray(idx[:, None] == idx[None, :], BF16)


def _layer_weights(l, t_chunk, norm_in, w_in, b_gate, ln_v_g, ln_v_b, w_spatial, b_spatial, q_norm,
                   k_norm_cmp, k_norm_sel, k_norm_win, cmp_pe_k, cmp_pe_v, w_cmp_k1, w_cmp_k2,
                   w_cmp_v1, w_cmp_v2, out_norm_a, out_norm_b, w_out):
    w = w_in[l]
    n_main = 3 * D_A + 2 * D_B
    n_gate = 3 * H_B
    w_pack = jnp.concatenate([w[:, :n_main], w[:, n_main + n_gate:], w[:, n_main:n_main + n_gate],
                              jnp.zeros((D_MODEL, N_PACK - w.shape[1]), F32)], axis=1).astype(BF16)
    c = t_chunk
    rep = CHUNK // c
    tri = jnp.where(jnp.tril(jnp.ones((c, c), bool)), w_spatial[l][:, :c, :c], 0.0)
    wc = jnp.einsum("ab,hij->haibj", jnp.eye(rep, dtype=F32), tri).reshape(H_A, CHUNK, CHUNK).astype(BF16)
    bc = jnp.tile(jnp.repeat(b_spatial[l][:, :c].T, D_A // H_A, axis=1), (rep, 1))

    def big1(w1):
        w1r = w1.reshape(2, CMP_STRIDE, HD, CMP_HID)
        return jnp.einsum("ajdh,ck->ajcdkh", w1r, jnp.eye(KV_B, dtype=F32)).reshape(
            2, CMP_STRIDE * KV_W, KV_B * CMP_HID).astype(BF16)

    def big2(w2):
        return jnp.einsum("hd,kc->khcd", w2, jnp.eye(KV_B, dtype=F32)).reshape(
            KV_B * CMP_HID, KV_W).astype(BF16)

    def pe_rows(pe):
        halves = pe.reshape(2, CMP_STRIDE, 1, HD)
        flat = jnp.broadcast_to(halves, (2, CMP_STRIDE, KV_B, HD)).reshape(2, 1, CMP_STRIDE * KV_W)
        flat = jnp.broadcast_to(flat, (2, 8, CMP_STRIDE * KV_W)).astype(BF16)
        return flat[0], flat[1]

    peak, pebk = pe_rows(cmp_pe_k[l])
    peav, pebv = pe_rows(cmp_pe_v[l])
    row = lambda a, reps: jnp.tile(a, reps)[None, :]
    bg = jnp.concatenate([b_gate[l], jnp.zeros((LANES - n_gate,), F32)])[None, :]
    return dict(
        nin=norm_in[l][None, :], w_in=w_pack, bg=bg, lng=ln_v_g[l][None, :], lnb=ln_v_b[l][None, :],
        wc=wc, bc=bc, qn=row(q_norm[l], H_B), ksn=row(k_norm_sel[l], KV_B), kwn=row(k_norm_win[l], KV_B),
        kcn=row(k_norm_cmp[l], KV_B), ona=out_norm_a[l][None, :], onb=out_norm_b[l][None, :],
        g512=_block_ones(D_B), g128=_block_ones(KV_W),
        w1k=big1(w_cmp_k1[l]), w1v=big1(w_cmp_v1[l]), w2k=big2(w_cmp_k2[l]), w2v=big2(w_cmp_v2[l]),
        peak=peak, pebk=pebk, peav=peav, pebv=pebv, w_out=w_out[l].astype(BF16))


def _prompt_layer(x, wts):
    b, t, _ = x.shape
    n = b * t
    x2d = x.reshape(n, D_MODEL)
    (oa, szb, qt, gt, ksa, kwa, kc_r, vc_r,
     kct, vct, kst, vst, kwt, vwt) = _projection_prompt(x2d, b, t, wts)
    kca, vcct = _compress_prompt(kc_r, vc_r, b, wts)
    ob = _prompt_attention(qt, gt, ksa, vst, kwa, vwt, kca, vcct)
    y = _merge(x2d, oa, ob, szb, wts).reshape(b, t, D_MODEL)
    heads = lambda a: jnp.transpose(a.reshape(b, KV_B, HD, a.shape[2]), (0, 3, 1, 2))
    n_keep = min(WINDOW, t)
    return y, (heads(kct), heads(vct), heads(kst), heads(vst),
               heads(kwt[:, :, t - n_keep:]), heads(vwt[:, :, t - n_keep:]))


def _sample_layer(x, l, cache_k_cmp, cache_v_cmp, cache_k_sel, cache_v_sel, k_win_buf, v_win_buf,
                  page_table, wts):
    b, t, _ = x.shape
    n = b * t
    n_pool, page = cache_k_cmp.shape[1], cache_k_cmp.shape[2]
    n_pages = page_table.shape[1]
    past_len = n_pages * page
    x2d = x.reshape(n, D_MODEL)
    oa, vn, q, szb, gates, kc_r, vc_r, ks, vs, kw, vw = _projection_sample(x2d, wts)

    nr = H_B * t
    sl = _slopes()
    slope = jnp.asarray(np.broadcast_to(np.repeat(sl, t)[:, None], (nr, LANES)).copy())
    irow = jnp.asarray(np.broadcast_to(np.tile(np.arange(t, dtype=np.float32), H_B)[:, None], (nr, LANES)).copy())
    qh = jnp.transpose((q * SCALE).reshape(b, t, KV_B, G_B, HD), (0, 2, 3, 1, 4))
    qk = jnp.einsum("bghtd,gk->bghtkd", qh, jnp.eye(KV_B, dtype=F32)).reshape(b, nr, KV_W).astype(BF16)

    minor_pos = lambda a: jnp.transpose(a, (0, 2, 3, 1)).reshape(a.shape[0], KV_W, a.shape[1])
    oc, imp = _sample_compressed(page_table, minor_pos(cache_k_cmp[l]), minor_pos(cache_v_cmp[l]),
                                 qk, slope, irow, wts, past_len)

    n_sb = -(-(past_len + t) // SEL_BLK)
    n_pb = past_len // SEL_BLK
    cur = past_len // SEL_BLK
    wpad = imp.shape[2]
    bias = _sample_topk(imp.reshape(b * KV_B * t, wpad), n_sb, cur).reshape(b, KV_B, 1, t, wpad)
    bias = jnp.broadcast_to(bias[..., :n_pb], (b, KV_B, G_B, t, n_pb)).reshape(b, nr, n_pb)
    qsel = jnp.concatenate([qk, bias.astype(BF16)], axis=-1)

    pad_rows = lambda a: jnp.pad(a.reshape(b, t, KV_W).astype(BF16), ((0, 0), (0, LANES - t), (0, 0)))
    pad_cols = lambda a: jnp.transpose(pad_rows(a), (0, 2, 1))
    gsel = jnp.transpose(gates[:, :3 * H_B].reshape(b, t, 3, H_B), (0, 3, 1, 2)).reshape(b, nr, 3)
    gsel = jnp.pad(gsel, ((0, 0), (0, 0), (0, LANES - 3)))
    tail = lambda a: jnp.pad(jnp.transpose(a.reshape(b, t, KV_W), (0, 2, 1)), ((0, 0), (0, 0), (LANES - t, 0)))
    o, kw_next, vw_next = _sample_selected(
        page_table, minor_pos(cache_k_sel[l]), minor_pos(cache_v_sel[l]), qsel, pad_cols(ks), pad_rows(vs),
        minor_pos(k_win_buf), minor_pos(v_win_buf), pad_cols(kw), pad_rows(vw), tail(kw), tail(vw), oc, gsel,
        slope, irow, past_len, t)
    o5 = o.reshape(b, KV_B, G_B, t, KV_B, HD)
    ob = jnp.stack([o5[:, g, :, :, g, :] for g in range(KV_B)], axis=1)
    ob = jnp.transpose(ob, (0, 3, 1, 2, 4)).reshape(n, D_B)
    y = _merge(x2d, oa, ob, szb, wts).reshape(b, t, D_MODEL)
    heads = lambda a: a.reshape(b, t, KV_B, HD)
    major_pos = lambda a: jnp.transpose(a.reshape(b, KV_B, HD, a.shape[2]), (0, 3, 1, 2))
    return y, (heads(kc_r), heads(vc_r), heads(ks), heads(vs), major_pos(kw_next), major_pos(vw_next),
               vn.reshape(b, t, D_A))


def kernel(x_prompt, x_sample, cache_k_cmp, cache_v_cmp, cache_k_sel, cache_v_sel, state_k_win, state_v_win, page_table, norm_in, w_in, b_gate, ln_v_g, ln_v_b, w_spatial, b_spatial, q_norm, k_norm_cmp, k_norm_sel, k_norm_win, cmp_pe_k, cmp_pe_v, w_cmp_k1, w_cmp_k2, w_cmp_v1, w_cmp_v2, out_norm_a, out_norm_b, w_out):
    params = (norm_in, w_in, b_gate, ln_v_g, ln_v_b, w_spatial, b_spatial, q_norm, k_norm_cmp,
              k_norm_sel, k_norm_win, cmp_pe_k, cmp_pe_v, w_cmp_k1, w_cmp_k2, w_cmp_v1, w_cmp_v2,
              out_norm_a, out_norm_b, w_out)
    depth = w_in.shape[0]
    h_p, h_s = x_prompt, x_sample
    st_p, st_s = [], []
    for l in range(depth):
        wts_p = _layer_weights(l, min(h_p.shape[1], CHUNK), *params)
        wts_s = _layer_weights(l, min(h_s.shape[1], CHUNK), *params)
        h_p, sp = _prompt_layer(h_p, wts_p)
        h_s, ss = _sample_layer(h_s, l, cache_k_cmp, cache_v_cmp, cache_k_sel, cache_v_sel,
                                state_k_win[l], state_v_win[l], page_table, wts_s)
        st_p.append(sp)
        st_s.append(ss)
    new_p = [jnp.stack(ts) for ts in zip(*st_p)]
    new_s = [jnp.stack(ts) for ts in zip(*st_s)]
    return (h_p, h_s, new_p[0], new_p[1], new_p[2], new_p[3], new_p[4], new_p[5],
            new_s[0], new_s[1], new_s[2], new_s[3], new_s[4], new_s[5], new_s[6])
```
